```python
import math
import jax, jax.numpy as jnp
from jax import lax
import numpy as np

D_MODEL = 4096
BATCH = 4
SEQ = 4096
DEPTH = 1

ATT_HEADS = 16
ATT_KV_HEADS = 4
HEAD_DIM = 128
IDX_HEADS = 16
IDX_DIM = 64
TOPK_MAX = 256
Q_BLOCK = 128
ROPE_THETA = 10000.0
GDN_HEADS = 16
GDN_DK = 128
GDN_DV = 128
CONV_WIDTH = 4
CHUNK = 64
N_GROUPS = 4
EXPERTS_PER_GROUP = 8
N_EXPERTS = N_GROUPS * EXPERTS_PER_GROUP
TOP_K_EXPERTS = 2
EXPERT_FF = 512
EPS = 1e-6

ATT_Q_WIDTH = ATT_HEADS * HEAD_DIM
ATT_KV_WIDTH = ATT_KV_HEADS * HEAD_DIM
IDX_Q_WIDTH = IDX_HEADS * IDX_DIM
GDN_QK_WIDTH = GDN_HEADS * GDN_DK
GDN_V_WIDTH = GDN_HEADS * GDN_DV
IN_SPLIT_SIZES = (ATT_Q_WIDTH, ATT_KV_WIDTH, ATT_KV_WIDTH, IDX_Q_WIDTH, IDX_DIM, IDX_HEADS,
                  GDN_QK_WIDTH, GDN_QK_WIDTH, GDN_V_WIDTH, GDN_V_WIDTH, GDN_HEADS, GDN_HEADS,
                  D_MODEL, D_MODEL)
IN_WIDTH = sum(IN_SPLIT_SIZES)

kernel_name = 'hybrid_dsa_gdn_hmoe_block'


def rms_norm(x, w):
    xf = x.astype(jnp.float32)
    y = xf * lax.rsqrt(jnp.mean(xf * xf, axis=-1, keepdims=True) + EPS)
    return (y * w.astype(jnp.float32)).astype(x.dtype)


def l2_norm(x):
    xf = x.astype(jnp.float32)
    return xf * lax.rsqrt(jnp.sum(xf * xf, axis=-1, keepdims=True) + EPS)


def rope(x, positions):
    d = x.shape[-1]
    inv_freq = ROPE_THETA ** (-jnp.arange(0, d, 2, dtype=jnp.float32) / d)
    ang = positions.astype(jnp.float32)[..., None] * inv_freq
    if x.ndim == 4:
        ang = ang[:, :, None, :]
    cos, sin = jnp.cos(ang), jnp.sin(ang)
    xf = x.astype(jnp.float32)
    x1, x2 = xf[..., : d // 2], xf[..., d // 2:]
    return jnp.concatenate([x1 * cos - x2 * sin, x2 * cos + x1 * sin], axis=-1).astype(x.dtype)


def dsa_sparse_attention(q, k, v, q_idx, k_idx, w_idx):
    B, S = q.shape[:2]
    n_sel = min(TOPK_MAX, S // 4)
    nb = S // Q_BLOCK
    grp = ATT_HEADS // ATT_KV_HEADS
    k_idx32 = k_idx.astype(jnp.float32)
    key_pos = jnp.arange(S, dtype=jnp.int32)

    def to_blocks(a):
        return jnp.moveaxis(a.reshape((B, nb, Q_BLOCK) + a.shape[2:]), 1, 0)

    def block(args):
        start, qb, qib, wb = args
        q_pos = start + jnp.arange(Q_BLOCK, dtype=jnp.int32)
        causal = key_pos[None, :] <= q_pos[:, None]
        dots = jnp.einsum('bqhd,bsd->bqhs', qib.astype(jnp.float32), k_idx32) * (IDX_DIM ** -0.5)
        score = jnp.einsum('bqh,bqhs->bqs', wb.astype(jnp.float32), jax.nn.relu(dots))
        score = jnp.where(causal[None], score, -jnp.inf)
        _, sel = lax.top_k(score, n_sel)
        valid = sel <= q_pos[None, :, None]
        kg = jax.vmap(lambda kk, ii: kk[ii])(k, sel)
        vg = jax.vmap(lambda vv, ii: vv[ii])(v, sel)
        qg = qb.reshape(B, Q_BLOCK, ATT_KV_HEADS, grp, HEAD_DIM)
        s = jnp.einsum('bqhgd,bqkhd->bqhgk', qg.astype(jnp.float32), kg.astype(jnp.float32)) * (HEAD_DIM ** -0.5)
        s = jnp.where(valid[:, :, None, None, :], s, -jnp.inf)
        p = jax.nn.softmax(s, axis=-1).astype(v.dtype)
        o = jnp.einsum('bqhgk,bqkhd->bqhgd', p, vg)
        return o.reshape(B, Q_BLOCK, ATT_Q_WIDTH)

    starts = jnp.arange(nb, dtype=jnp.int32) * Q_BLOCK
    out = lax.map(block, (starts, to_blocks(q), to_blocks(q_idx), to_blocks(w_idx)))
    return jnp.moveaxis(out, 0, 1).reshape(B, S, ATT_Q_WIDTH)


def causal_conv_silu(x, w):
    S = x.shape[1]
    xp = jnp.pad(x, ((0, 0), (CONV_WIDTH - 1, 0), (0, 0)))
    y = w[0] * xp[:, 0:S]
    for j in range(1, CONV_WIDTH):
        y = y + w[j] * xp[:, j:j + S]
    return jax.nn.silu(y)


def gated_delta_rule(q, k, v, g, beta):
    B, S, H, DK = q.shape
    DV = v.shape[-1]
    n = S // CHUNK

    def chunks(a):
        a = jnp.moveaxis(a, 2, 1)
        return a.reshape((B, H, n, CHUNK) + a.shape[3:])

    q, k, v, g, beta = (chunks(a) for a in (q, k, v, g, beta))
    gc = jnp.cumsum(g, axis=-1)
    tril = jnp.tril(jnp.ones((CHUNK, CHUNK), dtype=bool))
    strict = jnp.tril(jnp.ones((CHUNK, CHUNK), dtype=bool), -1)
    decay = jnp.exp(jnp.where(tril, gc[..., :, None] - gc[..., None, :], -jnp.inf))
    k_beta = k * beta[..., None]
    v_beta = v * beta[..., None]
    lower = jnp.where(strict, jnp.einsum('bhnid,bhnjd->bhnij', k_beta, k) * decay, 0.0)
    rhs = jnp.concatenate([v_beta, k_beta * jnp.exp(gc)[..., None]], axis=-1)
    sol = lax.linalg.triangular_solve(lower, rhs, left_side=True, lower=True, unit_diagonal=True)
    u, w = sol[..., :DV], sol[..., DV:]
    intra = jnp.einsum('bhnid,bhnjd->bhnij', q, k) * decay
    q_dec = q * jnp.exp(gc)[..., None]
    g_last = gc[..., -1]
    k_dec = k * jnp.exp(g_last[..., None] - gc)[..., None]

    def step(state, xs):
        q_d, k_d, u_c, w_c, intra_c, gl = xs
        v_new = u_c - jnp.einsum('bhck,bhkv->bhcv', w_c, state)
        o = jnp.einsum('bhck,bhkv->bhcv', q_d, state) + jnp.einsum('bhij,bhjv->bhiv', intra_c, v_new)
        state = state * jnp.exp(gl)[..., None, None] + jnp.einsum('bhck,bhcv->bhkv', k_d, v_new)
        return state, o

    xs = tuple(jnp.moveaxis(a, 2, 0) for a in (q_dec, k_dec, u, w, intra, g_last))
    state0 = jnp.zeros((B, H, DK, DV), jnp.float32)
    _, o = lax.scan(step, state0, xs)
    o = jnp.moveaxis(o, 0, 2).reshape(B, H, S, DV)
    return jnp.moveaxis(o, 1, 2)


def gated_deltanet(q, k, v, z, a, b, conv_w, a_log, dt_bias, norm_w):
    B, S, _ = q.shape
    qkv = causal_conv_silu(jnp.concatenate([q, k, v], axis=-1), conv_w)
    q, k, v = jnp.split(qkv, [GDN_QK_WIDTH, 2 * GDN_QK_WIDTH], axis=-1)
    q = l2_norm(q.reshape(B, S, GDN_HEADS, GDN_DK)) * (GDN_DK ** -0.5)
    k = l2_norm(k.reshape(B, S, GDN_HEADS, GDN_DK))
    v = v.reshape(B, S, GDN_HEADS, GDN_DV).astype(jnp.float32)
    beta = jax.nn.sigmoid(b.astype(jnp.float32))
    g = -jnp.exp(a_log.astype(jnp.float32)) * jax.nn.softplus(a.astype(jnp.float32) + dt_bias.astype(jnp.float32))
    o = gated_delta_rule(q, k, v, g, beta)
    o = rms_norm(o, norm_w) * jax.nn.silu(z.reshape(B, S, GDN_HEADS, GDN_DV).astype(jnp.float32))
    return o.reshape(B, S, GDN_V_WIDTH).astype(z.dtype)


def hierarchical_moe(h, w_rg, b_rg, w_re, b_re, w_gate, w_up, w_down):
    B, S, D = h.shape
    T = B * S
    t = h.reshape(T, D)
    grp_prob = jax.nn.softmax(jnp.matmul(t, w_rg).astype(jnp.float32) + b_rg.astype(jnp.float32), axis=-1)
    p_grp, grp = lax.top_k(grp_prob, 1)
    exp_logits = jnp.einsum('td,gde->tge', t, w_re).astype(jnp.float32) + b_re.astype(jnp.float32)
    exp_logits = exp_logits[jnp.arange(T), grp[:, 0]]
    p_exp, sel = lax.top_k(jax.nn.softmax(exp_logits, axis=-1), TOP_K_EXPERTS)
    weights = p_grp * p_exp / jnp.sum(p_exp, axis=-1, keepdims=True)
    eid = grp * EXPERTS_PER_GROUP + sel
    combine = jnp.sum(jax.nn.one_hot(eid, N_EXPERTS, dtype=jnp.float32) * weights[..., None], axis=1)
    y = jnp.zeros((T, D), jnp.float32)
    for e in range(N_EXPERTS):
        he = jax.nn.silu(t @ w_gate[e]) * (t @ w_up[e])
        y = y + combine[:, e:e + 1] * (he @ w_down[e]).astype(jnp.float32)
    return y.astype(h.dtype).reshape(B, S, D)


def setup_inputs(seed: int = 0) -> dict:
    key = jax.random.key(seed)
    ks = jax.random.split(key, 24)
    f32 = jnp.float32

    def nrm(k, shape, scale):
        return jax.random.normal(k, shape, f32) * scale

    def gain(k, shape):
        return 1.0 + 0.02 * jax.random.normal(k, shape, f32)

    L = DEPTH
    x = nrm(ks[0], (BATCH, SEQ, D_MODEL), 1.0)
    offs = jax.random.randint(ks[1], (BATCH, 1), 0, 1024, dtype=jnp.int32)
    positions = offs + jnp.arange(SEQ, dtype=jnp.int32)[None, :]
    dt = jnp.exp(jax.random.uniform(ks[9], (L, GDN_HEADS), f32) * (math.log(0.1) - math.log(1e-3)) + math.log(1e-3))
    return {
        'x': x,
        'positions': positions,
        'mix_norm_w': gain(ks[2], (L, D_MODEL)),
        'w_in': nrm(ks[3], (L, D_MODEL, IN_WIDTH), D_MODEL ** -0.5),
        'q_norm_w': gain(ks[4], (L, HEAD_DIM)),
        'k_norm_w': gain(ks[5], (L, HEAD_DIM)),
        'idx_k_norm_w': gain(ks[6], (L, IDX_DIM)),
        'conv_w': nrm(ks[7], (L, CONV_WIDTH, 2 * GDN_QK_WIDTH + GDN_V_WIDTH), 0.5),
        'a_log': jnp.log(jax.random.uniform(ks[8], (L, GDN_HEADS), f32, 1.0, 16.0)),
        'dt_bias': dt + jnp.log(-jnp.expm1(-dt)),
        'gdn_norm_w': gain(ks[10], (L, GDN_DV)),
        'w_proj_attn': nrm(ks[11], (L, ATT_Q_WIDTH, D_MODEL), ATT_Q_WIDTH ** -0.5),
        'w_proj_gdn': nrm(ks[12], (L, GDN_V_WIDTH, D_MODEL), GDN_V_WIDTH ** -0.5),
        'w_out': nrm(ks[13], (L, D_MODEL, D_MODEL), D_MODEL ** -0.5),
        'ffn_norm_w': gain(ks[14], (L, D_MODEL)),
        'w_router_group': nrm(ks[15], (L, D_MODEL, N_GROUPS), D_MODEL ** -0.5),
        'b_router_group': nrm(ks[16], (L, N_GROUPS), 0.01),
        'w_router_expert': nrm(ks[17], (L, N_GROUPS, D_MODEL, EXPERTS_PER_GROUP), D_MODEL ** -0.5),
        'b_router_expert': nrm(ks[18], (L, N_GROUPS, EXPERTS_PER_GROUP), 0.01),
        'w_gate': nrm(ks[19], (L, N_EXPERTS, D_MODEL, EXPERT_FF), D_MODEL ** -0.5),
        'w_up': nrm(ks[20], (L, N_EXPERTS, D_MODEL, EXPERT_FF), D_MODEL ** -0.5),
        'w_down': nrm(ks[21], (L, N_EXPERTS, EXPERT_FF, D_MODEL), EXPERT_FF ** -0.5),
    }


def reference(x, positions, mix_norm_w, w_in, q_norm_w, k_norm_w, idx_k_norm_w, conv_w, a_log, dt_bias,
              gdn_norm_w, w_proj_attn, w_proj_gdn, w_out, ffn_norm_w, w_router_group, b_router_group,
              w_router_expert, b_router_expert, w_gate, w_up, w_down):
    B, S, _ = x.shape
    split_idx = np.cumsum(IN_SPLIT_SIZES)[:-1].tolist()
    for l in range(DEPTH):
        h = rms_norm(x, mix_norm_w[l])
        proj = h @ w_in[l]
        (aq, ak, av, iq, ik, iw, gq, gk, gv, gz, ga, gb, gate_attn, gate_gdn) = jnp.split(proj, split_idx, axis=-1)
        aq = rope(rms_norm(aq.reshape(B, S, ATT_HEADS, HEAD_DIM), q_norm_w[l]), positions)
        ak = rope(rms_norm(ak.reshape(B, S, ATT_KV_HEADS, HEAD_DIM), k_norm_w[l]), positions)
        av = av.reshape(B, S, ATT_KV_HEADS, HEAD_DIM)
        iq = rope(iq.reshape(B, S, IDX_HEADS, IDX_DIM), positions)
        ik = rope(rms_norm(ik, idx_k_norm_w[l]), positions)
        iw = iw * (IDX_HEADS ** -0.5)
        y_attn = dsa_sparse_attention(aq, ak, av, iq, ik, iw)
        y_gdn = gated_deltanet(gq, gk, gv, gz, ga, gb, conv_w[l], a_log[l], dt_bias[l], gdn_norm_w[l])
        mixed = (jax.nn.sigmoid(gate_attn) * (y_attn @ w_proj_attn[l])
                 + jax.nn.sigmoid(gate_gdn) * (y_gdn @ w_proj_gdn[l]))
        x = x + mixed @ w_out[l]
        h2 = rms_norm(x, ffn_norm_w[l])
        x = x + hierarchical_moe(h2, w_router_group[l], b_router_group[l], w_router_expert[l],
                                 b_router_expert[l], w_gate[l], w_up[l], w_down[l])
    return x
```

```python
import functools
import math

import jax
import jax.numpy as jnp
from jax import lax
from jax.experimental import pallas as pl
from jax.experimental.pallas import tpu as pltpu

ATT_HEADS = 16
ATT_KV_HEADS = 4
HEAD_DIM = 128
IDX_HEADS = 16
IDX_DIM = 64
TOPK_MAX = 256
ROPE_THETA = 10000.0
GDN_HEADS = 16
GDN_DK = 128
GDN_DV = 128
CONV_WIDTH = 4
CHUNK = 64
N_GROUPS = 4
EXPERTS_PER_GROUP = 8
N_EXPERTS = N_GROUPS * EXPERTS_PER_GROUP
TOP_K_EXPERTS = 2
EPS = 1e-6

ATT_Q_WIDTH = ATT_HEADS * HEAD_DIM
ATT_KV_WIDTH = ATT_KV_HEADS * HEAD_DIM
IDX_Q_WIDTH = IDX_HEADS * IDX_DIM
GDN_WIDTH = GDN_HEADS * GDN_DK

LANES = 128
VMEM_LIMIT = 56 * 1024 * 1024
NEG_BIG = -1e30
LOG2E = math.log2(math.e)
SCORE_MASKED = 3.0e38
BISECT_MAX_STEPS = 192
BISECT_UNROLL = 4

OFF_AQ = 0
OFF_AK = OFF_AQ + ATT_Q_WIDTH
OFF_AV = OFF_AK + ATT_KV_WIDTH
OFF_IQ = OFF_AV + ATT_KV_WIDTH
OFF_GQ = OFF_IQ + IDX_Q_WIDTH
OFF_GK = OFF_GQ + GDN_WIDTH
OFF_GV = OFF_GK + GDN_WIDTH
OFF_GZ = OFF_GV + GDN_WIDTH
OFF_GATES = OFF_GZ + GDN_WIDTH
SM_IK = 0
SM_IW = SM_IK + IDX_DIM
SM_GA = SM_IW + IDX_HEADS
SM_GB = SM_GA + GDN_HEADS
RT_EXP = 8


def _cparams(sem):
    return pltpu.CompilerParams(dimension_semantics=sem, vmem_limit_bytes=VMEM_LIMIT)


def _tile(n, pref):
    t = min(n, pref)
    assert n % t == 0, (n, pref)
    return t


def _rmsnorm_kernel(x_ref, w_ref, o_ref):
    x = x_ref[...]
    ms = jnp.mean(x * x, axis=-1, keepdims=True)
    o_ref[...] = (x * lax.rsqrt(ms + EPS) * w_ref[...]).astype(o_ref.dtype)


def rmsnorm(x, w, out_dtype, tm=256):
    T, D = x.shape
    tm = _tile(T, tm)
    return pl.pallas_call(
        _rmsnorm_kernel,
        out_shape=jax.ShapeDtypeStruct((T, D), out_dtype),
        grid=(T // tm,),
        in_specs=[pl.BlockSpec((tm, D), lambda i: (i, 0)), pl.BlockSpec((1, D), lambda i: (0, 0))],
        out_specs=pl.BlockSpec((tm, D), lambda i: (i, 0)),
        compiler_params=_cparams(("parallel",)),
        name="rmsnorm",
    )(x, w.reshape(1, D))


def _matmul_kernel(a_ref, b_ref, o_ref):
    o_ref[...] = jnp.dot(a_ref[...], b_ref[...], preferred_element_type=jnp.float32).astype(o_ref.dtype)


def matmul(a, b, out_dtype, tm=1024, tn=1024, name="matmul"):
    M, K = a.shape
    _, N = b.shape
    tm, tn = _tile(M, tm), _tile(N, tn)
    return pl.pallas_call(
        _matmul_kernel,
        out_shape=jax.ShapeDtypeStruct((M, N), out_dtype),
        grid=(N // tn, M // tm),
        in_specs=[pl.BlockSpec((tm, K), lambda j, i: (i, 0)), pl.BlockSpec((K, tn), lambda j, i: (0, j))],
        out_specs=pl.BlockSpec((tm, tn), lambda j, i: (i, j)),
        compiler_params=_cparams(("parallel", "parallel")),
        name=name,
    )(a, b)


def _rope_table_kernel(pos_ref, inv_ref, sgn_ref, cos_ref, sin_ref):
    ang = pos_ref[...] * inv_ref[...]
    cos_ref[...] = jnp.cos(ang)
    sin_ref[...] = jnp.sin(ang) * sgn_ref[...]


def rope_tables(positions):
    T = positions.size
    pos = positions.reshape(T, 1).astype(jnp.float32)

    def inv(d):
        return ROPE_THETA ** (-jnp.arange(0, d, 2, dtype=jnp.float32) / d)

    i128, i64 = inv(HEAD_DIM), inv(IDX_DIM)
    inv_row = jnp.concatenate([i128, i128, i64, i64, i64, i64]).reshape(1, 2 * LANES)
    s128 = jnp.concatenate([-jnp.ones(HEAD_DIM // 2), jnp.ones(HEAD_DIM // 2)])
    s64 = jnp.concatenate([-jnp.ones(IDX_DIM // 2), jnp.ones(IDX_DIM // 2)])
    sgn_row = jnp.concatenate([s128, s64, s64]).astype(jnp.float32).reshape(1, 2 * LANES)
    tm = _tile(T, 512)
    spec = pl.BlockSpec((tm, 2 * LANES), lambda i: (i, 0))
    row = pl.BlockSpec((1, 2 * LANES), lambda i: (0, 0))
    return pl.pallas_call(
        _rope_table_kernel,
        out_shape=[jax.ShapeDtypeStruct((T, 2 * LANES), jnp.float32)] * 2,
        grid=(T // tm,),
        in_specs=[pl.BlockSpec((tm, 1), lambda i: (i, 0)), row, row],
        out_specs=[spec, spec],
        compiler_params=_cparams(("parallel",)),
        name="rope_tables",
    )(pos, inv_row, sgn_row)


def _rope128(x, cos, sin_signed):
    return x * cos + pltpu.roll(x, HEAD_DIM // 2, 1) * sin_signed


def _rope64x2(x, cos, sin_signed, lane):
    half = IDX_DIM // 2
    first = (lane % IDX_DIM) < half
    partner = jnp.where(first, pltpu.roll(x, LANES - half, 1), pltpu.roll(x, half, 1))
    return x * cos + partner * sin_signed


def _attn_prep_kernel(aq_ref, ak_ref, iq_ref, sm_ref, cos_ref, sin_ref, qw_ref, kw_ref, ikw_ref,
                      q_ref, k_ref, qi_ref, ki_ref, wi_ref):
    cos_a, sin_a = cos_ref[:, :LANES], sin_ref[:, :LANES]
    cos_i, sin_i = cos_ref[:, LANES:], sin_ref[:, LANES:]
    tm = cos_a.shape[0]
    lane = lax.broadcasted_iota(jnp.int32, (tm, LANES), 1)

    def head_norm(xh, w):
        ms = jnp.mean(xh * xh, axis=-1, keepdims=True)
        return xh * lax.rsqrt(ms + EPS) * w

    for h in range(ATT_HEADS):
        sl = slice(h * HEAD_DIM, (h + 1) * HEAD_DIM)
        xh = head_norm(aq_ref[:, sl].astype(jnp.float32), qw_ref[...])
        q_ref[:, sl] = (_rope128(xh, cos_a, sin_a) * (LOG2E * HEAD_DIM ** -0.5)).astype(q_ref.dtype)
    for h in range(ATT_KV_HEADS):
        sl = slice(h * HEAD_DIM, (h + 1) * HEAD_DIM)
        xh = head_norm(ak_ref[:, sl].astype(jnp.float32), kw_ref[...])
        k_ref[:, sl] = _rope128(xh, cos_a, sin_a).astype(k_ref.dtype)
    for p in range(IDX_Q_WIDTH // LANES):
        sl = slice(p * LANES, (p + 1) * LANES)
        xp = iq_ref[:, sl].astype(jnp.float32)
        qi_ref[:, sl] = (_rope64x2(xp, cos_i, sin_i, lane) * (IDX_DIM ** -0.5)).astype(qi_ref.dtype)
    sm = sm_ref[...]
    in_k = lane < IDX_DIM
    xk = jnp.where(in_k, sm, 0.0)
    ms = jnp.sum(xk * xk, axis=-1, keepdims=True) * (1.0 / IDX_DIM)
    kn = xk * lax.rsqrt(ms + EPS) * ikw_ref[...]
    kr = jnp.where(in_k, _rope64x2(kn, cos_i, sin_i, lane), 0.0)
    ki_ref[:, :LANES] = kr.astype(ki_ref.dtype)
    ki_ref[:, LANES:] = pltpu.roll(kr, IDX_DIM, 1).astype(ki_ref.dtype)
    wi_ref[...] = sm * (IDX_HEADS ** -0.5)


def attn_prep(P, Psm, cos_t, sin_t, q_norm_w, k_norm_w, idx_k_norm_w, tm=256):
    T = P.shape[0]
    tm = _tile(T, tm)
    ikw = jnp.concatenate([idx_k_norm_w, jnp.zeros((LANES - IDX_DIM,), jnp.float32)]).reshape(1, LANES)
    row = lambda w: pl.BlockSpec((1, w), lambda i: (0, 0))
    return pl.pallas_call(
        _attn_prep_kernel,
        out_shape=[jax.ShapeDtypeStruct((T, ATT_Q_WIDTH), jnp.bfloat16),
                   jax.ShapeDtypeStruct((T, ATT_KV_WIDTH), jnp.bfloat16),
                   jax.ShapeDtypeStruct((T, IDX_Q_WIDTH), jnp.bfloat16),
                   jax.ShapeDtypeStruct((T, 2 * LANES), jnp.bfloat16),
                   jax.ShapeDtypeStruct((T, LANES), jnp.float32)],
        grid=(T // tm,),
        in_specs=[pl.BlockSpec((tm, ATT_Q_WIDTH), lambda i: (i, OFF_AQ // ATT_Q_WIDTH)),
                  pl.BlockSpec((tm, ATT_KV_WIDTH), lambda i: (i, OFF_AK // ATT_KV_WIDTH)),
                  pl.BlockSpec((tm, IDX_Q_WIDTH), lambda i: (i, OFF_IQ // IDX_Q_WIDTH)),
                  pl.BlockSpec((tm, LANES), lambda i: (i, 0)),
                  pl.BlockSpec((tm, 2 * LANES), lambda i: (i, 0)),
                  pl.BlockSpec((tm, 2 * LANES), lambda i: (i, 0)),
                  row(LANES), row(LANES), row(LANES)],
        out_specs=[pl.BlockSpec((tm, ATT_Q_WIDTH), lambda i: (i, 0)),
                   pl.BlockSpec((tm, ATT_KV_WIDTH), lambda i: (i, 0)),
                   pl.BlockSpec((tm, IDX_Q_WIDTH), lambda i: (i, 0)),
                   pl.BlockSpec((tm, 2 * LANES), lambda i: (i, 0)),
                   pl.BlockSpec((tm, LANES), lambda i: (i, 0))],
        compiler_params=_cparams(("parallel",)),
        name="attn_prep",
    )(P, P, P, Psm, cos_t, sin_t, q_norm_w.reshape(1, LANES), k_norm_w.reshape(1, LANES), ikw)


def _lane_fold(x, op):
    s = x[:, :LANES]
    for j in range(1, x.shape[1] // LANES):
        s = op(s, x[:, j * LANES:(j + 1) * LANES])
    return s


def _indexer_kernel(qi_ref, ki_ref, wi_ref, bias_ref, key_ref, *, n_sel, tq, tk, nchunks):
    q0 = pl.program_id(1) * tq
    nck = (q0 + tq + tk - 1) // tk
    wi = wi_ref[...]
    rowpos = q0 + lax.broadcasted_iota(jnp.int32, (tq, tk), 0)
    colpos0 = lax.broadcasted_iota(jnp.int32, (tq, tk), 1)

    def score_chunk(c, carry):
        mx, mn = carry
        ks = ki_ref[pl.ds(pl.multiple_of(c * tk, tk), tk), :]
        acc = jnp.zeros((tq, tk), jnp.float32)
        for h in range(IDX_HEADS):
            qp = qi_ref[:, (h // 2) * LANES:(h // 2 + 1) * LANES]
            kh = ks[:, (h % 2) * LANES:(h % 2 + 1) * LANES]
            d = lax.dot_general(qp, kh, (((1,), (1,)), ((), ())), preferred_element_type=jnp.float32)
            acc = acc + wi[:, SM_IW + h:SM_IW + h + 1] * jnp.maximum(d, 0.0)
        causal = colpos0 + c * tk <= rowpos
        key_ref[c] = jnp.where(causal, acc, -SCORE_MASKED)
        mx = jnp.maximum(mx, _lane_fold(jnp.where(causal, acc, -SCORE_MASKED), jnp.maximum))
        mn = jnp.minimum(mn, _lane_fold(jnp.where(causal, acc, SCORE_MASKED), jnp.minimum))
        return mx, mn

    mx, mn = lax.fori_loop(0, nck, score_chunk, (jnp.full((tq, LANES), -SCORE_MASKED, jnp.float32),
                                                 jnp.full((tq, LANES), SCORE_MASKED, jnp.float32)))

    def count_ge(cand):
        def body(c, acc):
            return acc + _lane_fold(jnp.where(key_ref[c] >= cand, 1.0, 0.0), jnp.add)
        acc = lax.fori_loop(0, nck, body, jnp.zeros((tq, LANES), jnp.float32))
        return jnp.sum(acc, axis=1, keepdims=True)

    lo = jnp.min(mn, axis=1, keepdims=True)
    hi = jnp.max(mx, axis=1, keepdims=True)
    ncausal = q0 + lax.broadcasted_iota(jnp.int32, (tq, 1), 0) + 1
    hi = jnp.where(ncausal <= n_sel, lo, hi)

    def unfinished(carry):
        return (carry[0] < BISECT_MAX_STEPS) & (carry[3] > 0)

    def bisect(carry):
        it, lo, hi, _ = carry
        for _ in range(BISECT_UNROLL):
            mid = 0.5 * lo + 0.5 * hi
            cnt = count_ge(mid)
            ok = cnt >= n_sel
            open_row = (mid > lo) & (mid < hi)
            lo = jnp.where(ok, mid, lo)
            hi = jnp.where(cnt == n_sel, mid, jnp.where(ok, hi, mid))
            open_row = open_row & (hi > lo)
        return it + BISECT_UNROLL, lo, hi, jnp.sum(jnp.where(open_row, 1, 0))

    _, thr, _, _ = lax.while_loop(unfinished, bisect, (jnp.int32(0), lo, hi, jnp.int32(1)))

    def write_chunk(c, carry):
        bias_ref[c] = jnp.where(key_ref[c] >= thr, 0.0, NEG_BIG).astype(bias_ref.dtype)
        return carry

    def write_masked(c, carry):
        bias_ref[c] = jnp.full((tq, tk), NEG_BIG, bias_ref.dtype)
        return carry

    lax.fori_loop(0, nck, write_chunk, 0)
    lax.fori_loop(nck, nchunks, write_masked, 0)


def indexer_mask(qi, ki, wi, B, S, tq=256, tk=512):
    tq, tk = _tile(S, tq), _tile(S, tk)
    n_sel = min(TOPK_MAX, S // 4)
    nq, nchunks = S // tq, S // tk
    return pl.pallas_call(
        functools.partial(_indexer_kernel, n_sel=n_sel, tq=tq, tk=tk, nchunks=nchunks),
        out_shape=jax.ShapeDtypeStruct((B, nq, nchunks, tq, tk), jnp.bfloat16),
        grid=(B, nq),
        in_specs=[pl.BlockSpec((tq, IDX_Q_WIDTH), lambda b, i: (b * nq + i, 0)),
                  pl.BlockSpec((S, 2 * LANES), lambda b, i: (b, 0)),
                  pl.BlockSpec((tq, LANES), lambda b, i: (b * nq + i, 0))],
        out_specs=pl.BlockSpec((None, None, nchunks, tq, tk), lambda b, i: (b, i, 0, 0, 0)),
        scratch_shapes=[pltpu.VMEM((nchunks, tq, tk), jnp.float32)],
        compiler_params=_cparams(("parallel", "parallel")),
        name="indexer_mask",
    )(qi, ki, wi)


def _attn_kernel(q_ref, k_ref, v_ref, bias_ref, o_ref, m_ref, acc_ref, *, tq, tk):
    i, j = pl.program_id(1), pl.program_id(2)
    grp = ATT_HEADS // ATT_KV_HEADS

    @pl.when(j == 0)
    def _():
        m_ref[...] = jnp.full(m_ref.shape, -1e38, jnp.float32)
        acc_ref[...] = jnp.zeros(acc_ref.shape, jnp.float32)

    @pl.when(j * tk <= i * tq + tq - 1)
    def _():
        bias = bias_ref[...].astype(jnp.float32)
        ones = jnp.ones((tk, LANES), v_ref.dtype)
        v1 = [jnp.concatenate([v_ref[:, g * HEAD_DIM:(g + 1) * HEAD_DIM], ones], axis=1) for g in range(ATT_KV_HEADS)]
        s, m_new, alpha, p = {}, {}, {}, {}

        def scores(h):
            g = h // grp
            qh = q_ref[:, h * HEAD_DIM:(h + 1) * HEAD_DIM]
            kh = k_ref[:, g * HEAD_DIM:(g + 1) * HEAD_DIM]
            s[h] = lax.dot_general(qh, kh, (((1,), (1,)), ((), ())), preferred_element_type=jnp.float32) + bias
            m_prev = m_ref[h]
            m_new[h] = jnp.maximum(m_prev, jnp.max(s[h], axis=1, keepdims=True))
            alpha[h] = jnp.exp2(m_prev - m_new[h])
            m_ref[h] = m_new[h]

        def probs(h):
            p[h] = jnp.exp2(s.pop(h) - m_new.pop(h)[:, :1]).astype(v_ref.dtype)

        def values(h):
            a2 = jnp.concatenate([alpha[h], alpha.pop(h)], axis=1)
            acc_ref[h] = a2 * acc_ref[h] + jnp.dot(p.pop(h), v1[h // grp], preferred_element_type=jnp.float32)

        for t in range(ATT_HEADS + 2):
            if t < ATT_HEADS:
                scores(t)
            if 0 <= t - 1 < ATT_HEADS:
                probs(t - 1)
            if 0 <= t - 2 < ATT_HEADS:
                values(t - 2)

    @pl.when(j == pl.num_programs(2) - 1)
    def _():
        for h in range(ATT_HEADS):
            acc = acc_ref[h]
            o_ref[:, h * HEAD_DIM:(h + 1) * HEAD_DIM] = (acc[:, :HEAD_DIM] / acc[:, HEAD_DIM:]).astype(o_ref.dtype)


def masked_attention(q, k, P, bias, B, S):
    _, nq, nkv, tq, tk = bias.shape

    def kv_idx(i, j):
        return jnp.minimum(j, (i * tq + tq - 1) // tk)

    return pl.pallas_call(
        functools.partial(_attn_kernel, tq=tq, tk=tk),
        out_shape=jax.ShapeDtypeStruct((B * S, ATT_Q_WIDTH), jnp.bfloat16),
        grid=(B, nq, nkv),
        in_specs=[pl.BlockSpec((tq, ATT_Q_WIDTH), lambda b, i, j: (b * nq + i, 0)),
                  pl.BlockSpec((tk, ATT_KV_WIDTH), lambda b, i, j: (b * nkv + kv_idx(i, j), 0)),
                  pl.BlockSpec((tk, ATT_KV_WIDTH), lambda b, i, j: (b * nkv + kv_idx(i, j), OFF_AV // ATT_KV_WIDTH)),
                  pl.BlockSpec((None, None, None, tq, tk), lambda b, i, j: (b, i, kv_idx(i, j), 0, 0))],
        out_specs=pl.BlockSpec((tq, ATT_Q_WIDTH), lambda b, i, j: (b * nq + i, 0)),
        scratch_shapes=[pltpu.VMEM((ATT_HEADS, tq, LANES), jnp.float32),
                        pltpu.VMEM((ATT_HEADS, tq, HEAD_DIM + LANES), jnp.float32)],
        compiler_params=_cparams(("parallel", "parallel", "arbitrary")),
        name="masked_attention",
    )(q, k, P, bias)


GDN_HALO = 16


def _gdn_prep_kernel(xq_ref, xk_ref, xv_ref, hq_ref, hk_ref, hv_ref, sm_ref, cw_ref, alog_ref, dtb_ref,
                     q_ref, k_ref, v_ref, gb_ref, *, tt):
    first = pl.program_id(1) == 0

    def conv_silu(x_ref, h_ref, part):
        halo = jnp.where(first, 0.0, h_ref[...].astype(jnp.float32))
        xp = jnp.concatenate([halo, x_ref[...].astype(jnp.float32)], axis=0)
        w = cw_ref[:, part * GDN_WIDTH:(part + 1) * GDN_WIDTH]
        y = None
        for jj in range(CONV_WIDTH):
            off = GDN_HALO - (CONV_WIDTH - 1) + jj
            term = w[jj:jj + 1, :] * xp[off:off + tt, :]
            y = term if y is None else y + term
        return y * jax.nn.sigmoid(y)

    def l2(yh):
        return yh * lax.rsqrt(jnp.sum(yh * yh, axis=-1, keepdims=True) + EPS)

    yq = conv_silu(xq_ref, hq_ref, 0)
    yk = conv_silu(xk_ref, hk_ref, 1)
    yv = conv_silu(xv_ref, hv_ref, 2)
    for h in range(GDN_HEADS):
        sl = slice(h * GDN_DK, (h + 1) * GDN_DK)
        q_ref[:, sl] = (l2(yq[:, sl]) * (GDN_DK ** -0.5)).astype(q_ref.dtype)
        k_ref[:, sl] = l2(yk[:, sl]).astype(k_ref.dtype)
    v_ref[...] = yv.astype(v_ref.dtype)
    sm = sm_ref[...]
    a = sm + dtb_ref[...]
    softplus = jnp.maximum(a, 0.0) + jnp.log(1.0 + jnp.exp(-jnp.abs(a)))
    g = -jnp.exp(alog_ref[...]) * softplus
    beta = jax.nn.sigmoid(sm)
    lane = lax.broadcasted_iota(jnp.int32, sm.shape, 1)
    gb_ref[...] = jnp.where(lane < SM_GB, g, beta)


def gdn_prep(P, Psm, conv_w, a_log, dt_bias, B, S, tt=256):
    T = B * S
    tt = _tile(S, tt)
    nt = S // tt
    hpt = tt // GDN_HALO
    W = GDN_WIDTH

    def pad_lane(v, off):
        return jnp.zeros((1, LANES), jnp.float32).at[0, off:off + v.shape[0]].set(v)

    def xspec(off):
        return pl.BlockSpec((tt, W), lambda b, t: (b * nt + t, off // W))

    def hspec(off):
        return pl.BlockSpec((GDN_HALO, W), lambda b, t: (jnp.maximum((b * nt + t) * hpt - 1, 0), off // W))

    ospec = pl.BlockSpec((tt, W), lambda b, t: (b * nt + t, 0))
    return pl.pallas_call(
        functools.partial(_gdn_prep_kernel, tt=tt),
        out_shape=[jax.ShapeDtypeStruct((T, W), jnp.bfloat16)] * 3 + [jax.ShapeDtypeStruct((T, LANES), jnp.float32)],
        grid=(B, nt),
        in_specs=[xspec(OFF_GQ), xspec(OFF_GK), xspec(OFF_GV), hspec(OFF_GQ), hspec(OFF_GK), hspec(OFF_GV),
                  pl.BlockSpec((tt, LANES), lambda b, t: (b * nt + t, 0)),
                  pl.BlockSpec((CONV_WIDTH, 3 * W), lambda b, t: (0, 0)),
                  pl.BlockSpec((1, LANES), lambda b, t: (0, 0)),
                  pl.BlockSpec((1, LANES), lambda b, t: (0, 0))],
        out_specs=[ospec, ospec, ospec, pl.BlockSpec((tt, LANES), lambda b, t: (b * nt + t, 0))],
        compiler_params=_cparams(("parallel", "parallel")),
        name="gdn_prep",
    )(P, P, P, P, P, P, Psm, conv_w, pad_lane(a_log, SM_GA), pad_lane(dt_bias, SM_GA))


GDN_HB = 16


def _gdn_kernel(q_ref, k_ref, v_ref, z_ref, gcol_ref, bcol_ref, grow_ref, nw_ref, o_ref, state_ref):
    C = CHUNK
    f32, bf16 = jnp.float32, jnp.bfloat16

    @pl.when(pl.program_id(2) == 0)
    def _():
        state_ref[...] = jnp.zeros(state_ref.shape, f32)

    row = lax.broadcasted_iota(jnp.int32, (C, C), 0)
    col = lax.broadcasted_iota(jnp.int32, (C, C), 1)
    tril, strict = row >= col, row > col
    eye = (row == col).astype(f32)
    hi = lax.Precision.HIGHEST
    gc_col = jnp.dot(tril.astype(f32), gcol_ref[...], precision=hi, preferred_element_type=f32)
    gc_row = jnp.dot(grow_ref[...], (row <= col).astype(f32), precision=hi, preferred_element_type=f32)
    nt = (((1,), (1,)), ((), ()))
    heads = range(GDN_HB)
    dot = functools.partial(jnp.dot, preferred_element_type=f32)
    kq, a, intra, rhs, qd, kd, eglast = [], [], [], [], [], [], []
    for j in heads:
        sl = slice(j * GDN_DK, (j + 1) * GDN_DK)
        gc = gc_col[:, j:j + 1]
        glast = gc[C - 1:C, :]
        eg = jnp.exp(gc)
        beta = bcol_ref[:, j:j + 1]
        q, k, v = q_ref[:, sl], k_ref[:, sl], v_ref[:, sl]
        kf = k.astype(f32)
        kq.append(jnp.concatenate([k, q], axis=0))
        rhs.append(jnp.concatenate([(v.astype(f32) * beta).astype(bf16), (kf * (beta * eg)).astype(bf16)], axis=1))
        qd.append((q.astype(f32) * eg).astype(bf16))
        kd.append((kf * jnp.exp(glast - gc)).astype(bf16))
        eglast.append(jnp.exp(glast))
    skq = [lax.dot_general(kq[j], kq[j][:C], nt, preferred_element_type=f32) for j in heads]
    for j in heads:
        decay = jnp.exp(jnp.where(tril, gc_col[:, j:j + 1] - gc_row[j:j + 1, :], NEG_BIG))
        a.append(jnp.where(strict, skq[j][:C] * bcol_ref[:, j:j + 1] * decay, 0.0))
        intra.append((skq[j][C:] * decay).astype(bf16))
    ab = [a[j].astype(bf16) for j in heads]
    xb = [dot(ab[j], ab[j]).astype(bf16) for j in heads]
    tinv = [eye - a[j] for j in heads]
    for it in range(5):
        last = it == 4
        lhs = [tinv[j].astype(bf16) if last else jnp.concatenate([tinv[j].astype(bf16), xb[j]], axis=0) for j in heads]
        prod = [dot(lhs[j], xb[j]) for j in heads]
        tinv = [tinv[j] + prod[j][:C] for j in heads]
        if not last:
            xb = [prod[j][C:].astype(bf16) for j in heads]
    uw = [dot(tinv[j].astype(bf16), rhs[j]) for j in heads]
    sb = [state_ref[j].astype(bf16) for j in heads]
    ws = [dot(jnp.concatenate([uw[j][:, GDN_DV:].astype(bf16), qd[j]], axis=0), sb[j]) for j in heads]
    vb = [(uw[j][:, :GDN_DV] - ws[j][:C]).astype(bf16) for j in heads]
    o = [ws[j][C:] + dot(intra[j], vb[j]) for j in heads]
    upd = [lax.dot_general(kd[j], vb[j], (((0,), (0,)), ((), ())), preferred_element_type=f32) for j in heads]
    for j in heads:
        sl = slice(j * GDN_DK, (j + 1) * GDN_DK)
        state_ref[j] = state_ref[j] * eglast[j] + upd[j]
        ms = jnp.mean(o[j] * o[j], axis=-1, keepdims=True)
        z = z_ref[:, sl].astype(f32)
        o_ref[:, sl] = (o[j] * lax.rsqrt(ms + EPS) * nw_ref[...] * (z * jax.nn.sigmoid(z))).astype(o_ref.dtype)


def gdn_delta_rule(q, k, v, P, gb, norm_w, B, S):
    T = B * S
    C, HB = CHUNK, GDN_HB
    n = S // C
    ng = GDN_HEADS // HB
    W = HB * GDN_DK
    g = gb[:, SM_GA:SM_GA + GDN_HEADS]
    beta = gb[:, SM_GB:SM_GB + GDN_HEADS]
    gcol = g.reshape(T, ng, HB).transpose(1, 0, 2)
    bcol = beta.reshape(T, ng, HB).transpose(1, 0, 2)
    grow = g.reshape(B * n, C, ng, HB).transpose(2, 0, 3, 1)
    blk = lambda off: pl.BlockSpec((C, W), lambda b, hg, c: (b * n + c, off // W + hg))
    colspec = pl.BlockSpec((None, C, HB), lambda b, hg, c: (hg, b * n + c, 0))
    return pl.pallas_call(
        _gdn_kernel,
        out_shape=jax.ShapeDtypeStruct((T, GDN_WIDTH), jnp.bfloat16),
        grid=(B, ng, n),
        in_specs=[blk(0), blk(0), blk(0), blk(OFF_GZ), colspec, colspec,
                  pl.BlockSpec((None, None, HB, C), lambda b, hg, c: (hg, b * n + c, 0, 0)),
                  pl.BlockSpec((1, GDN_DV), lambda b, hg, c: (0, 0))],
        out_specs=blk(0),
        scratch_shapes=[pltpu.VMEM((HB, GDN_DK, GDN_DV), jnp.float32)],
        compiler_params=_cparams(("parallel", "parallel", "arbitrary")),
        name="gdn_delta_rule",
    )(q, k, v, P, gcol, bcol, grow, norm_w.reshape(1, GDN_DV))


def _merge_kernel(ya_ref, yg_ref, wa_ref, wg_ref, ga_ref, gg_ref, o_ref):
    pa = jnp.dot(ya_ref[...], wa_ref[...], preferred_element_type=jnp.float32)
    pg = jnp.dot(yg_ref[...], wg_ref[...], preferred_element_type=jnp.float32)
    ga = jax.nn.sigmoid(ga_ref[...].astype(jnp.float32))
    gg = jax.nn.sigmoid(gg_ref[...].astype(jnp.float32))
    o_ref[...] = (ga * pa + gg * pg).astype(o_ref.dtype)


def gated_merge(ya, yg, wa, wg, P, D, tm=1024, tn=512):
    T = ya.shape[0]
    tm, tn = _tile(T, tm), _tile(D, tn)
    goff = OFF_GATES // tn
    return pl.pallas_call(
        _merge_kernel,
        out_shape=jax.ShapeDtypeStruct((T, D), jnp.bfloat16),
        grid=(D // tn, T // tm),
        in_specs=[pl.BlockSpec((tm, ATT_Q_WIDTH), lambda j, i: (i, 0)),
                  pl.BlockSpec((tm, GDN_WIDTH), lambda j, i: (i, 0)),
                  pl.BlockSpec((ATT_Q_WIDTH, tn), lambda j, i: (0, j)),
                  pl.BlockSpec((GDN_WIDTH, tn), lambda j, i: (0, j)),
                  pl.BlockSpec((tm, tn), lambda j, i: (i, goff + j)),
                  pl.BlockSpec((tm, tn), lambda j, i: (i, goff + D // tn + j))],
        out_specs=pl.BlockSpec((tm, tn), lambda j, i: (i, j)),
        compiler_params=_cparams(("parallel", "parallel")),
        name="gated_merge",
    )(ya, yg, wa, wg, P, P)


def _outproj_kernel(a_ref, w_ref, x_ref, o_ref):
    o_ref[...] = x_ref[...] + jnp.dot(a_ref[...], w_ref[...], preferred_element_type=jnp.float32)


def out_proj_residual(a, w, x, tm=1024, tn=512):
    T, D = x.shape
    K = a.shape[1]
    tm, tn = _tile(T, tm), _tile(D, tn)
    return pl.pallas_call(
        _outproj_kernel,
        out_shape=jax.ShapeDtypeStruct((T, D), jnp.float32),
        grid=(D // tn, T // tm),
        in_specs=[pl.BlockSpec((tm, K), lambda j, i: (i, 0)),
                  pl.BlockSpec((K, tn), lambda j, i: (0, j)),
                  pl.BlockSpec((tm, tn), lambda j, i: (i, j))],
        out_specs=pl.BlockSpec((tm, tn), lambda j, i: (i, j)),
        compiler_params=_cparams(("parallel", "parallel")),
        name="out_proj_residual",
    )(a, w, x)


def _ffn_norm_router_kernel(x_ref, w_ref, rhi_ref, rlo_ref, h_ref, lg_ref):
    x = x_ref[...]
    ms = jnp.mean(x * x, axis=-1, keepdims=True)
    h = x * lax.rsqrt(ms + EPS) * w_ref[...]
    h_ref[...] = h
    hh = h.astype(jnp.bfloat16)
    hl = (h - hh.astype(jnp.float32)).astype(jnp.bfloat16)
    f32 = jnp.float32
    lg_ref[...] = (jnp.dot(hh, rhi_ref[...], preferred_element_type=f32)
                   + jnp.dot(hh, rlo_ref[...], preferred_element_type=f32)
                   + jnp.dot(hl, rhi_ref[...], preferred_element_type=f32))


def ffn_norm_router(x1, norm_w, w_router, tm=256):
    T, D = x1.shape
    tm = _tile(T, tm)
    rhi = w_router.astype(jnp.bfloat16)
    rlo = (w_router - rhi.astype(jnp.float32)).astype(jnp.bfloat16)
    return pl.pallas_call(
        _ffn_norm_router_kernel,
        out_shape=[jax.ShapeDtypeStruct((T, D), jnp.float32), jax.ShapeDtypeStruct((T, LANES), jnp.float32)],
        grid=(T // tm,),
        in_specs=[pl.BlockSpec((tm, D), lambda i: (i, 0)), pl.BlockSpec((1, D), lambda i: (0, 0)),
                  pl.BlockSpec((D, LANES), lambda i: (0, 0)), pl.BlockSpec((D, LANES), lambda i: (0, 0))],
        out_specs=[pl.BlockSpec((tm, D), lambda i: (i, 0)), pl.BlockSpec((tm, LANES), lambda i: (i, 0))],
        compiler_params=_cparams(("parallel",)),
        name="ffn_norm_router",
    )(x1, norm_w.reshape(1, D), rhi, rlo)


def _routing_kernel(lg_ref, b_ref, eid_ref, wt_ref):
    lg = lg_ref[...] + b_ref[...]
    lane = lax.broadcasted_iota(jnp.int32, lg.shape, 1)
    ninf = -jnp.inf

    def first_argmax(vals, vmax):
        return jnp.min(jnp.where(vals == vmax, lane, LANES), axis=-1, keepdims=True)

    glog = jnp.where(lane < N_GROUPS, lg, ninf)
    gmax = jnp.max(glog, axis=-1, keepdims=True)
    p_grp = 1.0 / jnp.sum(jnp.exp(glog - gmax), axis=-1, keepdims=True)
    grp = first_argmax(glog, gmax)
    base = RT_EXP + grp * EXPERTS_PER_GROUP
    elog = jnp.where((lane >= base) & (lane < base + EXPERTS_PER_GROUP), lg, ninf)
    emax = jnp.max(elog, axis=-1, keepdims=True)
    idx1 = first_argmax(elog, emax)
    elog2 = jnp.where(lane == idx1, ninf, elog)
    emax2 = jnp.max(elog2, axis=-1, keepdims=True)
    idx2 = first_argmax(elog2, emax2)
    e2 = jnp.exp(emax2 - emax)
    w1 = p_grp / (1.0 + e2)
    w2 = p_grp * e2 / (1.0 + e2)
    eid_ref[...] = jnp.where(lane == 0, idx1 - RT_EXP, jnp.where(lane == 1, idx2 - RT_EXP, 0))
    wt_ref[...] = jnp.where(lane == 0, w1, jnp.where(lane == 1, w2, 0.0))


def routing(logits, bias_row, tm=512):
    T = logits.shape[0]
    tm = _tile(T, tm)
    spec = pl.BlockSpec((tm, LANES), lambda i: (i, 0))
    return pl.pallas_call(
        _routing_kernel,
        out_shape=[jax.ShapeDtypeStruct((T, LANES), jnp.int32), jax.ShapeDtypeStruct((T, LANES), jnp.float32)],
        grid=(T // tm,),
        in_specs=[spec, pl.BlockSpec((1, LANES), lambda i: (0, 0))],
        out_specs=[spec, spec],
        compiler_params=_cparams(("parallel",)),
        name="routing",
    )(logits, bias_row)


GATHER_UNROLL = 8


def _row_gather_copy(src_hbm, row, dst, r, sem):
    return pltpu.make_async_copy(src_hbm.at[pl.ds(row, 1), :], dst.at[pl.ds(r, 1), :], sem)


def _moe_ffn_kernel(texp_ref, nused_ref, rowtok_ref, h_hbm, roww_ref, wg_ref, wu_ref, wd_ref, o_ref,
                    xbuf, sem, *, tm):
    i = pl.program_id(0)
    nused = nused_ref[0]

    def start_gather(tile, slot):
        def body(r, carry):
            _row_gather_copy(h_hbm, rowtok_ref[tile * tm + r], xbuf.at[slot], r, sem.at[slot]).start()
            return carry
        lax.fori_loop(0, tm, body, 0, unroll=GATHER_UNROLL)

    def wait_gather(slot):
        pltpu.make_async_copy(h_hbm.at[pl.ds(0, tm), :], xbuf.at[slot], sem.at[slot]).wait()

    @pl.when((i == 0) & (nused > 0))
    def _():
        start_gather(0, 0)

    @pl.when(i + 1 < nused)
    def _():
        start_gather(i + 1, (i + 1) % 2)

    @pl.when(i < nused)
    def _():
        slot = i % 2
        wait_gather(slot)
        x = xbuf[slot].astype(jnp.bfloat16)
        g = jnp.dot(x, wg_ref[0], preferred_element_type=jnp.float32)
        u = jnp.dot(x, wu_ref[0], preferred_element_type=jnp.float32)
        hmid = (g * jax.nn.sigmoid(g) * u).astype(jnp.bfloat16)
        y = jnp.dot(hmid, wd_ref[0], preferred_element_type=jnp.float32)
        o_ref[...] = y * roww_ref[...]

    @pl.when(i >= nused)
    def _():
        o_ref[...] = jnp.zeros(o_ref.shape, o_ref.dtype)


def moe_ffn(h2, tile_expert, n_used, row_token, row_w, wg, wu, wd, tm):
    T, D = h2.shape
    R = row_token.shape[0]
    FF = wg.shape[2]
    ntiles = R // tm
    grid_spec = pltpu.PrefetchScalarGridSpec(
        num_scalar_prefetch=3,
        grid=(ntiles,),
        in_specs=[pl.BlockSpec(memory_space=pl.ANY),
                  pl.BlockSpec((tm, 1), lambda i, te, nu, rt: (i, 0)),
                  pl.BlockSpec((1, D, FF), lambda i, te, nu, rt: (te[i], 0, 0)),
                  pl.BlockSpec((1, D, FF), lambda i, te, nu, rt: (te[i], 0, 0)),
                  pl.BlockSpec((1, FF, D), lambda i, te, nu, rt: (te[i], 0, 0))],
        out_specs=pl.BlockSpec((tm, D), lambda i, te, nu, rt: (i, 0)),
        scratch_shapes=[pltpu.VMEM((2, tm, D), jnp.float32), pltpu.SemaphoreType.DMA((2,))],
    )
    return pl.pallas_call(
        functools.partial(_moe_ffn_kernel, tm=tm),
        out_shape=jax.ShapeDtypeStruct((R, D), jnp.float32),
        grid_spec=grid_spec,
        compiler_params=_cparams(("arbitrary",)),
        name="moe_ffn",
    )(tile_expert, n_used, row_token, h2, row_w, wg, wu, wd)


def _combine_kernel(dest_ref, x_ref, ys_hbm, o_ref, buf, sem, *, tm):
    i = pl.program_id(0)
    n = pl.num_programs(0)

    def start_gather(tile, slot):
        def body(r, carry):
            for kk in range(TOP_K_EXPERTS):
                row = dest_ref[(tile * tm + r) * TOP_K_EXPERTS + kk]
                _row_gather_copy(ys_hbm, row, buf.at[slot, kk], r, sem.at[slot]).start()
            return carry
        lax.fori_loop(0, tm, body, 0, unroll=GATHER_UNROLL)

    def wait_gather(slot):
        for kk in range(TOP_K_EXPERTS):
            pltpu.make_async_copy(ys_hbm.at[pl.ds(0, tm), :], buf.at[slot, kk], sem.at[slot]).wait()

    @pl.when(i == 0)
    def _():
        start_gather(0, 0)

    @pl.when(i + 1 < n)
    def _():
        start_gather(i + 1, (i + 1) % 2)

    slot = i % 2
    wait_gather(slot)
    o_ref[...] = x_ref[...] + buf[slot, 0] + buf[slot, 1]


def moe_combine(x1, ys, dest_row, tm=128):
    T, D = x1.shape
    tm = _tile(T, tm)
    grid_spec = pltpu.PrefetchScalarGridSpec(
        num_scalar_prefetch=1,
        grid=(T // tm,),
        in_specs=[pl.BlockSpec((tm, D), lambda i, d: (i, 0)), pl.BlockSpec(memory_space=pl.ANY)],
        out_specs=pl.BlockSpec((tm, D), lambda i, d: (i, 0)),
        scratch_shapes=[pltpu.VMEM((2, TOP_K_EXPERTS, tm, D), jnp.float32), pltpu.SemaphoreType.DMA((2,))],
    )
    return pl.pallas_call(
        functools.partial(_combine_kernel, tm=tm),
        out_shape=jax.ShapeDtypeStruct((T, D), jnp.float32),
        grid_spec=grid_spec,
        compiler_params=_cparams(("arbitrary",)),
        name="moe_combine",
    )(dest_row, x1, ys)


def moe_dispatch_plan(eid, wts, tm):
    T = eid.shape[0]
    A = T * TOP_K_EXPERTS
    e_flat = eid.reshape(A)
    onehot = (e_flat[:, None] == jnp.arange(N_EXPERTS, dtype=jnp.int32)[None, :]).astype(jnp.int32)
    csum = jnp.cumsum(onehot, axis=0)
    rank = jnp.sum((csum - onehot) * onehot, axis=1)
    counts = csum[-1]
    padded = ((counts + tm - 1) // tm) * tm
    pend = jnp.cumsum(padded)
    pstart = pend - padded
    dest_row = (pstart[e_flat] + rank).astype(jnp.int32)
    R = ((A + N_EXPERTS * (tm - 1)) + tm - 1) // tm * tm
    row_token = jnp.zeros((R,), jnp.int32).at[dest_row].set(jnp.arange(A, dtype=jnp.int32) // TOP_K_EXPERTS)
    row_w = jnp.zeros((R,), jnp.float32).at[dest_row].set(wts.reshape(A)).reshape(R, 1)
    tile_start = jnp.arange(R // tm, dtype=jnp.int32) * tm
    tile_expert = jnp.minimum(jnp.sum(tile_start[:, None] >= pend[None, :], axis=1), N_EXPERTS - 1).astype(jnp.int32)
    n_used = (pend[-1] // tm).astype(jnp.int32).reshape(1)
    return dest_row, row_token, row_w, tile_expert, n_used


def _split_w_in(w, D):
    sizes = (ATT_Q_WIDTH, ATT_KV_WIDTH, ATT_KV_WIDTH, IDX_Q_WIDTH, IDX_DIM, IDX_HEADS,
             GDN_WIDTH, GDN_WIDTH, GDN_WIDTH, GDN_WIDTH, GDN_HEADS, GDN_HEADS, D, D)
    offs = [0]
    for s in sizes:
        offs.append(offs[-1] + s)
    w = w.astype(jnp.bfloat16)
    seg = [w[:, offs[i]:offs[i + 1]] for i in range(len(sizes))]
    aq, ak, av, iq, ik, iw, gq, gk, gv, gz, ga, gb, gate_a, gate_g = seg
    w_big = jnp.concatenate([aq, ak, av, iq, gq, gk, gv, gz, gate_a, gate_g], axis=1)
    pad = jnp.zeros((w.shape[0], LANES - (IDX_DIM + IDX_HEADS + 2 * GDN_HEADS)), w.dtype)
    w_small = jnp.concatenate([ik, iw, ga, gb, pad], axis=1)
    return w_big, w_small


def kernel(x, positions, mix_norm_w, w_in, q_norm_w, k_norm_w, idx_k_norm_w, conv_w, a_log, dt_bias, gdn_norm_w, w_proj_attn, w_proj_gdn, w_out, ffn_norm_w, w_router_group, b_router_group, w_router_expert, b_router_expert, w_gate, w_up, w_down):
    B, S, D = x.shape
    T = B * S
    bf16 = jnp.bfloat16
    moe_tm = 256 if T * TOP_K_EXPERTS >= 256 * N_EXPERTS else 64
    xt = x.reshape(T, D)
    cos_t, sin_t = rope_tables(positions)
    for l in range(w_in.shape[0]):
        w_big, w_small = _split_w_in(w_in[l], D)
        h = rmsnorm(xt, mix_norm_w[l], bf16)
        P = matmul(h, w_big, bf16, name="in_proj")
        Psm = matmul(h, w_small, jnp.float32, name="in_proj_small")
        q, k, qi, ki, wi = attn_prep(P, Psm, cos_t, sin_t, q_norm_w[l], k_norm_w[l], idx_k_norm_w[l])
        bias = indexer_mask(qi, ki, wi, B, S)
        y_attn = masked_attention(q, k, P, bias, B, S)
        gq, gk, gv, gb = gdn_prep(P, Psm, conv_w[l], a_log[l], dt_bias[l], B, S)
        y_gdn = gdn_delta_rule(gq, gk, gv, P, gb, gdn_norm_w[l], B, S)
        mixed = gated_merge(y_attn, y_gdn, w_proj_attn[l].astype(bf16), w_proj_gdn[l].astype(bf16), P, D)
        x1 = out_proj_residual(mixed, w_out[l].astype(bf16), xt)
        w_router = jnp.zeros((D, LANES), jnp.float32)
        w_router = w_router.at[:, :N_GROUPS].set(w_router_group[l])
        w_router = w_router.at[:, RT_EXP:RT_EXP + N_EXPERTS].set(
            w_router_expert[l].transpose(1, 0, 2).reshape(D, N_EXPERTS))
        b_router = jnp.zeros((1, LANES), jnp.float32)
        b_router = b_router.at[0, :N_GROUPS].set(b_router_group[l])
        b_router = b_router.at[0, RT_EXP:RT_EXP + N_EXPERTS].set(b_router_expert[l].reshape(N_EXPERTS))
        h2, logits = ffn_norm_router(x1, ffn_norm_w[l], w_router)
        eid_l, wt_l = routing(logits, b_router)
        eid, wts = eid_l[:, :TOP_K_EXPERTS], wt_l[:, :TOP_K_EXPERTS]
        dest_row, row_token, row_w, tile_expert, n_used = moe_dispatch_plan(eid, wts, moe_tm)
        ys = moe_ffn(h2, tile_expert, n_used, row_token, row_w,
                     w_gate[l].astype(bf16), w_up[l].astype(bf16), w_down[l].astype(bf16), moe_tm)
        xt = moe_combine(x1, ys, dest_row)
    return xt.reshape(B, S, D)
```

```python
import functools
import math

import jax
import jax.numpy as jnp
from jax import lax
from jax.experimental import pallas as pl
from jax.experimental.pallas import tpu as pltpu

ATT_HEADS = 16
ATT_KV_HEADS = 4
HEAD_DIM = 128
IDX_HEADS = 16
IDX_DIM = 64
TOPK_MAX = 256
ROPE_THETA = 10000.0
GDN_HEADS = 16
GDN_DK = 128
GDN_DV = 128
CONV_WIDTH = 4
CHUNK = 64
N_GROUPS = 4
EXPERTS_PER_GROUP = 8
N_EXPERTS = N_GROUPS * EXPERTS_PER_GROUP
TOP_K_EXPERTS = 2
EPS = 1e-6

ATT_Q_WIDTH = ATT_HEADS * HEAD_DIM
ATT_KV_WIDTH = ATT_KV_HEADS * HEAD_DIM
IDX_Q_WIDTH = IDX_HEADS * IDX_DIM
GDN_WIDTH = GDN_HEADS * GDN_DK

LANES = 128
VMEM_LIMIT = 56 * 1024 * 1024
NEG_BIG = -1e30
LOG2E = math.log2(math.e)
SCORE_MASKED = 3.0e38
BISECT_MAX_STEPS = 192
BISECT_UNROLL = 4
COUNT_ROWS = 128

OFF_AQ = 0
OFF_AK = OFF_AQ + ATT_Q_WIDTH
OFF_AV = OFF_AK + ATT_KV_WIDTH
OFF_IQ = OFF_AV + ATT_KV_WIDTH
OFF_GQ = OFF_IQ + IDX_Q_WIDTH
OFF_GK = OFF_GQ + GDN_WIDTH
OFF_GV = OFF_GK + GDN_WIDTH
OFF_GZ = OFF_GV + GDN_WIDTH
OFF_GATES = OFF_GZ + GDN_WIDTH
SM_IK = 0
SM_IW = SM_IK + IDX_DIM
SM_GA = SM_IW + IDX_HEADS
SM_GB = SM_GA + GDN_HEADS
RT_EXP = 8


def _cparams(sem):
    return pltpu.CompilerParams(dimension_semantics=sem, vmem_limit_bytes=VMEM_LIMIT)


def _tile(n, pref):
    t = min(n, pref)
    assert n % t == 0, (n, pref)
    return t


def _rmsnorm_kernel(x_ref, w_ref, o_ref):
    x = x_ref[...]
    ms = jnp.mean(x * x, axis=-1, keepdims=True)
    o_ref[...] = (x * lax.rsqrt(ms + EPS) * w_ref[...]).astype(o_ref.dtype)


def rmsnorm(x, w, out_dtype, tm=256):
    T, D = x.shape
    tm = _tile(T, tm)
    return pl.pallas_call(
        _rmsnorm_kernel,
        out_shape=jax.ShapeDtypeStruct((T, D), out_dtype),
        grid=(T // tm,),
        in_specs=[pl.BlockSpec((tm, D), lambda i: (i, 0)), pl.BlockSpec((1, D), lambda i: (0, 0))],
        out_specs=pl.BlockSpec((tm, D), lambda i: (i, 0)),
        compiler_params=_cparams(("parallel",)),
        name="rmsnorm",
    )(x, w.reshape(1, D))


def _matmul_kernel(a_ref, b_ref, o_ref):
    o_ref[...] = jnp.dot(a_ref[...], b_ref[...], preferred_element_type=jnp.float32).astype(o_ref.dtype)


def matmul(a, b, out_dtype, tm=1024, tn=1024, name="matmul"):
    M, K = a.shape
    _, N = b.shape
    tm, tn = _tile(M, tm), _tile(N, tn)
    return pl.pallas_call(
        _matmul_kernel,
        out_shape=jax.ShapeDtypeStruct((M, N), out_dtype),
        grid=(N // tn, M // tm),
        in_specs=[pl.BlockSpec((tm, K), lambda j, i: (i, 0)), pl.BlockSpec((K, tn), lambda j, i: (0, j))],
        out_specs=pl.BlockSpec((tm, tn), lambda j, i: (i, j)),
        compiler_params=_cparams(("parallel", "parallel")),
        name=name,
    )(a, b)


def _rope_table_kernel(pos_ref, inv_ref, sgn_ref, cos_ref, sin_ref):
    ang = pos_ref[...] * inv_ref[...]
    cos_ref[...] = jnp.cos(ang)
    sin_ref[...] = jnp.sin(ang) * sgn_ref[...]


def rope_tables(positions):
    T = positions.size
    pos = positions.reshape(T, 1).astype(jnp.float32)

    def inv(d):
        return ROPE_THETA ** (-jnp.arange(0, d, 2, dtype=jnp.float32) / d)

    i128, i64 = inv(HEAD_DIM), inv(IDX_DIM)
    inv_row = jnp.concatenate([i128, i128, i64, i64, i64, i64]).reshape(1, 2 * LANES)
    s128 = jnp.concatenate([-jnp.ones(HEAD_DIM // 2), jnp.ones(HEAD_DIM // 2)])
    s64 = jnp.concatenate([-jnp.ones(IDX_DIM // 2), jnp.ones(IDX_DIM // 2)])
    sgn_row = jnp.concatenate([s128, s64, s64]).astype(jnp.float32).reshape(1, 2 * LANES)
    tm = _tile(T, 512)
    spec = pl.BlockSpec((tm, 2 * LANES), lambda i: (i, 0))
    row = pl.BlockSpec((1, 2 * LANES), lambda i: (0, 0))
    return pl.pallas_call(
        _rope_table_kernel,
        out_shape=[jax.ShapeDtypeStruct((T, 2 * LANES), jnp.float32)] * 2,
        grid=(T // tm,),
        in_specs=[pl.BlockSpec((tm, 1), lambda i: (i, 0)), row, row],
        out_specs=[spec, spec],
        compiler_params=_cparams(("parallel",)),
        name="rope_tables",
    )(pos, inv_row, sgn_row)


def _rope128(x, cos, sin_signed):
    return x * cos + pltpu.roll(x, HEAD_DIM // 2, 1) * sin_signed


def _rope64x2(x, cos, sin_signed, lane):
    half = IDX_DIM // 2
    first = (lane % IDX_DIM) < half
    partner = jnp.where(first, pltpu.roll(x, LANES - half, 1), pltpu.roll(x, half, 1))
    return x * cos + partner * sin_signed


def _attn_prep_kernel(aq_ref, ak_ref, iq_ref, sm_ref, cos_ref, sin_ref, qw_ref, kw_ref, ikw_ref,
                      q_ref, k_ref, qi_ref, ki_ref, wi_ref):
    cos_a, sin_a = cos_ref[:, :LANES], sin_ref[:, :LANES]
    cos_i, sin_i = cos_ref[:, LANES:], sin_ref[:, LANES:]
    tm = cos_a.shape[0]
    lane = lax.broadcasted_iota(jnp.int32, (tm, LANES), 1)

    def head_norm(xh, w):
        ms = jnp.mean(xh * xh, axis=-1, keepdims=True)
        return xh * lax.rsqrt(ms + EPS) * w

    for h in range(ATT_HEADS):
        sl = slice(h * HEAD_DIM, (h + 1) * HEAD_DIM)
        xh = head_norm(aq_ref[:, sl].astype(jnp.float32), qw_ref[...])
        q_ref[:, sl] = (_rope128(xh, cos_a, sin_a) * (LOG2E * HEAD_DIM ** -0.5)).astype(q_ref.dtype)
    for h in range(ATT_KV_HEADS):
        sl = slice(h * HEAD_DIM, (h + 1) * HEAD_DIM)
        xh = head_norm(ak_ref[:, sl].astype(jnp.float32), kw_ref[...])
        k_ref[:, sl] = _rope128(xh, cos_a, sin_a).astype(k_ref.dtype)
    for p in range(IDX_Q_WIDTH // LANES):
        sl = slice(p * LANES, (p + 1) * LANES)
        xp = iq_ref[:, sl].astype(jnp.float32)
        qi_ref[:, sl] = (_rope64x2(xp, cos_i, sin_i, lane) * (IDX_DIM ** -0.5)).astype(qi_ref.dtype)
    sm = sm_ref[...]
    in_k = lane < IDX_DIM
    xk = jnp.where(in_k, sm, 0.0)
    ms = jnp.sum(xk * xk, axis=-1, keepdims=True) * (1.0 / IDX_DIM)
    kn = xk * lax.rsqrt(ms + EPS) * ikw_ref[...]
    kr = jnp.where(in_k, _rope64x2(kn, cos_i, sin_i, lane), 0.0)
    ki_ref[:, :LANES] = kr.astype(ki_ref.dtype)
    ki_ref[:, LANES:] = pltpu.roll(kr, IDX_DIM, 1).astype(ki_ref.dtype)
    wi_ref[...] = sm * (IDX_HEADS ** -0.5)


def attn_prep(P, Psm, cos_t, sin_t, q_norm_w, k_norm_w, idx_k_norm_w, tm=256):
    T = P.shape[0]
    tm = _tile(T, tm)
    ikw = jnp.concatenate([idx_k_norm_w, jnp.zeros((LANES - IDX_DIM,), jnp.float32)]).reshape(1, LANES)
    row = lambda w: pl.BlockSpec((1, w), lambda i: (0, 0))
    return pl.pallas_call(
        _attn_prep_kernel,
        out_shape=[jax.ShapeDtypeStruct((T, ATT_Q_WIDTH), jnp.bfloat16),
                   jax.ShapeDtypeStruct((T, ATT_KV_WIDTH), jnp.bfloat16),
                   jax.ShapeDtypeStruct((T, IDX_Q_WIDTH), jnp.bfloat16),
                   jax.ShapeDtypeStruct((T, 2 * LANES), jnp.bfloat16),
                   jax.ShapeDtypeStruct((T, LANES), jnp.float32)],
        grid=(T // tm,),
        in_specs=[pl.BlockSpec((tm, ATT_Q_WIDTH), lambda i: (i, OFF_AQ // ATT_Q_WIDTH)),
                  pl.BlockSpec((tm, ATT_KV_WIDTH), lambda i: (i, OFF_AK // ATT_KV_WIDTH)),
                  pl.BlockSpec((tm, IDX_Q_WIDTH), lambda i: (i, OFF_IQ // IDX_Q_WIDTH)),
                  pl.BlockSpec((tm, LANES), lambda i: (i, 0)),
                  pl.BlockSpec((tm, 2 * LANES), lambda i: (i, 0)),
                  pl.BlockSpec((tm, 2 * LANES), lambda i: (i, 0)),
                  row(LANES), row(LANES), row(LANES)],
        out_specs=[pl.BlockSpec((tm, ATT_Q_WIDTH), lambda i: (i, 0)),
                   pl.BlockSpec((tm, ATT_KV_WIDTH), lambda i: (i, 0)),
                   pl.BlockSpec((tm, IDX_Q_WIDTH), lambda i: (i, 0)),
                   pl.BlockSpec((tm, 2 * LANES), lambda i: (i, 0)),
                   pl.BlockSpec((tm, LANES), lambda i: (i, 0))],
        compiler_params=_cparams(("parallel",)),
        name="attn_prep",
    )(P, P, P, Psm, cos_t, sin_t, q_norm_w.reshape(1, LANES), k_norm_w.reshape(1, LANES), ikw)


def _lane_fold(x, op):
    s = x[:, :LANES]
    for j in range(1, x.shape[1] // LANES):
        s = op(s, x[:, j * LANES:(j + 1) * LANES])
    return s


def _indexer_kernel(qi_ref, ki_ref, wi_ref, bias_ref, key_ref, *, n_sel, tq, tk, nchunks):
    q0 = pl.program_id(1) * tq
    nck = (q0 + tq + tk - 1) // tk
    wi = wi_ref[...]
    rowpos = q0 + lax.broadcasted_iota(jnp.int32, (tq, tk), 0)
    colpos0 = lax.broadcasted_iota(jnp.int32, (tq, tk), 1)

    def score_chunk(c, carry):
        mx, mn = carry
        ks = ki_ref[pl.ds(pl.multiple_of(c * tk, tk), tk), :]
        acc = jnp.zeros((tq, tk), jnp.float32)
        for h in range(IDX_HEADS):
            qp = qi_ref[:, (h // 2) * LANES:(h // 2 + 1) * LANES]
            kh = ks[:, (h % 2) * LANES:(h % 2 + 1) * LANES]
            d = lax.dot_general(qp, kh, (((1,), (1,)), ((), ())), preferred_element_type=jnp.float32)
            acc = acc + wi[:, SM_IW + h:SM_IW + h + 1] * jnp.maximum(d, 0.0)
        causal = colpos0 + c * tk <= rowpos
        key_ref[c] = jnp.where(causal, acc, -SCORE_MASKED)
        mx = jnp.maximum(mx, _lane_fold(jnp.where(causal, acc, -SCORE_MASKED), jnp.maximum))
        mn = jnp.minimum(mn, _lane_fold(jnp.where(causal, acc, SCORE_MASKED), jnp.minimum))
        return mx, mn

    mx, mn = lax.fori_loop(0, nck, score_chunk, (jnp.full((tq, LANES), -SCORE_MASKED, jnp.float32),
                                                 jnp.full((tq, LANES), SCORE_MASKED, jnp.float32)))

    ones = jnp.ones((LANES, LANES), jnp.bfloat16)

    def count_ge(probe):
        accs = []
        for r0 in range(0, tq, COUNT_ROWS):
            rows = slice(r0, min(r0 + COUNT_ROWS, tq))

            def body(c, acc, rows=rows):
                for j in range(tk // LANES):
                    acc = acc + jnp.where(key_ref[c, rows, j * LANES:(j + 1) * LANES] >= probe[rows], 1.0, 0.0)
                return acc

            accs.append(lax.fori_loop(0, nck, body, jnp.zeros((rows.stop - rows.start, LANES), jnp.float32)))
        acc = accs[0] if len(accs) == 1 else jnp.concatenate(accs, axis=0)
        return jnp.dot(acc.astype(jnp.bfloat16), ones, preferred_element_type=jnp.float32)

    lo = jnp.broadcast_to(jnp.min(mn, axis=1, keepdims=True), (tq, LANES))
    hi = jnp.broadcast_to(jnp.max(mx, axis=1, keepdims=True), (tq, LANES))
    ncausal = q0 + lax.broadcasted_iota(jnp.int32, (tq, LANES), 0) + 1
    hi = jnp.where(ncausal <= n_sel, lo, hi)

    def unfinished(carry):
        return (carry[0] < BISECT_MAX_STEPS) & (carry[3] > 0)

    def bisect(carry):
        it, lo, hi, _ = carry
        for _ in range(BISECT_UNROLL):
            mid = 0.5 * lo + 0.5 * hi
            cnt = count_ge(mid)
            ok = cnt >= n_sel
            open_row = (mid > lo) & (mid < hi)
            lo = jnp.where(ok, mid, lo)
            hi = jnp.where(cnt == n_sel, mid, jnp.where(ok, hi, mid))
            open_row = open_row & (hi > lo)
        return it + BISECT_UNROLL, lo, hi, jnp.sum(jnp.where(open_row, 1, 0))

    _, thr, _, _ = lax.while_loop(unfinished, bisect, (jnp.int32(0), lo, hi, jnp.int32(1)))

    def write_chunk(c, carry):
        for j in range(tk // LANES):
            cols = slice(j * LANES, (j + 1) * LANES)
            bias_ref[c, :, cols] = jnp.where(key_ref[c, :, cols] >= thr, 0.0, NEG_BIG).astype(bias_ref.dtype)
        return carry

    def write_masked(c, carry):
        bias_ref[c] = jnp.full((tq, tk), NEG_BIG, bias_ref.dtype)
        return carry

    lax.fori_loop(0, nck, write_chunk, 0)
    lax.fori_loop(nck, nchunks, write_masked, 0)


def indexer_mask(qi, ki, wi, B, S, tq=512, tk=512):
    tq, tk = _tile(S, tq), _tile(S, tk)
    n_sel = min(TOPK_MAX, S // 4)
    nq, nchunks = S // tq, S // tk
    return pl.pallas_call(
        functools.partial(_indexer_kernel, n_sel=n_sel, tq=tq, tk=tk, nchunks=nchunks),
        out_shape=jax.ShapeDtypeStruct((B, nq, nchunks, tq, tk), jnp.bfloat16),
        grid=(B, nq),
        in_specs=[pl.BlockSpec((tq, IDX_Q_WIDTH), lambda b, i: (b * nq + i, 0)),
                  pl.BlockSpec((S, 2 * LANES), lambda b, i: (b, 0)),
                  pl.BlockSpec((tq, LANES), lambda b, i: (b * nq + i, 0))],
        out_specs=pl.BlockSpec((None, None, nchunks, tq, tk), lambda b, i: (b, i, 0, 0, 0)),
        scratch_shapes=[pltpu.VMEM((nchunks, tq, tk), jnp.float32)],
        compiler_params=_cparams(("parallel", "parallel")),
        name="indexer_mask",
    )(qi, ki, wi)


def _attn_kernel(q_ref, k_ref, v_ref, bias_ref, o_ref, m_ref, acc_ref, *, tq, tk):
    i, j = pl.program_id(1), pl.program_id(2)
    grp = ATT_HEADS // ATT_KV_HEADS

    @pl.when(j == 0)
    def _():
        m_ref[...] = jnp.full(m_ref.shape, -1e38, jnp.float32)
        acc_ref[...] = jnp.zeros(acc_ref.shape, jnp.float32)

    @pl.when(j * tk <= i * tq + tq - 1)
    def _():
        bias = bias_ref[...].astype(jnp.float32)
        ones = jnp.ones((tk, LANES), v_ref.dtype)
        v1 = [jnp.concatenate([v_ref[:, g * HEAD_DIM:(g + 1) * HEAD_DIM], ones], axis=1) for g in range(ATT_KV_HEADS)]
        s, m_new, alpha, p = {}, {}, {}, {}

        def scores(h):
            g = h // grp
            qh = q_ref[:, h * HEAD_DIM:(h + 1) * HEAD_DIM]
            kh = k_ref[:, g * HEAD_DIM:(g + 1) * HEAD_DIM]
            s[h] = lax.dot_general(qh, kh, (((1,), (1,)), ((), ())), preferred_element_type=jnp.float32) + bias
            m_prev = m_ref[h]
            m_new[h] = jnp.maximum(m_prev, jnp.max(s[h], axis=1, keepdims=True))
            alpha[h] = jnp.exp2(m_prev - m_new[h])
            m_ref[h] = m_new[h]

        def probs(h):
            p[h] = jnp.exp2(s.pop(h) - m_new.pop(h)[:, :1]).astype(v_ref.dtype)

        def values(h):
            a2 = jnp.concatenate([alpha[h], alpha.pop(h)], axis=1)
            acc_ref[h] = a2 * acc_ref[h] + jnp.dot(p.pop(h), v1[h // grp], preferred_element_type=jnp.float32)

        for t in range(ATT_HEADS + 2):
            if t < ATT_HEADS:
                scores(t)
            if 0 <= t - 1 < ATT_HEADS:
                probs(t - 1)
            if 0 <= t - 2 < ATT_HEADS:
                values(t - 2)

    @pl.when(j == pl.num_programs(2) - 1)
    def _():
        for h in range(ATT_HEADS):
            acc = acc_ref[h]
            o_ref[:, h * HEAD_DIM:(h + 1) * HEAD_DIM] = (acc[:, :HEAD_DIM] / acc[:, HEAD_DIM:]).astype(o_ref.dtype)


def masked_attention(q, k, P, bias, B, S):
    _, nq, nkv, tq, tk = bias.shape

    def kv_idx(i, j):
        return jnp.minimum(j, (i * tq + tq - 1) // tk)

    return pl.pallas_call(
        functools.partial(_attn_kernel, tq=tq, tk=tk),
        out_shape=jax.ShapeDtypeStruct((B * S, ATT_Q_WIDTH), jnp.bfloat16),
        grid=(B, nq, nkv),
        in_specs=[pl.BlockSpec((tq, ATT_Q_WIDTH), lambda b, i, j: (b * nq + i, 0)),
                  pl.BlockSpec((tk, ATT_KV_WIDTH), lambda b, i, j: (b * nkv + kv_idx(i, j), 0)),
                  pl.BlockSpec((tk, ATT_KV_WIDTH), lambda b, i, j: (b * nkv + kv_idx(i, j), OFF_AV // ATT_KV_WIDTH)),
                  pl.BlockSpec((None, None, None, tq, tk), lambda b, i, j: (b, i, kv_idx(i, j), 0, 0))],
        out_specs=pl.BlockSpec((tq, ATT_Q_WIDTH), lambda b, i, j: (b * nq + i, 0)),
        scratch_shapes=[pltpu.VMEM((ATT_HEADS, tq, LANES), jnp.float32),
                        pltpu.VMEM((ATT_HEADS, tq, HEAD_DIM + LANES), jnp.float32)],
        compiler_params=_cparams(("parallel", "parallel", "arbitrary")),
        name="masked_attention",
    )(q, k, P, bias)


GDN_HALO = 16


def _gdn_prep_kernel(xq_ref, xk_ref, xv_ref, hq_ref, hk_ref, hv_ref, sm_ref, cw_ref, alog_ref, dtb_ref,
                     q_ref, k_ref, v_ref, gb_ref, *, tt):
    first = pl.program_id(1) == 0

    def conv_silu(x_ref, h_ref, part):
        halo = jnp.where(first, 0.0, h_ref[...].astype(jnp.float32))
        xp = jnp.concatenate([halo, x_ref[...].astype(jnp.float32)], axis=0)
        w = cw_ref[:, part * GDN_WIDTH:(part + 1) * GDN_WIDTH]
        y = None
        for jj in range(CONV_WIDTH):
            off = GDN_HALO - (CONV_WIDTH - 1) + jj
            term = w[jj:jj + 1, :] * xp[off:off + tt, :]
            y = term if y is None else y + term
        return y * jax.nn.sigmoid(y)

    def l2(yh):
        return yh * lax.rsqrt(jnp.sum(yh * yh, axis=-1, keepdims=True) + EPS)

    yq = conv_silu(xq_ref, hq_ref, 0)
    yk = conv_silu(xk_ref, hk_ref, 1)
    yv = conv_silu(xv_ref, hv_ref, 2)
    for h in range(GDN_HEADS):
        sl = slice(h * GDN_DK, (h + 1) * GDN_DK)
        q_ref[:, sl] = (l2(yq[:, sl]) * (GDN_DK ** -0.5)).astype(q_ref.dtype)
        k_ref[:, sl] = l2(yk[:, sl]).astype(k_ref.dtype)
    v_ref[...] = yv.astype(v_ref.dtype)
    sm = sm_ref[...]
    a = sm + dtb_ref[...]
    softplus = jnp.maximum(a, 0.0) + jnp.log(1.0 + jnp.exp(-jnp.abs(a)))
    g = -jnp.exp(alog_ref[...]) * softplus
    beta = jax.nn.sigmoid(sm)
    lane = lax.broadcasted_iota(jnp.int32, sm.shape, 1)
    gb_ref[...] = jnp.where(lane < SM_GB, g, beta)


def gdn_prep(P, Psm, conv_w, a_log, dt_bias, B, S, tt=256):
    T = B * S
    tt = _tile(S, tt)
    nt = S // tt
    hpt = tt // GDN_HALO
    W = GDN_WIDTH

    def pad_lane(v, off):
        return jnp.zeros((1, LANES), jnp.float32).at[0, off:off + v.shape[0]].set(v)

    def xspec(off):
        return pl.BlockSpec((tt, W), lambda b, t: (b * nt + t, off // W))

    def hspec(off):
        return pl.BlockSpec((GDN_HALO, W), lambda b, t: (jnp.maximum((b * nt + t) * hpt - 1, 0), off // W))

    ospec = pl.BlockSpec((tt, W), lambda b, t: (b * nt + t, 0))
    return pl.pallas_call(
        functools.partial(_gdn_prep_kernel, tt=tt),
        out_shape=[jax.ShapeDtypeStruct((T, W), jnp.bfloat16)] * 3 + [jax.ShapeDtypeStruct((T, LANES), jnp.float32)],
        grid=(B, nt),
        in_specs=[xspec(OFF_GQ), xspec(OFF_GK), xspec(OFF_GV), hspec(OFF_GQ), hspec(OFF_GK), hspec(OFF_GV),
                  pl.BlockSpec((tt, LANES), lambda b, t: (b * nt + t, 0)),
                  pl.BlockSpec((CONV_WIDTH, 3 * W), lambda b, t: (0, 0)),
                  pl.BlockSpec((1, LANES), lambda b, t: (0, 0)),
                  pl.BlockSpec((1, LANES), lambda b, t: (0, 0))],
        out_specs=[ospec, ospec, ospec, pl.BlockSpec((tt, LANES), lambda b, t: (b * nt + t, 0))],
        compiler_params=_cparams(("parallel", "parallel")),
        name="gdn_prep",
    )(P, P, P, P, P, P, Psm, conv_w, pad_lane(a_log, SM_GA), pad_lane(dt_bias, SM_GA))


GDN_HB = 16


def _gdn_kernel(q_ref, k_ref, v_ref, z_ref, gcol_ref, bcol_ref, grow_ref, nw_ref, o_ref, state_ref):
    C = CHUNK
    f32, bf16 = jnp.float32, jnp.bfloat16

    @pl.when(pl.program_id(2) == 0)
    def _():
        state_ref[...] = jnp.zeros(state_ref.shape, f32)

    row = lax.broadcasted_iota(jnp.int32, (C, C), 0)
    col = lax.broadcasted_iota(jnp.int32, (C, C), 1)
    tril, strict = row >= col, row > col
    eye = (row == col).astype(f32)
    hi = lax.Precision.HIGHEST
    gc_col = jnp.dot(tril.astype(f32), gcol_ref[...], precision=hi, preferred_element_type=f32)
    gc_row = jnp.dot(grow_ref[...], (row <= col).astype(f32), precision=hi, preferred_element_type=f32)
    nt = (((1,), (1,)), ((), ()))
    heads = range(GDN_HB)
    dot = functools.partial(jnp.dot, preferred_element_type=f32)
    kq, a, intra, rhs, qd, kd, eglast = [], [], [], [], [], [], []
    for j in heads:
        sl = slice(j * GDN_DK, (j + 1) * GDN_DK)
        gc = gc_col[:, j:j + 1]
        glast = gc[C - 1:C, :]
        eg = jnp.exp(gc)
        beta = bcol_ref[:, j:j + 1]
        q, k, v = q_ref[:, sl], k_ref[:, sl], v_ref[:, sl]
        kf = k.astype(f32)
        kq.append(jnp.concatenate([k, q], axis=0))
        rhs.append(jnp.concatenate([(v.astype(f32) * beta).astype(bf16), (kf * (beta * eg)).astype(bf16)], axis=1))
        qd.append((q.astype(f32) * eg).astype(bf16))
        kd.append((kf * jnp.exp(glast - gc)).astype(bf16))
        eglast.append(jnp.exp(glast))
    skq = [lax.dot_general(kq[j], kq[j][:C], nt, preferred_element_type=f32) for j in heads]
    for j in heads:
        decay = jnp.exp(jnp.where(tril, gc_col[:, j:j + 1] - gc_row[j:j + 1, :], NEG_BIG))
        a.append(jnp.where(strict, skq[j][:C] * bcol_ref[:, j:j + 1] * decay, 0.0))
        intra.append((skq[j][C:] * decay).astype(bf16))
    ab = [a[j].astype(bf16) for j in heads]
    xb = [dot(ab[j], ab[j]).astype(bf16) for j in heads]
    tinv = [eye - a[j] for j in heads]
    for it in range(5):
        last = it == 4
        lhs = [tinv[j].astype(bf16) if last else jnp.concatenate([tinv[j].astype(bf16), xb[j]], axis=0) for j in heads]
        prod = [dot(lhs[j], xb[j]) for j in heads]
        tinv = [tinv[j] + prod[j][:C] for j in heads]
        if not last:
            xb = [prod[j][C:].astype(bf16) for j in heads]
    uw = [dot(tinv[j].astype(bf16), rhs[j]) for j in heads]
    sb = [state_ref[j].astype(bf16) for j in heads]
    ws = [dot(jnp.concatenate([uw[j][:, GDN_DV:].astype(bf16), qd[j]], axis=0), sb[j]) for j in heads]
    vb = [(uw[j][:, :GDN_DV] - ws[j][:C]).astype(bf16) for j in heads]
    o = [ws[j][C:] + dot(intra[j], vb[j]) for j in heads]
    upd = [lax.dot_general(kd[j], vb[j], (((0,), (0,)), ((), ())), preferred_element_type=f32) for j in heads]
    for j in heads:
        sl = slice(j * GDN_DK, (j + 1) * GDN_DK)
        state_ref[j] = state_ref[j] * eglast[j] + upd[j]
        ms = jnp.mean(o[j] * o[j], axis=-1, keepdims=True)
        z = z_ref[:, sl].astype(f32)
        o_ref[:, sl] = (o[j] * lax.rsqrt(ms + EPS) * nw_ref[...] * (z * jax.nn.sigmoid(z))).astype(o_ref.dtype)


def gdn_delta_rule(q, k, v, P, gb, norm_w, B, S):
    T = B * S
    C, HB = CHUNK, GDN_HB
    n = S // C
    ng = GDN_HEADS // HB
    W = HB * GDN_DK
    g = gb[:, SM_GA:SM_GA + GDN_HEADS]
    beta = gb[:, SM_GB:SM_GB + GDN_HEADS]
    gcol = g.reshape(T, ng, HB).transpose(1, 0, 2)
    bcol = beta.reshape(T, ng, HB).transpose(1, 0, 2)
    grow = g.reshape(B * n, C, ng, HB).transpose(2, 0, 3, 1)
    blk = lambda off: pl.BlockSpec((C, W), lambda b, hg, c: (b * n + c, off // W + hg))
    colspec = pl.BlockSpec((None, C, HB), lambda b, hg, c: (hg, b * n + c, 0))
    return pl.pallas_call(
        _gdn_kernel,
        out_shape=jax.ShapeDtypeStruct((T, GDN_WIDTH), jnp.bfloat16),
        grid=(B, ng, n),
        in_specs=[blk(0), blk(0), blk(0), blk(OFF_GZ), colspec, colspec,
                  pl.BlockSpec((None, None, HB, C), lambda b, hg, c: (hg, b * n + c, 0, 0)),
                  pl.BlockSpec((1, GDN_DV), lambda b, hg, c: (0, 0))],
        out_specs=blk(0),
        scratch_shapes=[pltpu.VMEM((HB, GDN_DK, GDN_DV), jnp.float32)],
        compiler_params=_cparams(("parallel", "parallel", "arbitrary")),
        name="gdn_delta_rule",
    )(q, k, v, P, gcol, bcol, grow, norm_w.reshape(1, GDN_DV))


def _merge_kernel(ya_ref, yg_ref, wa_ref, wg_ref, ga_ref, gg_ref, o_ref):
    pa = jnp.dot(ya_ref[...], wa_ref[...], preferred_element_type=jnp.float32)
    pg = jnp.dot(yg_ref[...], wg_ref[...], preferred_element_type=jnp.float32)
    ga = jax.nn.sigmoid(ga_ref[...].astype(jnp.float32))
    gg = jax.nn.sigmoid(gg_ref[...].astype(jnp.float32))
    o_ref[...] = (ga * pa + gg * pg).astype(o_ref.dtype)


def gated_merge(ya, yg, wa, wg, P, D, tm=1024, tn=512):
    T = ya.shape[0]
    tm, tn = _tile(T, tm), _tile(D, tn)
    goff = OFF_GATES // tn
    return pl.pallas_call(
        _merge_kernel,
        out_shape=jax.ShapeDtypeStruct((T, D), jnp.bfloat16),
        grid=(D // tn, T // tm),
        in_specs=[pl.BlockSpec((tm, ATT_Q_WIDTH), lambda j, i: (i, 0)),
                  pl.BlockSpec((tm, GDN_WIDTH), lambda j, i: (i, 0)),
                  pl.BlockSpec((ATT_Q_WIDTH, tn), lambda j, i: (0, j)),
                  pl.BlockSpec((GDN_WIDTH, tn), lambda j, i: (0, j)),
                  pl.BlockSpec((tm, tn), lambda j, i: (i, goff + j)),
                  pl.BlockSpec((tm, tn), lambda j, i: (i, goff + D // tn + j))],
        out_specs=pl.BlockSpec((tm, tn), lambda j, i: (i, j)),
        compiler_params=_cparams(("parallel", "parallel")),
        name="gated_merge",
    )(ya, yg, wa, wg, P, P)


def _outproj_kernel(a_ref, w_ref, x_ref, o_ref):
    o_ref[...] = x_ref[...] + jnp.dot(a_ref[...], w_ref[...], preferred_element_type=jnp.float32)


def out_proj_residual(a, w, x, tm=1024, tn=512):
    T, D = x.shape
    K = a.shape[1]
    tm, tn = _tile(T, tm), _tile(D, tn)
    return pl.pallas_call(
        _outproj_kernel,
        out_shape=jax.ShapeDtypeStruct((T, D), jnp.float32),
        grid=(D // tn, T // tm),
        in_specs=[pl.BlockSpec((tm, K), lambda j, i: (i, 0)),
                  pl.BlockSpec((K, tn), lambda j, i: (0, j)),
                  pl.BlockSpec((tm, tn), lambda j, i: (i, j))],
        out_specs=pl.BlockSpec((tm, tn), lambda j, i: (i, j)),
        compiler_params=_cparams(("parallel", "parallel")),
        name="out_proj_residual",
    )(a, w, x)


def _ffn_norm_router_kernel(x_ref, w_ref, rhi_ref, rlo_ref, h_ref, lg_ref):
    x = x_ref[...]
    ms = jnp.mean(x * x, axis=-1, keepdims=True)
    h = x * lax.rsqrt(ms + EPS) * w_ref[...]
    h_ref[...] = h
    hh = h.astype(jnp.bfloat16)
    hl = (h - hh.astype(jnp.float32)).astype(jnp.bfloat16)
    f32 = jnp.float32
    lg_ref[...] = (jnp.dot(hh, rhi_ref[...], preferred_element_type=f32)
                   + jnp.dot(hh, rlo_ref[...], preferred_element_type=f32)
                   + jnp.dot(hl, rhi_ref[...], preferred_element_type=f32))


def ffn_norm_router(x1, norm_w, w_router, tm=256):
    T, D = x1.shape
    tm = _tile(T, tm)
    rhi = w_router.astype(jnp.bfloat16)
    rlo = (w_router - rhi.astype(jnp.float32)).astype(jnp.bfloat16)
    return pl.pallas_call(
        _ffn_norm_router_kernel,
        out_shape=[jax.ShapeDtypeStruct((T, D), jnp.float32), jax.ShapeDtypeStruct((T, LANES), jnp.float32)],
        grid=(T // tm,),
        in_specs=[pl.BlockSpec((tm, D), lambda i: (i, 0)), pl.BlockSpec((1, D), lambda i: (0, 0)),
                  pl.BlockSpec((D, LANES), lambda i: (0, 0)), pl.BlockSpec((D, LANES), lambda i: (0, 0))],
        out_specs=[pl.BlockSpec((tm, D), lambda i: (i, 0)), pl.BlockSpec((tm, LANES), lambda i: (i, 0))],
        compiler_params=_cparams(("parallel",)),
        name="ffn_norm_router",
    )(x1, norm_w.reshape(1, D), rhi, rlo)


def _routing_kernel(lg_ref, b_ref, eid_ref, wt_ref):
    lg = lg_ref[...] + b_ref[...]
    lane = lax.broadcasted_iota(jnp.int32, lg.shape, 1)
    ninf = -jnp.inf

    def first_argmax(vals, vmax):
        return jnp.min(jnp.where(vals == vmax, lane, LANES), axis=-1, keepdims=True)

    glog = jnp.where(lane < N_GROUPS, lg, ninf)
    gmax = jnp.max(glog, axis=-1, keepdims=True)
    p_grp = 1.0 / jnp.sum(jnp.exp(glog - gmax), axis=-1, keepdims=True)
    grp = first_argmax(glog, gmax)
    base = RT_EXP + grp * EXPERTS_PER_GROUP
    elog = jnp.where((lane >= base) & (lane < base + EXPERTS_PER_GROUP), lg, ninf)
    emax = jnp.max(elog, axis=-1, keepdims=True)
    idx1 = first_argmax(elog, emax)
    elog2 = jnp.where(lane == idx1, ninf, elog)
    emax2 = jnp.max(elog2, axis=-1, keepdims=True)
    idx2 = first_argmax(elog2, emax2)
    e2 = jnp.exp(emax2 - emax)
    w1 = p_grp / (1.0 + e2)
    w2 = p_grp * e2 / (1.0 + e2)
    eid_ref[...] = jnp.where(lane == 0, idx1 - RT_EXP, jnp.where(lane == 1, idx2 - RT_EXP, 0))
    wt_ref[...] = jnp.where(lane == 0, w1, jnp.where(lane == 1, w2, 0.0))


def routing(logits, bias_row, tm=512):
    T = logits.shape[0]
    tm = _tile(T, tm)
    spec = pl.BlockSpec((tm, LANES), lambda i: (i, 0))
    return pl.pallas_call(
        _routing_kernel,
        out_shape=[jax.ShapeDtypeStruct((T, LANES), jnp.int32), jax.ShapeDtypeStruct((T, LANES), jnp.float32)],
        grid=(T // tm,),
        in_specs=[spec, pl.BlockSpec((1, LANES), lambda i: (0, 0))],
        out_specs=[spec, spec],
        compiler_params=_cparams(("parallel",)),
        name="routing",
    )(logits, bias_row)


GATHER_UNROLL = 8


def _row_gather_copy(src_hbm, row, dst, r, sem):
    return pltpu.make_async_copy(src_hbm.at[pl.ds(row, 1), :], dst.at[pl.ds(r, 1), :], sem)


def _moe_ffn_kernel(texp_ref, nused_ref, rowtok_ref, h_hbm, wg_ref, wu_ref, wd_ref, o_ref,
                    xbuf, sem, *, tm):
    i = pl.program_id(0)
    nused = nused_ref[0]

    ntiles = pl.num_programs(0)
    slot = i % 2

    def start_gather(tile, slot):
        for r in range(tm):
            _row_gather_copy(h_hbm, rowtok_ref[tile * tm + r], xbuf.at[slot], r, sem.at[slot]).start()

    def wait_gather(slot):
        pltpu.make_async_copy(h_hbm.at[pl.ds(0, tm), :], xbuf.at[slot], sem.at[slot]).wait()

    @pl.when(i == 0)
    def _():
        start_gather(0, 0)

    wait_gather(slot)
    nxt = jnp.where(i + 1 < ntiles, i + 1, 0)

    @pl.when(i < nused)
    def _():
        start_gather(nxt, 1 - slot)
        x = xbuf[slot].astype(jnp.bfloat16)
        g = jnp.dot(x, wg_ref[0], preferred_element_type=jnp.float32)
        u = jnp.dot(x, wu_ref[0], preferred_element_type=jnp.float32)
        hmid = (g * jax.nn.sigmoid(g) * u).astype(jnp.bfloat16)
        y = jnp.dot(hmid, wd_ref[0], preferred_element_type=jnp.float32)
        o_ref[...] = y

    @pl.when(i >= nused)
    def _():
        start_gather(nxt, 1 - slot)
        o_ref[...] = jnp.zeros(o_ref.shape, o_ref.dtype)

    @pl.when(i == ntiles - 1)
    def _():
        wait_gather(1 - slot)


def moe_ffn(h2, tile_expert, n_used, row_token, wg, wu, wd, tm):
    T, D = h2.shape
    R = row_token.shape[0]
    FF = wg.shape[2]
    ntiles = R // tm
    grid_spec = pltpu.PrefetchScalarGridSpec(
        num_scalar_prefetch=3,
        grid=(ntiles,),
        in_specs=[pl.BlockSpec(memory_space=pl.ANY),
                  pl.BlockSpec((1, D, FF), lambda i, te, nu, rt: (te[i], 0, 0)),
                  pl.BlockSpec((1, D, FF), lambda i, te, nu, rt: (te[i], 0, 0)),
                  pl.BlockSpec((1, FF, D), lambda i, te, nu, rt: (te[i], 0, 0))],
        out_specs=pl.BlockSpec((tm, D), lambda i, te, nu, rt: (i, 0)),
        scratch_shapes=[pltpu.VMEM((2, tm, D), jnp.float32), pltpu.SemaphoreType.DMA((2,))],
    )
    return pl.pallas_call(
        functools.partial(_moe_ffn_kernel, tm=tm),
        out_shape=jax.ShapeDtypeStruct((R, D), jnp.float32),
        grid_spec=grid_spec,
        compiler_params=_cparams(("arbitrary",)),
        name="moe_ffn",
    )(tile_expert, n_used, row_token, h2, wg, wu, wd)


def _combine_kernel(dest_ref, x_ref, wt_ref, ys_hbm, o_ref, buf, sem, *, tm):
    i = pl.program_id(0)
    n = pl.num_programs(0)

    def start_gather(tile, slot):
        def body(r, carry):
            for kk in range(TOP_K_EXPERTS):
                row = dest_ref[(tile * tm + r) * TOP_K_EXPERTS + kk]
                _row_gather_copy(ys_hbm, row, buf.at[slot, kk], r, sem.at[slot]).start()
            return carry
        lax.fori_loop(0, tm, body, 0, unroll=GATHER_UNROLL)

    def wait_gather(slot):
        for kk in range(TOP_K_EXPERTS):
            pltpu.make_async_copy(ys_hbm.at[pl.ds(0, tm), :], buf.at[slot, kk], sem.at[slot]).wait()

    @pl.when(i == 0)
    def _():
        start_gather(0, 0)

    @pl.when(i + 1 < n)
    def _():
        start_gather(i + 1, (i + 1) % 2)

    slot = i % 2
    wait_gather(slot)
    wt = wt_ref[...]
    o_ref[...] = x_ref[...] + wt[:, 0:1] * buf[slot, 0] + wt[:, 1:2] * buf[slot, 1]


def moe_combine(x1, wt_lanes, ys, dest_row, tm=128):
    T, D = x1.shape
    tm = _tile(T, tm)
    grid_spec = pltpu.PrefetchScalarGridSpec(
        num_scalar_prefetch=1,
        grid=(T // tm,),
        in_specs=[pl.BlockSpec((tm, D), lambda i, d: (i, 0)), pl.BlockSpec((tm, LANES), lambda i, d: (i, 0)),
                  pl.BlockSpec(memory_space=pl.ANY)],
        out_specs=pl.BlockSpec((tm, D), lambda i, d: (i, 0)),
        scratch_shapes=[pltpu.VMEM((2, TOP_K_EXPERTS, tm, D), jnp.float32), pltpu.SemaphoreType.DMA((2,))],
    )
    return pl.pallas_call(
        functools.partial(_combine_kernel, tm=tm),
        out_shape=jax.ShapeDtypeStruct((T, D), jnp.float32),
        grid_spec=grid_spec,
        compiler_params=_cparams(("arbitrary",)),
        name="moe_combine",
    )(dest_row, x1, wt_lanes, ys)


def moe_dispatch_plan(eid, tm):
    T = eid.shape[0]
    A = T * TOP_K_EXPERTS
    e_flat = eid.reshape(A)
    onehot = (e_flat[:, None] == jnp.arange(N_EXPERTS, dtype=jnp.int32)[None, :]).astype(jnp.int32)
    csum = jnp.cumsum(onehot, axis=0)
    rank = jnp.sum((csum - onehot) * onehot, axis=1)
    counts = csum[-1]
    padded = ((counts + tm - 1) // tm) * tm
    pend = jnp.cumsum(padded)
    pstart = pend - padded
    dest_row = (pstart[e_flat] + rank).astype(jnp.int32)
    R = ((A + N_EXPERTS * (tm - 1)) + tm - 1) // tm * tm
    row_token = jnp.zeros((R,), jnp.int32).at[dest_row].set(jnp.arange(A, dtype=jnp.int32) // TOP_K_EXPERTS)
    tile_start = jnp.arange(R // tm, dtype=jnp.int32) * tm
    tile_expert = jnp.minimum(jnp.sum(tile_start[:, None] >= pend[None, :], axis=1), N_EXPERTS - 1).astype(jnp.int32)
    n_used = (pend[-1] // tm).astype(jnp.int32).reshape(1)
    return dest_row, row_token, tile_expert, n_used


def _split_w_in(w, D):
    sizes = (ATT_Q_WIDTH, ATT_KV_WIDTH, ATT_KV_WIDTH, IDX_Q_WIDTH, IDX_DIM, IDX_HEADS,
             GDN_WIDTH, GDN_WIDTH, GDN_WIDTH, GDN_WIDTH, GDN_HEADS, GDN_HEADS, D, D)
    offs = [0]
    for s in sizes:
        offs.append(offs[-1] + s)
    w = w.astype(jnp.bfloat16)
    seg = [w[:, offs[i]:offs[i + 1]] for i in range(len(sizes))]
    aq, ak, av, iq, ik, iw, gq, gk, gv, gz, ga, gb, gate_a, gate_g = seg
    w_big = jnp.concatenate([aq, ak, av, iq, gq, gk, gv, gz, gate_a, gate_g], axis=1)
    pad = jnp.zeros((w.shape[0], LANES - (IDX_DIM + IDX_HEADS + 2 * GDN_HEADS)), w.dtype)
    w_small = jnp.concatenate([ik, iw, ga, gb, pad], axis=1)
    return w_big, w_small


def kernel(x, positions, mix_norm_w, w_in, q_norm_w, k_norm_w, idx_k_norm_w, conv_w, a_log, dt_bias, gdn_norm_w, w_proj_attn, w_proj_gdn, w_out, ffn_norm_w, w_router_group, b_router_group, w_router_expert, b_router_expert, w_gate, w_up, w_down):
    B, S, D = x.shape
    T = B * S
    bf16 = jnp.bfloat16
    moe_tm = 256 if T * TOP_K_EXPERTS >= 256 * N_EXPERTS else 64
    xt = x.reshape(T, D)
    cos_t, sin_t = rope_tables(positions)
    for l in range(w_in.shape[0]):
        w_big, w_small = _split_w_in(w_in[l], D)
        h = rmsnorm(xt, mix_norm_w[l], bf16)
        P = matmul(h, w_big, bf16, name="in_proj")
        Psm = matmul(h, w_small, jnp.float32, name="in_proj_small")
        q, k, qi, ki, wi = attn_prep(P, Psm, cos_t, sin_t, q_norm_w[l], k_norm_w[l], idx_k_norm_w[l])
        bias = indexer_mask(qi, ki, wi, B, S)
        y_attn = masked_attention(q, k, P, bias, B, S)
        gq, gk, gv, gb = gdn_prep(P, Psm, conv_w[l], a_log[l], dt_bias[l], B, S)
        y_gdn = gdn_delta_rule(gq, gk, gv, P, gb, gdn_norm_w[l], B, S)
        mixed = gated_merge(y_attn, y_gdn, w_proj_attn[l].astype(bf16), w_proj_gdn[l].astype(bf16), P, D)
        x1 = out_proj_residual(mixed, w_out[l].astype(bf16), xt)
        w_router = jnp.zeros((D, LANES), jnp.float32)
        w_router = w_router.at[:, :N_GROUPS].set(w_router_group[l])
        w_router = w_router.at[:, RT_EXP:RT_EXP + N_EXPERTS].set(
            w_router_expert[l].transpose(1, 0, 2).reshape(D, N_EXPERTS))
        b_router = jnp.zeros((1, LANES), jnp.float32)
        b_router = b_router.at[0, :N_GROUPS].set(b_router_group[l])
        b_router = b_router.at[0, RT_EXP:RT_EXP + N_EXPERTS].set(b_router_expert[l].reshape(N_EXPERTS))
        h2, logits = ffn_norm_router(x1, ffn_norm_w[l], w_router)
        eid_l, wt_l = routing(logits, b_router)
        dest_row, row_token, tile_expert, n_used = moe_dispatch_plan(eid_l[:, :TOP_K_EXPERTS], moe_tm)
        ys = moe_ffn(h2, tile_expert, n_used, row_token,
                     w_gate[l].astype(bf16), w_up[l].astype(bf16), w_down[l].astype(bf16), moe_tm)
        xt = moe_combine(x1, wt_l, ys, dest_row)
    return xt.reshape(B, S, D)
```

```python
import functools
import math

import jax
import jax.numpy as jnp
from jax import lax
from jax.experimental import pallas as pl
from jax.experimental.pallas import tpu as pltpu

ATT_HEADS = 16
ATT_KV_HEADS = 4
HEAD_DIM = 128
IDX_HEADS = 16
IDX_DIM = 64
TOPK_MAX = 256
ROPE_THETA = 10000.0
GDN_HEADS = 16
GDN_DK = 128
GDN_DV = 128
CONV_WIDTH = 4
CHUNK = 64
N_GROUPS = 4
EXPERTS_PER_GROUP = 8
N_EXPERTS = N_GROUPS * EXPERTS_PER_GROUP
TOP_K_EXPERTS = 2
EPS = 1e-6

ATT_Q_WIDTH = ATT_HEADS * HEAD_DIM
ATT_KV_WIDTH = ATT_KV_HEADS * HEAD_DIM
IDX_Q_WIDTH = IDX_HEADS * IDX_DIM
GDN_WIDTH = GDN_HEADS * GDN_DK

LANES = 128
BF16_SUBLANES = 16
VMEM_LIMIT = 56 * 1024 * 1024
NEG_BIG = -1e30
LOG2E = math.log2(math.e)
SCORE_MASKED = 3.0e38
BISECT_MAX_STEPS = 192
BISECT_UNROLL = 4
COUNT_ROWS = 128

OFF_AQ = 0
OFF_AK = OFF_AQ + ATT_Q_WIDTH
OFF_AV = OFF_AK + ATT_KV_WIDTH
OFF_IQ = OFF_AV + ATT_KV_WIDTH
OFF_GQ = OFF_IQ + IDX_Q_WIDTH
OFF_GK = OFF_GQ + GDN_WIDTH
OFF_GV = OFF_GK + GDN_WIDTH
OFF_GZ = OFF_GV + GDN_WIDTH
OFF_GATES = OFF_GZ + GDN_WIDTH
SM_IK = 0
SM_IW = SM_IK + IDX_DIM
SM_GA = SM_IW + IDX_HEADS
SM_GB = SM_GA + GDN_HEADS
RT_EXP = 8


def _cparams(sem):
    return pltpu.CompilerParams(dimension_semantics=sem, vmem_limit_bytes=VMEM_LIMIT)


def _tile(n, pref):
    t = min(n, pref)
    assert n % t == 0, (n, pref)
    return t


def _rmsnorm_kernel(x_ref, w_ref, o_ref):
    x = x_ref[...]
    ms = jnp.mean(x * x, axis=-1, keepdims=True)
    o_ref[...] = (x * lax.rsqrt(ms + EPS) * w_ref[...]).astype(o_ref.dtype)


def rmsnorm(x, w, out_dtype, tm=256):
    T, D = x.shape
    tm = _tile(T, tm)
    return pl.pallas_call(
        _rmsnorm_kernel,
        out_shape=jax.ShapeDtypeStruct((T, D), out_dtype),
        grid=(T // tm,),
        in_specs=[pl.BlockSpec((tm, D), lambda i: (i, 0)), pl.BlockSpec((1, D), lambda i: (0, 0))],
        out_specs=pl.BlockSpec((tm, D), lambda i: (i, 0)),
        compiler_params=_cparams(("parallel",)),
        name="rmsnorm",
    )(x, w.reshape(1, D))


def _matmul_kernel(a_ref, b_ref, o_ref):
    o_ref[...] = jnp.dot(a_ref[...], b_ref[...], preferred_element_type=jnp.float32).astype(o_ref.dtype)


def matmul(a, b, out_dtype, tm=1024, tn=1024, name="matmul"):
    M, K = a.shape
    _, N = b.shape
    tm, tn = _tile(M, tm), _tile(N, tn)
    return pl.pallas_call(
        _matmul_kernel,
        out_shape=jax.ShapeDtypeStruct((M, N), out_dtype),
        grid=(N // tn, M // tm),
        in_specs=[pl.BlockSpec((tm, K), lambda j, i: (i, 0)), pl.BlockSpec((K, tn), lambda j, i: (0, j))],
        out_specs=pl.BlockSpec((tm, tn), lambda j, i: (i, j)),
        compiler_params=_cparams(("parallel", "parallel")),
        name=name,
    )(a, b)


def _matmul_cast_kernel(a_ref, b_ref, *refs):
    ncast = (len(refs) - 1) // 2
    o_ref = refs[ncast]
    o_ref[...] = jnp.dot(a_ref[...], b_ref[...], preferred_element_type=jnp.float32).astype(o_ref.dtype)
    for x_ref, y_ref in zip(refs[:ncast], refs[ncast + 1:]):
        y_ref[...] = x_ref[...].astype(y_ref.dtype)


def _cast_block_rows(rows, nsteps):
    for rb in range(BF16_SUBLANES, rows + 1, BF16_SUBLANES):
        if rows % rb == 0 and rows // rb <= nsteps:
            return rb
    raise ValueError((rows, nsteps))


def matmul_with_casts(a, b, out_dtype, to_cast, tm=1024, tn=1024, name="matmul_casts"):
    M, K = a.shape
    _, N = b.shape
    tm, tn = _tile(M, tm), _tile(N, tn)
    ni = M // tm
    nsteps = (N // tn) * ni
    cast_in, cast_out, cast_shapes = [], [], []
    for x in to_cast:
        rows, cols = x.shape
        rb = _cast_block_rows(rows, nsteps)
        last = rows // rb - 1
        imap = functools.partial(lambda j, i, last: (jnp.minimum(j * ni + i, last), 0), last=last)
        cast_in.append(pl.BlockSpec((rb, cols), imap))
        cast_out.append(pl.BlockSpec((rb, cols), imap))
        cast_shapes.append(jax.ShapeDtypeStruct((rows, cols), jnp.bfloat16))
    outs = pl.pallas_call(
        _matmul_cast_kernel,
        out_shape=[jax.ShapeDtypeStruct((M, N), out_dtype)] + cast_shapes,
        grid=(N // tn, ni),
        in_specs=[pl.BlockSpec((tm, K), lambda j, i: (i, 0)), pl.BlockSpec((K, tn), lambda j, i: (0, j))] + cast_in,
        out_specs=[pl.BlockSpec((tm, tn), lambda j, i: (i, j))] + cast_out,
        compiler_params=_cparams(("arbitrary", "arbitrary")),
        name=name,
    )(a, b, *to_cast)
    return outs[0], outs[1:]


def _rope_table_kernel(pos_ref, inv_ref, sgn_ref, cos_ref, sin_ref):
    ang = pos_ref[...] * inv_ref[...]
    cos_ref[...] = jnp.cos(ang)
    sin_ref[...] = jnp.sin(ang) * sgn_ref[...]


def rope_tables(positions):
    T = positions.size
    pos = positions.reshape(T, 1).astype(jnp.float32)

    def inv(d):
        return ROPE_THETA ** (-jnp.arange(0, d, 2, dtype=jnp.float32) / d)

    i128, i64 = inv(HEAD_DIM), inv(IDX_DIM)
    inv_row = jnp.concatenate([i128, i128, i64, i64, i64, i64]).reshape(1, 2 * LANES)
    s128 = jnp.concatenate([-jnp.ones(HEAD_DIM // 2), jnp.ones(HEAD_DIM // 2)])
    s64 = jnp.concatenate([-jnp.ones(IDX_DIM // 2), jnp.ones(IDX_DIM // 2)])
    sgn_row = jnp.concatenate([s128, s64, s64]).astype(jnp.float32).reshape(1, 2 * LANES)
    tm = _tile(T, 512)
    spec = pl.BlockSpec((tm, 2 * LANES), lambda i: (i, 0))
    row = pl.BlockSpec((1, 2 * LANES), lambda i: (0, 0))
    return pl.pallas_call(
        _rope_table_kernel,
        out_shape=[jax.ShapeDtypeStruct((T, 2 * LANES), jnp.float32)] * 2,
        grid=(T // tm,),
        in_specs=[pl.BlockSpec((tm, 1), lambda i: (i, 0)), row, row],
        out_specs=[spec, spec],
        compiler_params=_cparams(("parallel",)),
        name="rope_tables",
    )(pos, inv_row, sgn_row)


def _rope128(x, cos, sin_signed):
    return x * cos + pltpu.roll(x, HEAD_DIM // 2, 1) * sin_signed


def _rope64x2(x, cos, sin_signed, lane):
    half = IDX_DIM // 2
    first = (lane % IDX_DIM) < half
    partner = jnp.where(first, pltpu.roll(x, LANES - half, 1), pltpu.roll(x, half, 1))
    return x * cos + partner * sin_signed


def _attn_prep_kernel(aq_ref, ak_ref, iq_ref, sm_ref, cos_ref, sin_ref, qw_ref, kw_ref, ikw_ref,
                      q_ref, k_ref, qi_ref, ki_ref, wi_ref):
    cos_a, sin_a = cos_ref[:, :LANES], sin_ref[:, :LANES]
    cos_i, sin_i = cos_ref[:, LANES:], sin_ref[:, LANES:]
    tm = cos_a.shape[0]
    lane = lax.broadcasted_iota(jnp.int32, (tm, LANES), 1)

    def head_norm(xh, w):
        ms = jnp.mean(xh * xh, axis=-1, keepdims=True)
        return xh * lax.rsqrt(ms + EPS) * w

    for h in range(ATT_HEADS):
        sl = slice(h * HEAD_DIM, (h + 1) * HEAD_DIM)
        xh = head_norm(aq_ref[:, sl].astype(jnp.float32), qw_ref[...])
        q_ref[:, sl] = (_rope128(xh, cos_a, sin_a) * (LOG2E * HEAD_DIM ** -0.5)).astype(q_ref.dtype)
    for h in range(ATT_KV_HEADS):
        sl = slice(h * HEAD_DIM, (h + 1) * HEAD_DIM)
        xh = head_norm(ak_ref[:, sl].astype(jnp.float32), kw_ref[...])
        k_ref[:, sl] = _rope128(xh, cos_a, sin_a).astype(k_ref.dtype)
    for p in range(IDX_Q_WIDTH // LANES):
        sl = slice(p * LANES, (p + 1) * LANES)
        xp = iq_ref[:, sl].astype(jnp.float32)
        qi_ref[:, sl] = (_rope64x2(xp, cos_i, sin_i, lane) * (IDX_DIM ** -0.5)).astype(qi_ref.dtype)
    sm = sm_ref[...]
    in_k = lane < IDX_DIM
    xk = jnp.where(in_k, sm, 0.0)
    ms = jnp.sum(xk * xk, axis=-1, keepdims=True) * (1.0 / IDX_DIM)
    kn = xk * lax.rsqrt(ms + EPS) * ikw_ref[...]
    kr = jnp.where(in_k, _rope64x2(kn, cos_i, sin_i, lane), 0.0)
    ki_ref[:, :LANES] = kr.astype(ki_ref.dtype)
    ki_ref[:, LANES:] = pltpu.roll(kr, IDX_DIM, 1).astype(ki_ref.dtype)
    wi_ref[...] = sm * (IDX_HEADS ** -0.5)


def attn_prep(P, Psm, cos_t, sin_t, q_norm_w, k_norm_w, idx_k_norm_w, tm=256):
    T = P.shape[0]
    tm = _tile(T, tm)
    ikw = jnp.concatenate([idx_k_norm_w, jnp.zeros((LANES - IDX_DIM,), jnp.float32)]).reshape(1, LANES)
    row = lambda w: pl.BlockSpec((1, w), lambda i: (0, 0))
    return pl.pallas_call(
        _attn_prep_kernel,
        out_shape=[jax.ShapeDtypeStruct((T, ATT_Q_WIDTH), jnp.bfloat16),
                   jax.ShapeDtypeStruct((T, ATT_KV_WIDTH), jnp.bfloat16),
                   jax.ShapeDtypeStruct((T, IDX_Q_WIDTH), jnp.bfloat16),
                   jax.ShapeDtypeStruct((T, 2 * LANES), jnp.bfloat16),
                   jax.ShapeDtypeStruct((T, LANES), jnp.float32)],
        grid=(T // tm,),
        in_specs=[pl.BlockSpec((tm, ATT_Q_WIDTH), lambda i: (i, OFF_AQ // ATT_Q_WIDTH)),
                  pl.BlockSpec((tm, ATT_KV_WIDTH), lambda i: (i, OFF_AK // ATT_KV_WIDTH)),
                  pl.BlockSpec((tm, IDX_Q_WIDTH), lambda i: (i, OFF_IQ // IDX_Q_WIDTH)),
                  pl.BlockSpec((tm, LANES), lambda i: (i, 0)),
                  pl.BlockSpec((tm, 2 * LANES), lambda i: (i, 0)),
                  pl.BlockSpec((tm, 2 * LANES), lambda i: (i, 0)),
                  row(LANES), row(LANES), row(LANES)],
        out_specs=[pl.BlockSpec((tm, ATT_Q_WIDTH), lambda i: (i, 0)),
                   pl.BlockSpec((tm, ATT_KV_WIDTH), lambda i: (i, 0)),
                   pl.BlockSpec((tm, IDX_Q_WIDTH), lambda i: (i, 0)),
                   pl.BlockSpec((tm, 2 * LANES), lambda i: (i, 0)),
                   pl.BlockSpec((tm, LANES), lambda i: (i, 0))],
        compiler_params=_cparams(("parallel",)),
        name="attn_prep",
    )(P, P, P, Psm, cos_t, sin_t, q_norm_w.reshape(1, LANES), k_norm_w.reshape(1, LANES), ikw)


def _lane_fold(x, op):
    s = x[:, :LANES]
    for j in range(1, x.shape[1] // LANES):
        s = op(s, x[:, j * LANES:(j + 1) * LANES])
    return s


def _indexer_kernel(qi_ref, ki_ref, wi_ref, bias_ref, key_ref, *, n_sel, tq, tk, nchunks):
    q0 = pl.program_id(1) * tq
    nck = (q0 + tq + tk - 1) // tk
    wi = wi_ref[...]
    rowpos = q0 + lax.broadcasted_iota(jnp.int32, (tq, tk), 0)
    colpos0 = lax.broadcasted_iota(jnp.int32, (tq, tk), 1)

    def score_chunk(c, carry):
        mx, mn = carry
        ks = ki_ref[pl.ds(pl.multiple_of(c * tk, tk), tk), :]
        acc = jnp.zeros((tq, tk), jnp.float32)
        for h in range(IDX_HEADS):
            qp = qi_ref[:, (h // 2) * LANES:(h // 2 + 1) * LANES]
            kh = ks[:, (h % 2) * LANES:(h % 2 + 1) * LANES]
            d = lax.dot_general(qp, kh, (((1,), (1,)), ((), ())), preferred_element_type=jnp.float32)
            acc = acc + wi[:, SM_IW + h:SM_IW + h + 1] * jnp.maximum(d, 0.0)
        causal = colpos0 + c * tk <= rowpos
        key_ref[c] = jnp.where(causal, acc, -SCORE_MASKED)
        mx = jnp.maximum(mx, _lane_fold(jnp.where(causal, acc, -SCORE_MASKED), jnp.maximum))
        mn = jnp.minimum(mn, _lane_fold(jnp.where(causal, acc, SCORE_MASKED), jnp.minimum))
        return mx, mn

    mx, mn = lax.fori_loop(0, nck, score_chunk, (jnp.full((tq, LANES), -SCORE_MASKED, jnp.float32),
                                                 jnp.full((tq, LANES), SCORE_MASKED, jnp.float32)))

    ones = jnp.ones((LANES, LANES), jnp.bfloat16)

    def count_ge(probe):
        accs = []
        for r0 in range(0, tq, COUNT_ROWS):
            rows = slice(r0, min(r0 + COUNT_ROWS, tq))

            def body(c, acc, rows=rows):
                for j in range(tk // LANES):
                    acc = acc + jnp.where(key_ref[c, rows, j * LANES:(j + 1) * LANES] >= probe[rows], 1.0, 0.0)
                return acc

            accs.append(lax.fori_loop(0, nck, body, jnp.zeros((rows.stop - rows.start, LANES), jnp.float32)))
        acc = accs[0] if len(accs) == 1 else jnp.concatenate(accs, axis=0)
        return jnp.dot(acc.astype(jnp.bfloat16), ones, preferred_element_type=jnp.float32)

    lo = jnp.broadcast_to(jnp.min(mn, axis=1, keepdims=True), (tq, LANES))
    hi = jnp.broadcast_to(jnp.max(mx, axis=1, keepdims=True), (tq, LANES))
    ncausal = q0 + lax.broadcasted_iota(jnp.int32, (tq, LANES), 0) + 1
    hi = jnp.where(ncausal <= n_sel, lo, hi)

    def unfinished(carry):
        return (carry[0] < BISECT_MAX_STEPS) & (carry[3] > 0)

    def bisect(carry):
        it, lo, hi, _ = carry
        for _ in range(BISECT_UNROLL):
            mid = 0.5 * lo + 0.5 * hi
            cnt = count_ge(mid)
            ok = cnt >= n_sel
            open_row = (mid > lo) & (mid < hi)
            lo = jnp.where(ok, mid, lo)
            hi = jnp.where(cnt == n_sel, mid, jnp.where(ok, hi, mid))
            open_row = open_row & (hi > lo)
        return it + BISECT_UNROLL, lo, hi, jnp.sum(jnp.where(open_row, 1, 0))

    _, thr, _, _ = lax.while_loop(unfinished, bisect, (jnp.int32(0), lo, hi, jnp.int32(1)))

    def write_chunk(c, carry):
        for j in range(tk // LANES):
            cols = slice(j * LANES, (j + 1) * LANES)
            bias_ref[c, :, cols] = jnp.where(key_ref[c, :, cols] >= thr, 0.0, NEG_BIG).astype(bias_ref.dtype)
        return carry

    def write_masked(c, carry):
        bias_ref[c] = jnp.full((tq, tk), NEG_BIG, bias_ref.dtype)
        return carry

    lax.fori_loop(0, nck, write_chunk, 0)
    lax.fori_loop(nck, nchunks, write_masked, 0)


def indexer_mask(qi, ki, wi, B, S, tq=512, tk=512):
    tq, tk = _tile(S, tq), _tile(S, tk)
    n_sel = min(TOPK_MAX, S // 4)
    nq, nchunks = S // tq, S // tk
    return pl.pallas_call(
        functools.partial(_indexer_kernel, n_sel=n_sel, tq=tq, tk=tk, nchunks=nchunks),
        out_shape=jax.ShapeDtypeStruct((B, nq, nchunks, tq, tk), jnp.bfloat16),
        grid=(B, nq),
        in_specs=[pl.BlockSpec((tq, IDX_Q_WIDTH), lambda b, i: (b * nq + i, 0)),
                  pl.BlockSpec((S, 2 * LANES), lambda b, i: (b, 0)),
                  pl.BlockSpec((tq, LANES), lambda b, i: (b * nq + i, 0))],
        out_specs=pl.BlockSpec((None, None, nchunks, tq, tk), lambda b, i: (b, i, 0, 0, 0)),
        scratch_shapes=[pltpu.VMEM((nchunks, tq, tk), jnp.float32)],
        compiler_params=_cparams(("parallel", "parallel")),
        name="indexer_mask",
    )(qi, ki, wi)


def _attn_kernel(q_ref, k_ref, v_ref, bias_ref, o_ref, m_ref, acc_ref, *, tq, tk):
    i, j = pl.program_id(1), pl.program_id(2)
    grp = ATT_HEADS // ATT_KV_HEADS

    @pl.when(j == 0)
    def _():
        m_ref[...] = jnp.full(m_ref.shape, -1e38, jnp.float32)
        acc_ref[...] = jnp.zeros(acc_ref.shape, jnp.float32)

    @pl.when(j * tk <= i * tq + tq - 1)
    def _():
        bias = bias_ref[...].astype(jnp.float32)
        ones = jnp.ones((tk, LANES), v_ref.dtype)
        v1 = [jnp.concatenate([v_ref[:, g * HEAD_DIM:(g + 1) * HEAD_DIM], ones], axis=1) for g in range(ATT_KV_HEADS)]
        s, m_new, alpha, p = {}, {}, {}, {}

        def scores(h):
            g = h // grp
            qh = q_ref[:, h * HEAD_DIM:(h + 1) * HEAD_DIM]
            kh = k_ref[:, g * HEAD_DIM:(g + 1) * HEAD_DIM]
            s[h] = lax.dot_general(qh, kh, (((1,), (1,)), ((), ())), preferred_element_type=jnp.float32) + bias
            m_prev = m_ref[h]
            m_new[h] = jnp.maximum(m_prev, jnp.max(s[h], axis=1, keepdims=True))
            alpha[h] = jnp.exp2(m_prev - m_new[h])
            m_ref[h] = m_new[h]

        def probs(h):
            p[h] = jnp.exp2(s.pop(h) - m_new.pop(h)[:, :1]).astype(v_ref.dtype)

        def values(h):
            a2 = jnp.concatenate([alpha[h], alpha.pop(h)], axis=1)
            acc_ref[h] = a2 * acc_ref[h] + jnp.dot(p.pop(h), v1[h // grp], preferred_element_type=jnp.float32)

        for t in range(ATT_HEADS + 2):
            if t < ATT_HEADS:
                scores(t)
            if 0 <= t - 1 < ATT_HEADS:
                probs(t - 1)
            if 0 <= t - 2 < ATT_HEADS:
                values(t - 2)

    @pl.when(j == pl.num_programs(2) - 1)
    def _():
        for h in range(ATT_HEADS):
            acc = acc_ref[h]
            o_ref[:, h * HEAD_DIM:(h + 1) * HEAD_DIM] = (acc[:, :HEAD_DIM] / acc[:, HEAD_DIM:]).astype(o_ref.dtype)


def masked_attention(q, k, P, bias, B, S):
    _, nq, nkv, tq, tk = bias.shape

    def kv_idx(i, j):
        return jnp.minimum(j, (i * tq + tq - 1) // tk)

    return pl.pallas_call(
        functools.partial(_attn_kernel, tq=tq, tk=tk),
        out_shape=jax.ShapeDtypeStruct((B * S, ATT_Q_WIDTH), jnp.bfloat16),
        grid=(B, nq, nkv),
        in_specs=[pl.BlockSpec((tq, ATT_Q_WIDTH), lambda b, i, j: (b * nq + i, 0)),
                  pl.BlockSpec((tk, ATT_KV_WIDTH), lambda b, i, j: (b * nkv + kv_idx(i, j), 0)),
                  pl.BlockSpec((tk, ATT_KV_WIDTH), lambda b, i, j: (b * nkv + kv_idx(i, j), OFF_AV // ATT_KV_WIDTH)),
                  pl.BlockSpec((None, None, None, tq, tk), lambda b, i, j: (b, i, kv_idx(i, j), 0, 0))],
        out_specs=pl.BlockSpec((tq, ATT_Q_WIDTH), lambda b, i, j: (b * nq + i, 0)),
        scratch_shapes=[pltpu.VMEM((ATT_HEADS, tq, LANES), jnp.float32),
                        pltpu.VMEM((ATT_HEADS, tq, HEAD_DIM + LANES), jnp.float32)],
        compiler_params=_cparams(("parallel", "parallel", "arbitrary")),
        name="masked_attention",
    )(q, k, P, bias)


GDN_HALO = 16


def _gdn_prep_kernel(xq_ref, xk_ref, xv_ref, hq_ref, hk_ref, hv_ref, sm_ref, cw_ref, alog_ref, dtb_ref,
                     q_ref, k_ref, v_ref, gb_ref, *, tt):
    first = pl.program_id(1) == 0

    def conv_silu(x_ref, h_ref, part):
        halo = jnp.where(first, 0.0, h_ref[...].astype(jnp.float32))
        xp = jnp.concatenate([halo, x_ref[...].astype(jnp.float32)], axis=0)
        w = cw_ref[:, part * GDN_WIDTH:(part + 1) * GDN_WIDTH]
        y = None
        for jj in range(CONV_WIDTH):
            off = GDN_HALO - (CONV_WIDTH - 1) + jj
            term = w[jj:jj + 1, :] * xp[off:off + tt, :]
            y = term if y is None else y + term
        return y * jax.nn.sigmoid(y)

    def l2(yh):
        return yh * lax.rsqrt(jnp.sum(yh * yh, axis=-1, keepdims=True) + EPS)

    yq = conv_silu(xq_ref, hq_ref, 0)
    yk = conv_silu(xk_ref, hk_ref, 1)
    yv = conv_silu(xv_ref, hv_ref, 2)
    for h in range(GDN_HEADS):
        sl = slice(h * GDN_DK, (h + 1) * GDN_DK)
        q_ref[:, sl] = (l2(yq[:, sl]) * (GDN_DK ** -0.5)).astype(q_ref.dtype)
        k_ref[:, sl] = l2(yk[:, sl]).astype(k_ref.dtype)
    v_ref[...] = yv.astype(v_ref.dtype)
    sm = sm_ref[...]
    a = sm + dtb_ref[...]
    softplus = jnp.maximum(a, 0.0) + jnp.log(1.0 + jnp.exp(-jnp.abs(a)))
    g = -jnp.exp(alog_ref[...]) * softplus
    beta = jax.nn.sigmoid(sm)
    lane = lax.broadcasted_iota(jnp.int32, sm.shape, 1)
    gb_ref[...] = jnp.where(lane < SM_GB, g, beta)


def gdn_prep(P, Psm, conv_w, a_log, dt_bias, B, S, tt=256):
    T = B * S
    tt = _tile(S, tt)
    nt = S // tt
    hpt = tt // GDN_HALO
    W = GDN_WIDTH

    def pad_lane(v, off):
        return jnp.zeros((1, LANES), jnp.float32).at[0, off:off + v.shape[0]].set(v)

    def xspec(off):
        return pl.BlockSpec((tt, W), lambda b, t: (b * nt + t, off // W))

    def hspec(off):
        return pl.BlockSpec((GDN_HALO, W), lambda b, t: (jnp.maximum((b * nt + t) * hpt - 1, 0), off // W))

    ospec = pl.BlockSpec((tt, W), lambda b, t: (b * nt + t, 0))
    return pl.pallas_call(
        functools.partial(_gdn_prep_kernel, tt=tt),
        out_shape=[jax.ShapeDtypeStruct((T, W), jnp.bfloat16)] * 3 + [jax.ShapeDtypeStruct((T, LANES), jnp.float32)],
        grid=(B, nt),
        in_specs=[xspec(OFF_GQ), xspec(OFF_GK), xspec(OFF_GV), hspec(OFF_GQ), hspec(OFF_GK), hspec(OFF_GV),
                  pl.BlockSpec((tt, LANES), lambda b, t: (b * nt + t, 0)),
                  pl.BlockSpec((CONV_WIDTH, 3 * W), lambda b, t: (0, 0)),
                  pl.BlockSpec((1, LANES), lambda b, t: (0, 0)),
                  pl.BlockSpec((1, LANES), lambda b, t: (0, 0))],
        out_specs=[ospec, ospec, ospec, pl.BlockSpec((tt, LANES), lambda b, t: (b * nt + t, 0))],
        compiler_params=_cparams(("parallel", "parallel")),
        name="gdn_prep",
    )(P, P, P, P, P, P, Psm, conv_w, pad_lane(a_log, SM_GA), pad_lane(dt_bias, SM_GA))


GDN_HB = 16


def _gdn_kernel(q_ref, k_ref, v_ref, z_ref, gcol_ref, bcol_ref, grow_ref, nw_ref, o_ref, state_ref):
    C = CHUNK
    f32, bf16 = jnp.float32, jnp.bfloat16

    @pl.when(pl.program_id(2) == 0)
    def _():
        state_ref[...] = jnp.zeros(state_ref.shape, f32)

    row = lax.broadcasted_iota(jnp.int32, (C, C), 0)
    col = lax.broadcasted_iota(jnp.int32, (C, C), 1)
    tril, strict = row >= col, row > col
    eye = (row == col).astype(f32)
    hi = lax.Precision.HIGHEST
    gc_col = jnp.dot(tril.astype(f32), gcol_ref[...], precision=hi, preferred_element_type=f32)
    gc_row = jnp.dot(grow_ref[...], (row <= col).astype(f32), precision=hi, preferred_element_type=f32)
    nt = (((1,), (1,)), ((), ()))
    heads = range(GDN_HB)
    dot = functools.partial(jnp.dot, preferred_element_type=f32)
    kq, a, intra, rhs, qd, kd, eglast = [], [], [], [], [], [], []
    for j in heads:
        sl = slice(j * GDN_DK, (j + 1) * GDN_DK)
        gc = gc_col[:, j:j + 1]
        glast = gc[C - 1:C, :]
        eg = jnp.exp(gc)
        beta = bcol_ref[:, j:j + 1]
        q, k, v = q_ref[:, sl], k_ref[:, sl], v_ref[:, sl]
        kf = k.astype(f32)
        kq.append(jnp.concatenate([k, q], axis=0))
        rhs.append(jnp.concatenate([(v.astype(f32) * beta).astype(bf16), (kf * (beta * eg)).astype(bf16)], axis=1))
        qd.append((q.astype(f32) * eg).astype(bf16))
        kd.append((kf * jnp.exp(glast - gc)).astype(bf16))
        eglast.append(jnp.exp(glast))
    skq = [lax.dot_general(kq[j], kq[j][:C], nt, preferred_element_type=f32) for j in heads]
    for j in heads:
        decay = jnp.exp(jnp.where(tril, gc_col[:, j:j + 1] - gc_row[j:j + 1, :], NEG_BIG))
        a.append(jnp.where(strict, skq[j][:C] * bcol_ref[:, j:j + 1] * decay, 0.0))
        intra.append((skq[j][C:] * decay).astype(bf16))
    ab = [a[j].astype(bf16) for j in heads]
    xb = [dot(ab[j], ab[j]).astype(bf16) for j in heads]
    tinv = [eye - a[j] for j in heads]
    for it in range(5):
        last = it == 4
        lhs = [tinv[j].astype(bf16) if last else jnp.concatenate([tinv[j].astype(bf16), xb[j]], axis=0) for j in heads]
        prod = [dot(lhs[j], xb[j]) for j in heads]
        tinv = [tinv[j] + prod[j][:C] for j in heads]
        if not last:
            xb = [prod[j][C:].astype(bf16) for j in heads]
    uw = [dot(tinv[j].astype(bf16), rhs[j]) for j in heads]
    sb = [state_ref[j].astype(bf16) for j in heads]
    ws = [dot(jnp.concatenate([uw[j][:, GDN_DV:].astype(bf16), qd[j]], axis=0), sb[j]) for j in heads]
    vb = [(uw[j][:, :GDN_DV] - ws[j][:C]).astype(bf16) for j in heads]
    o = [ws[j][C:] + dot(intra[j], vb[j]) for j in heads]
    upd = [lax.dot_general(kd[j], vb[j], (((0,), (0,)), ((), ())), preferred_element_type=f32) for j in heads]
    for j in heads:
        sl = slice(j * GDN_DK, (j + 1) * GDN_DK)
        state_ref[j] = state_ref[j] * eglast[j] + upd[j]
        ms = jnp.mean(o[j] * o[j], axis=-1, keepdims=True)
        z = z_ref[:, sl].astype(f32)
        o_ref[:, sl] = (o[j] * lax.rsqrt(ms + EPS) * nw_ref[...] * (z * jax.nn.sigmoid(z))).astype(o_ref.dtype)


def gdn_delta_rule(q, k, v, P, gb, norm_w, B, S):
    T = B * S
    C, HB = CHUNK, GDN_HB
    n = S // C
    ng = GDN_HEADS // HB
    W = HB * GDN_DK
    g = gb[:, SM_GA:SM_GA + GDN_HEADS]
    beta = gb[:, SM_GB:SM_GB + GDN_HEADS]
    gcol = g.reshape(T, ng, HB).transpose(1, 0, 2)
    bcol = beta.reshape(T, ng, HB).transpose(1, 0, 2)
    grow = g.reshape(B * n, C, ng, HB).transpose(2, 0, 3, 1)
    blk = lambda off: pl.BlockSpec((C, W), lambda b, hg, c: (b * n + c, off // W + hg))
    colspec = pl.BlockSpec((None, C, HB), lambda b, hg, c: (hg, b * n + c, 0))
    return pl.pallas_call(
        _gdn_kernel,
        out_shape=jax.ShapeDtypeStruct((T, GDN_WIDTH), jnp.bfloat16),
        grid=(B, ng, n),
        in_specs=[blk(0), blk(0), blk(0), blk(OFF_GZ), colspec, colspec,
                  pl.BlockSpec((None, None, HB, C), lambda b, hg, c: (hg, b * n + c, 0, 0)),
                  pl.BlockSpec((1, GDN_DV), lambda b, hg, c: (0, 0))],
        out_specs=blk(0),
        scratch_shapes=[pltpu.VMEM((HB, GDN_DK, GDN_DV), jnp.float32)],
        compiler_params=_cparams(("parallel", "parallel", "arbitrary")),
        name="gdn_delta_rule",
    )(q, k, v, P, gcol, bcol, grow, norm_w.reshape(1, GDN_DV))


def _merge_kernel(ya_ref, yg_ref, wa_ref, wg_ref, ga_ref, gg_ref, o_ref):
    pa = jnp.dot(ya_ref[...], wa_ref[...], preferred_element_type=jnp.float32)
    pg = jnp.dot(yg_ref[...], wg_ref[...], preferred_element_type=jnp.float32)
    ga = jax.nn.sigmoid(ga_ref[...].astype(jnp.float32))
    gg = jax.nn.sigmoid(gg_ref[...].astype(jnp.float32))
    o_ref[...] = (ga * pa + gg * pg).astype(o_ref.dtype)


def gated_merge(ya, yg, wa, wg, P, D, tm=1024, tn=512):
    T = ya.shape[0]
    tm, tn = _tile(T, tm), _tile(D, tn)
    goff = OFF_GATES // tn
    return pl.pallas_call(
        _merge_kernel,
        out_shape=jax.ShapeDtypeStruct((T, D), jnp.bfloat16),
        grid=(D // tn, T // tm),
        in_specs=[pl.BlockSpec((tm, ATT_Q_WIDTH), lambda j, i: (i, 0)),
                  pl.BlockSpec((tm, GDN_WIDTH), lambda j, i: (i, 0)),
                  pl.BlockSpec((ATT_Q_WIDTH, tn), lambda j, i: (0, j)),
                  pl.BlockSpec((GDN_WIDTH, tn), lambda j, i: (0, j)),
                  pl.BlockSpec((tm, tn), lambda j, i: (i, goff + j)),
                  pl.BlockSpec((tm, tn), lambda j, i: (i, goff + D // tn + j))],
        out_specs=pl.BlockSpec((tm, tn), lambda j, i: (i, j)),
        compiler_params=_cparams(("parallel", "parallel")),
        name="gated_merge",
    )(ya, yg, wa, wg, P, P)


def _outproj_kernel(a_ref, w_ref, x_ref, o_ref):
    o_ref[...] = x_ref[...] + jnp.dot(a_ref[...], w_ref[...], preferred_element_type=jnp.float32)


def out_proj_residual(a, w, x, tm=1024, tn=512):
    T, D = x.shape
    K = a.shape[1]
    tm, tn = _tile(T, tm), _tile(D, tn)
    return pl.pallas_call(
        _outproj_kernel,
        out_shape=jax.ShapeDtypeStruct((T, D), jnp.float32),
        grid=(D // tn, T // tm),
        in_specs=[pl.BlockSpec((tm, K), lambda j, i: (i, 0)),
                  pl.BlockSpec((K, tn), lambda j, i: (0, j)),
                  pl.BlockSpec((tm, tn), lambda j, i: (i, j))],
        out_specs=pl.BlockSpec((tm, tn), lambda j, i: (i, j)),
        compiler_params=_cparams(("parallel", "parallel")),
        name="out_proj_residual",
    )(a, w, x)


def _ffn_norm_router_kernel(x_ref, w_ref, rhi_ref, rlo_ref, h_ref, lg_ref):
    x = x_ref[...]
    ms = jnp.mean(x * x, axis=-1, keepdims=True)
    h = x * lax.rsqrt(ms + EPS) * w_ref[...]
    h_ref[...] = h
    hh = h.astype(jnp.bfloat16)
    hl = (h - hh.astype(jnp.float32)).astype(jnp.bfloat16)
    f32 = jnp.float32
    lg_ref[...] = (jnp.dot(hh, rhi_ref[...], preferred_element_type=f32)
                   + jnp.dot(hh, rlo_ref[...], preferred_element_type=f32)
                   + jnp.dot(hl, rhi_ref[...], preferred_element_type=f32))


def ffn_norm_router(x1, norm_w, w_router, tm=256):
    T, D = x1.shape
    tm = _tile(T, tm)
    rhi = w_router.astype(jnp.bfloat16)
    rlo = (w_router - rhi.astype(jnp.float32)).astype(jnp.bfloat16)
    return pl.pallas_call(
        _ffn_norm_router_kernel,
        out_shape=[jax.ShapeDtypeStruct((T, D), jnp.float32), jax.ShapeDtypeStruct((T, LANES), jnp.float32)],
        grid=(T // tm,),
        in_specs=[pl.BlockSpec((tm, D), lambda i: (i, 0)), pl.BlockSpec((1, D), lambda i: (0, 0)),
                  pl.BlockSpec((D, LANES), lambda i: (0, 0)), pl.BlockSpec((D, LANES), lambda i: (0, 0))],
        out_specs=[pl.BlockSpec((tm, D), lambda i: (i, 0)), pl.BlockSpec((tm, LANES), lambda i: (i, 0))],
        compiler_params=_cparams(("parallel",)),
        name="ffn_norm_router",
    )(x1, norm_w.reshape(1, D), rhi, rlo)


def _routing_kernel(lg_ref, b_ref, eid_ref, wt_ref):
    lg = lg_ref[...] + b_ref[...]
    lane = lax.broadcasted_iota(jnp.int32, lg.shape, 1)
    ninf = -jnp.inf

    def first_argmax(vals, vmax):
        return jnp.min(jnp.where(vals == vmax, lane, LANES), axis=-1, keepdims=True)

    glog = jnp.where(lane < N_GROUPS, lg, ninf)
    gmax = jnp.max(glog, axis=-1, keepdims=True)
    p_grp = 1.0 / jnp.sum(jnp.exp(glog - gmax), axis=-1, keepdims=True)
    grp = first_argmax(glog, gmax)
    base = RT_EXP + grp * EXPERTS_PER_GROUP
    elog = jnp.where((lane >= base) & (lane < base + EXPERTS_PER_GROUP), lg, ninf)
    emax = jnp.max(elog, axis=-1, keepdims=True)
    idx1 = first_argmax(elog, emax)
    elog2 = jnp.where(lane == idx1, ninf, elog)
    emax2 = jnp.max(elog2, axis=-1, keepdims=True)
    idx2 = first_argmax(elog2, emax2)
    e2 = jnp.exp(emax2 - emax)
    w1 = p_grp / (1.0 + e2)
    w2 = p_grp * e2 / (1.0 + e2)
    eid_ref[...] = jnp.where(lane == 0, idx1 - RT_EXP, jnp.where(lane == 1, idx2 - RT_EXP, 0))
    wt_ref[...] = jnp.where(lane == 0, w1, jnp.where(lane == 1, w2, 0.0))


def routing(logits, bias_row, tm=512):
    T = logits.shape[0]
    tm = _tile(T, tm)
    spec = pl.BlockSpec((tm, LANES), lambda i: (i, 0))
    return pl.pallas_call(
        _routing_kernel,
        out_shape=[jax.ShapeDtypeStruct((T, LANES), jnp.int32), jax.ShapeDtypeStruct((T, LANES), jnp.float32)],
        grid=(T // tm,),
        in_specs=[spec, pl.BlockSpec((1, LANES), lambda i: (0, 0))],
        out_specs=[spec, spec],
        compiler_params=_cparams(("parallel",)),
        name="routing",
    )(logits, bias_row)


GATHER_UNROLL = 8


def _row_gather_copy(src_hbm, row, dst, r, sem):
    return pltpu.make_async_copy(src_hbm.at[pl.ds(row, 1), :], dst.at[pl.ds(r, 1), :], sem)


def _moe_ffn_kernel(texp_ref, nused_ref, rowtok_ref, h_hbm, wg_ref, wu_ref, wd_ref, o_ref,
                    xbuf, sem, *, tm):
    i = pl.program_id(0)
    nused = nused_ref[0]

    def start_gather(tile, slot):
        def body(r, carry):
            _row_gather_copy(h_hbm, rowtok_ref[tile * tm + r], xbuf.at[slot], r, sem.at[slot]).start()
            return carry
        lax.fori_loop(0, tm, body, 0, unroll=GATHER_UNROLL)

    def wait_gather(slot):
        pltpu.make_async_copy(h_hbm.at[pl.ds(0, tm), :], xbuf.at[slot], sem.at[slot]).wait()

    @pl.when((i == 0) & (nused > 0))
    def _():
        start_gather(0, 0)

    @pl.when(i + 1 < nused)
    def _():
        start_gather(i + 1, (i + 1) % 2)

    @pl.when(i < nused)
    def _():
        slot = i % 2
        wait_gather(slot)
        x = xbuf[slot].astype(jnp.bfloat16)
        g = jnp.dot(x, wg_ref[0], preferred_element_type=jnp.float32)
        u = jnp.dot(x, wu_ref[0], preferred_element_type=jnp.float32)
        hmid = (g * jax.nn.sigmoid(g) * u).astype(jnp.bfloat16)
        y = jnp.dot(hmid, wd_ref[0], preferred_element_type=jnp.float32)
        o_ref[...] = y

    @pl.when(i >= nused)
    def _():
        o_ref[...] = jnp.zeros(o_ref.shape, o_ref.dtype)


def moe_ffn(h2, tile_expert, n_used, row_token, wg, wu, wd, tm):
    T, D = h2.shape
    R = row_token.shape[0]
    FF = wg.shape[2]
    ntiles = R // tm
    grid_spec = pltpu.PrefetchScalarGridSpec(
        num_scalar_prefetch=3,
        grid=(ntiles,),
        in_specs=[pl.BlockSpec(memory_space=pl.ANY),
                  pl.BlockSpec((1, D, FF), lambda i, te, nu, rt: (te[i], 0, 0)),
                  pl.BlockSpec((1, D, FF), lambda i, te, nu, rt: (te[i], 0, 0)),
                  pl.BlockSpec((1, FF, D), lambda i, te, nu, rt: (te[i], 0, 0))],
        out_specs=pl.BlockSpec((tm, D), lambda i, te, nu, rt: (i, 0)),
        scratch_shapes=[pltpu.VMEM((2, tm, D), jnp.float32), pltpu.SemaphoreType.DMA((2,))],
    )
    return pl.pallas_call(
        functools.partial(_moe_ffn_kernel, tm=tm),
        out_shape=jax.ShapeDtypeStruct((R, D), jnp.float32),
        grid_spec=grid_spec,
        compiler_params=_cparams(("arbitrary",)),
        name="moe_ffn",
    )(tile_expert, n_used, row_token, h2, wg, wu, wd)


def _combine_kernel(dest_ref, x_ref, wt_ref, ys_hbm, o_ref, buf, sem, *, tm):
    i = pl.program_id(0)
    n = pl.num_programs(0)

    def start_gather(tile, slot):
        def body(r, carry):
            for kk in range(TOP_K_EXPERTS):
                row = dest_ref[(tile * tm + r) * TOP_K_EXPERTS + kk]
                _row_gather_copy(ys_hbm, row, buf.at[slot, kk], r, sem.at[slot]).start()
            return carry
        lax.fori_loop(0, tm, body, 0, unroll=GATHER_UNROLL)

    def wait_gather(slot):
        for kk in range(TOP_K_EXPERTS):
            pltpu.make_async_copy(ys_hbm.at[pl.ds(0, tm), :], buf.at[slot, kk], sem.at[slot]).wait()

    @pl.when(i == 0)
    def _():
        start_gather(0, 0)

    @pl.when(i + 1 < n)
    def _():
        start_gather(i + 1, (i + 1) % 2)

    slot = i % 2
    wait_gather(slot)
    wt = wt_ref[...]
    o_ref[...] = x_ref[...] + wt[:, 0:1] * buf[slot, 0] + wt[:, 1:2] * buf[slot, 1]


def moe_combine(x1, wt_lanes, ys, dest_row, tm=128):
    T, D = x1.shape
    tm = _tile(T, tm)
    grid_spec = pltpu.PrefetchScalarGridSpec(
        num_scalar_prefetch=1,
        grid=(T // tm,),
        in_specs=[pl.BlockSpec((tm, D), lambda i, d: (i, 0)), pl.BlockSpec((tm, LANES), lambda i, d: (i, 0)),
                  pl.BlockSpec(memory_space=pl.ANY)],
        out_specs=pl.BlockSpec((tm, D), lambda i, d: (i, 0)),
        scratch_shapes=[pltpu.VMEM((2, TOP_K_EXPERTS, tm, D), jnp.float32), pltpu.SemaphoreType.DMA((2,))],
    )
    return pl.pallas_call(
        functools.partial(_combine_kernel, tm=tm),
        out_shape=jax.ShapeDtypeStruct((T, D), jnp.float32),
        grid_spec=grid_spec,
        compiler_params=_cparams(("arbitrary",)),
        name="moe_combine",
    )(dest_row, x1, wt_lanes, ys)


def moe_dispatch_plan(eid, tm):
    T = eid.shape[0]
    A = T * TOP_K_EXPERTS
    e_flat = eid.reshape(A)
    onehot = (e_flat[:, None] == jnp.arange(N_EXPERTS, dtype=jnp.int32)[None, :]).astype(jnp.int32)
    csum = jnp.cumsum(onehot, axis=0)
    rank = jnp.sum((csum - onehot) * onehot, axis=1)
    counts = csum[-1]
    padded = ((counts + tm - 1) // tm) * tm
    pend = jnp.cumsum(padded)
    pstart = pend - padded
    dest_row = (pstart[e_flat] + rank).astype(jnp.int32)
    R = ((A + N_EXPERTS * (tm - 1)) + tm - 1) // tm * tm
    row_token = jnp.zeros((R,), jnp.int32).at[dest_row].set(jnp.arange(A, dtype=jnp.int32) // TOP_K_EXPERTS)
    tile_start = jnp.arange(R // tm, dtype=jnp.int32) * tm
    tile_expert = jnp.minimum(jnp.sum(tile_start[:, None] >= pend[None, :], axis=1), N_EXPERTS - 1).astype(jnp.int32)
    n_used = (pend[-1] // tm).astype(jnp.int32).reshape(1)
    return dest_row, row_token, tile_expert, n_used


def _split_w_in(w, D):
    sizes = (ATT_Q_WIDTH, ATT_KV_WIDTH, ATT_KV_WIDTH, IDX_Q_WIDTH, IDX_DIM, IDX_HEADS,
             GDN_WIDTH, GDN_WIDTH, GDN_WIDTH, GDN_WIDTH, GDN_HEADS, GDN_HEADS, D, D)
    offs = [0]
    for s in sizes:
        offs.append(offs[-1] + s)
    w = w.astype(jnp.bfloat16)
    seg = [w[:, offs[i]:offs[i + 1]] for i in range(len(sizes))]
    aq, ak, av, iq, ik, iw, gq, gk, gv, gz, ga, gb, gate_a, gate_g = seg
    w_big = jnp.concatenate([aq, ak, av, iq, gq, gk, gv, gz, gate_a, gate_g], axis=1)
    pad = jnp.zeros((w.shape[0], LANES - (IDX_DIM + IDX_HEADS + 2 * GDN_HEADS)), w.dtype)
    w_small = jnp.concatenate([ik, iw, ga, gb, pad], axis=1)
    return w_big, w_small


def kernel(x, positions, mix_norm_w, w_in, q_norm_w, k_norm_w, idx_k_norm_w, conv_w, a_log, dt_bias, gdn_norm_w, w_proj_attn, w_proj_gdn, w_out, ffn_norm_w, w_router_group, b_router_group, w_router_expert, b_router_expert, w_gate, w_up, w_down):
    B, S, D = x.shape
    T = B * S
    bf16 = jnp.bfloat16
    moe_tm = 256 if T * TOP_K_EXPERTS >= 256 * N_EXPERTS else 64
    xt = x.reshape(T, D)
    cos_t, sin_t = rope_tables(positions)
    for l in range(w_in.shape[0]):
        w_big, w_small = _split_w_in(w_in[l], D)
        h = rmsnorm(xt, mix_norm_w[l], bf16)
        NE, _, FF = w_gate[l].shape
        later_weights = [w_proj_attn[l], w_proj_gdn[l], w_out[l], w_gate[l].reshape(NE * D, FF),
                         w_up[l].reshape(NE * D, FF), w_down[l].reshape(NE * FF, D)]
        P, (wpa, wpg, wo, wg, wu, wd) = matmul_with_casts(h, w_big, bf16, later_weights, name="in_proj")
        wg, wu, wd = wg.reshape(NE, D, FF), wu.reshape(NE, D, FF), wd.reshape(NE, FF, D)
        Psm = matmul(h, w_small, jnp.float32, name="in_proj_small")
        q, k, qi, ki, wi = attn_prep(P, Psm, cos_t, sin_t, q_norm_w[l], k_norm_w[l], idx_k_norm_w[l])
        bias = indexer_mask(qi, ki, wi, B, S)
        y_attn = masked_attention(q, k, P, bias, B, S)
        gq, gk, gv, gb = gdn_prep(P, Psm, conv_w[l], a_log[l], dt_bias[l], B, S)
        y_gdn = gdn_delta_rule(gq, gk, gv, P, gb, gdn_norm_w[l], B, S)
        mixed = gated_merge(y_attn, y_gdn, wpa, wpg, P, D)
        x1 = out_proj_residual(mixed, wo, xt)
        w_router = jnp.zeros((D, LANES), jnp.float32)
        w_router = w_router.at[:, :N_GROUPS].set(w_router_group[l])
        w_router = w_router.at[:, RT_EXP:RT_EXP + N_EXPERTS].set(
            w_router_expert[l].transpose(1, 0, 2).reshape(D, N_EXPERTS))
        b_router = jnp.zeros((1, LANES), jnp.float32)
        b_router = b_router.at[0, :N_GROUPS].set(b_router_group[l])
        b_router = b_router.at[0, RT_EXP:RT_EXP + N_EXPERTS].set(b_router_expert[l].reshape(N_EXPERTS))
        h2, logits = ffn_norm_router(x1, ffn_norm_w[l], w_router)
        eid_l, wt_l = routing(logits, b_router)
        dest_row, row_token, tile_expert, n_used = moe_dispatch_plan(eid_l[:, :TOP_K_EXPERTS], moe_tm)
        ys = moe_ffn(h2, tile_expert, n_used, row_token, wg, wu, wd, moe_tm)
        xt = moe_combine(x1, wt_l, ys, dest_row)
    return xt.reshape(B, S, D)
```

```python
import functools
import math

import jax
import jax.numpy as jnp
from jax import lax
from jax.experimental import pallas as pl
from jax.experimental.pallas import tpu as pltpu

ATT_HEADS = 16
ATT_KV_HEADS = 4
HEAD_DIM = 128
IDX_HEADS = 16
IDX_DIM = 64
TOPK_MAX = 256
ROPE_THETA = 10000.0
GDN_HEADS = 16
GDN_DK = 128
GDN_DV = 128
CONV_WIDTH = 4
CHUNK = 64
N_GROUPS = 4
EXPERTS_PER_GROUP = 8
N_EXPERTS = N_GROUPS * EXPERTS_PER_GROUP
TOP_K_EXPERTS = 2
EPS = 1e-6

ATT_Q_WIDTH = ATT_HEADS * HEAD_DIM
ATT_KV_WIDTH = ATT_KV_HEADS * HEAD_DIM
IDX_Q_WIDTH = IDX_HEADS * IDX_DIM
GDN_WIDTH = GDN_HEADS * GDN_DK

LANES = 128
BF16_SUBLANES = 16
VMEM_LIMIT = 56 * 1024 * 1024
NEG_BIG = -1e30
LOG2E = math.log2(math.e)
SCORE_MASKED = 3.0e38
BISECT_MAX_STEPS = 192
BISECT_UNROLL = 4
COUNT_ROWS = 128

OFF_AQ = 0
OFF_AK = OFF_AQ + ATT_Q_WIDTH
OFF_AV = OFF_AK + ATT_KV_WIDTH
OFF_IQ = OFF_AV + ATT_KV_WIDTH
OFF_GQ = OFF_IQ + IDX_Q_WIDTH
OFF_GK = OFF_GQ + GDN_WIDTH
OFF_GV = OFF_GK + GDN_WIDTH
OFF_GZ = OFF_GV + GDN_WIDTH
OFF_GATES = OFF_GZ + GDN_WIDTH
SM_IK = 0
SM_IW = SM_IK + IDX_DIM
SM_GA = SM_IW + IDX_HEADS
SM_GB = SM_GA + GDN_HEADS
RT_EXP = 8


def _cparams(sem):
    return pltpu.CompilerParams(dimension_semantics=sem, vmem_limit_bytes=VMEM_LIMIT)


def _tile(n, pref):
    t = min(n, pref)
    assert n % t == 0, (n, pref)
    return t


def _rmsnorm_kernel(x_ref, w_ref, o_ref):
    x = x_ref[...]
    ms = jnp.mean(x * x, axis=-1, keepdims=True)
    o_ref[...] = (x * lax.rsqrt(ms + EPS) * w_ref[...]).astype(o_ref.dtype)


def rmsnorm(x, w, out_dtype, tm=256):
    T, D = x.shape
    tm = _tile(T, tm)
    return pl.pallas_call(
        _rmsnorm_kernel,
        out_shape=jax.ShapeDtypeStruct((T, D), out_dtype),
        grid=(T // tm,),
        in_specs=[pl.BlockSpec((tm, D), lambda i: (i, 0)), pl.BlockSpec((1, D), lambda i: (0, 0))],
        out_specs=pl.BlockSpec((tm, D), lambda i: (i, 0)),
        compiler_params=_cparams(("parallel",)),
        name="rmsnorm",
    )(x, w.reshape(1, D))


def _matmul_kernel(a_ref, b_ref, o_ref):
    o_ref[...] = jnp.dot(a_ref[...], b_ref[...], preferred_element_type=jnp.float32).astype(o_ref.dtype)


def matmul(a, b, out_dtype, tm=1024, tn=1024, name="matmul"):
    M, K = a.shape
    _, N = b.shape
    tm, tn = _tile(M, tm), _tile(N, tn)
    return pl.pallas_call(
        _matmul_kernel,
        out_shape=jax.ShapeDtypeStruct((M, N), out_dtype),
        grid=(N // tn, M // tm),
        in_specs=[pl.BlockSpec((tm, K), lambda j, i: (i, 0)), pl.BlockSpec((K, tn), lambda j, i: (0, j))],
        out_specs=pl.BlockSpec((tm, tn), lambda j, i: (i, j)),
        compiler_params=_cparams(("parallel", "parallel")),
        name=name,
    )(a, b)


def _matmul_cast_kernel(a_ref, b_ref, *refs):
    ncast = (len(refs) - 1) // 2
    o_ref = refs[ncast]
    o_ref[...] = jnp.dot(a_ref[...], b_ref[...], preferred_element_type=jnp.float32).astype(o_ref.dtype)
    for x_ref, y_ref in zip(refs[:ncast], refs[ncast + 1:]):
        y_ref[...] = x_ref[...].astype(y_ref.dtype)


def _cast_block_rows(rows, nsteps):
    for rb in range(BF16_SUBLANES, rows + 1, BF16_SUBLANES):
        if rows % rb == 0 and rows // rb <= nsteps:
            return rb
    raise ValueError((rows, nsteps))


def matmul_with_casts(a, b, out_dtype, to_cast, tm=1024, tn=1024, name="matmul_casts"):
    M, K = a.shape
    _, N = b.shape
    tm, tn = _tile(M, tm), _tile(N, tn)
    ni = M // tm
    nsteps = (N // tn) * ni
    cast_in, cast_out, cast_shapes = [], [], []
    for x in to_cast:
        rows, cols = x.shape
        rb = _cast_block_rows(rows, nsteps)
        last = rows // rb - 1
        imap = functools.partial(lambda j, i, last: (jnp.minimum(j * ni + i, last), 0), last=last)
        cast_in.append(pl.BlockSpec((rb, cols), imap))
        cast_out.append(pl.BlockSpec((rb, cols), imap))
        cast_shapes.append(jax.ShapeDtypeStruct((rows, cols), jnp.bfloat16))
    outs = pl.pallas_call(
        _matmul_cast_kernel,
        out_shape=[jax.ShapeDtypeStruct((M, N), out_dtype)] + cast_shapes,
        grid=(N // tn, ni),
        in_specs=[pl.BlockSpec((tm, K), lambda j, i: (i, 0)), pl.BlockSpec((K, tn), lambda j, i: (0, j))] + cast_in,
        out_specs=[pl.BlockSpec((tm, tn), lambda j, i: (i, j))] + cast_out,
        compiler_params=_cparams(("arbitrary", "arbitrary")),
        name=name,
    )(a, b, *to_cast)
    return outs[0], outs[1:]


def _rope_table_kernel(pos_ref, inv_ref, sgn_ref, cos_ref, sin_ref):
    ang = pos_ref[...] * inv_ref[...]
    cos_ref[...] = jnp.cos(ang)
    sin_ref[...] = jnp.sin(ang) * sgn_ref[...]


def rope_tables(positions):
    T = positions.size
    pos = positions.reshape(T, 1).astype(jnp.float32)

    def inv(d):
        return ROPE_THETA ** (-jnp.arange(0, d, 2, dtype=jnp.float32) / d)

    i128, i64 = inv(HEAD_DIM), inv(IDX_DIM)
    inv_row = jnp.concatenate([i128, i128, i64, i64, i64, i64]).reshape(1, 2 * LANES)
    s128 = jnp.concatenate([-jnp.ones(HEAD_DIM // 2), jnp.ones(HEAD_DIM // 2)])
    s64 = jnp.concatenate([-jnp.ones(IDX_DIM // 2), jnp.ones(IDX_DIM // 2)])
    sgn_row = jnp.concatenate([s128, s64, s64]).astype(jnp.float32).reshape(1, 2 * LANES)
    tm = _tile(T, 512)
    spec = pl.BlockSpec((tm, 2 * LANES), lambda i: (i, 0))
    row = pl.BlockSpec((1, 2 * LANES), lambda i: (0, 0))
    return pl.pallas_call(
        _rope_table_kernel,
        out_shape=[jax.ShapeDtypeStruct((T, 2 * LANES), jnp.float32)] * 2,
        grid=(T // tm,),
        in_specs=[pl.BlockSpec((tm, 1), lambda i: (i, 0)), row, row],
        out_specs=[spec, spec],
        compiler_params=_cparams(("parallel",)),
        name="rope_tables",
    )(pos, inv_row, sgn_row)


def _rope128(x, cos, sin_signed):
    return x * cos + pltpu.roll(x, HEAD_DIM // 2, 1) * sin_signed


def _rope64x2(x, cos, sin_signed, lane):
    half = IDX_DIM // 2
    first = (lane % IDX_DIM) < half
    partner = jnp.where(first, pltpu.roll(x, LANES - half, 1), pltpu.roll(x, half, 1))
    return x * cos + partner * sin_signed


def _attn_prep_kernel(aq_ref, ak_ref, iq_ref, sm_ref, cos_ref, sin_ref, qw_ref, kw_ref, ikw_ref,
                      q_ref, k_ref, qi_ref, ki_ref, wi_ref):
    cos_a, sin_a = cos_ref[:, :LANES], sin_ref[:, :LANES]
    cos_i, sin_i = cos_ref[:, LANES:], sin_ref[:, LANES:]
    tm = cos_a.shape[0]
    lane = lax.broadcasted_iota(jnp.int32, (tm, LANES), 1)

    def head_norm(xh, w):
        ms = jnp.mean(xh * xh, axis=-1, keepdims=True)
        return xh * lax.rsqrt(ms + EPS) * w

    for h in range(ATT_HEADS):
        sl = slice(h * HEAD_DIM, (h + 1) * HEAD_DIM)
        xh = head_norm(aq_ref[:, sl].astype(jnp.float32), qw_ref[...])
        q_ref[:, sl] = (_rope128(xh, cos_a, sin_a) * (LOG2E * HEAD_DIM ** -0.5)).astype(q_ref.dtype)
    for h in range(ATT_KV_HEADS):
        sl = slice(h * HEAD_DIM, (h + 1) * HEAD_DIM)
        xh = head_norm(ak_ref[:, sl].astype(jnp.float32), kw_ref[...])
        k_ref[:, sl] = _rope128(xh, cos_a, sin_a).astype(k_ref.dtype)
    for p in range(IDX_Q_WIDTH // LANES):
        sl = slice(p * LANES, (p + 1) * LANES)
        xp = iq_ref[:, sl].astype(jnp.float32)
        qi_ref[:, sl] = (_rope64x2(xp, cos_i, sin_i, lane) * (IDX_DIM ** -0.5)).astype(qi_ref.dtype)
    sm = sm_ref[...]
    in_k = lane < IDX_DIM
    xk = jnp.where(in_k, sm, 0.0)
    ms = jnp.sum(xk * xk, axis=-1, keepdims=True) * (1.0 / IDX_DIM)
    kn = xk * lax.rsqrt(ms + EPS) * ikw_ref[...]
    kr = jnp.where(in_k, _rope64x2(kn, cos_i, sin_i, lane), 0.0)
    ki_ref[:, :LANES] = kr.astype(ki_ref.dtype)
    ki_ref[:, LANES:] = pltpu.roll(kr, IDX_DIM, 1).astype(ki_ref.dtype)
    wi_ref[...] = sm * (IDX_HEADS ** -0.5)


def attn_prep(P, Psm, cos_t, sin_t, q_norm_w, k_norm_w, idx_k_norm_w, tm=256):
    T = P.shape[0]
    tm = _tile(T, tm)
    ikw = jnp.concatenate([idx_k_norm_w, jnp.zeros((LANES - IDX_DIM,), jnp.float32)]).reshape(1, LANES)
    row = lambda w: pl.BlockSpec((1, w), lambda i: (0, 0))
    return pl.pallas_call(
        _attn_prep_kernel,
        out_shape=[jax.ShapeDtypeStruct((T, ATT_Q_WIDTH), jnp.bfloat16),
                   jax.ShapeDtypeStruct((T, ATT_KV_WIDTH), jnp.bfloat16),
                   jax.ShapeDtypeStruct((T, IDX_Q_WIDTH), jnp.bfloat16),
                   jax.ShapeDtypeStruct((T, 2 * LANES), jnp.bfloat16),
                   jax.ShapeDtypeStruct((T, LANES), jnp.float32)],
        grid=(T // tm,),
        in_specs=[pl.BlockSpec((tm, ATT_Q_WIDTH), lambda i: (i, OFF_AQ // ATT_Q_WIDTH)),
                  pl.BlockSpec((tm, ATT_KV_WIDTH), lambda i: (i, OFF_AK // ATT_KV_WIDTH)),
                  pl.BlockSpec((tm, IDX_Q_WIDTH), lambda i: (i, OFF_IQ // IDX_Q_WIDTH)),
                  pl.BlockSpec((tm, LANES), lambda i: (i, 0)),
                  pl.BlockSpec((tm, 2 * LANES), lambda i: (i, 0)),
                  pl.BlockSpec((tm, 2 * LANES), lambda i: (i, 0)),
                  row(LANES), row(LANES), row(LANES)],
        out_specs=[pl.BlockSpec((tm, ATT_Q_WIDTH), lambda i: (i, 0)),
                   pl.BlockSpec((tm, ATT_KV_WIDTH), lambda i: (i, 0)),
                   pl.BlockSpec((tm, IDX_Q_WIDTH), lambda i: (i, 0)),
                   pl.BlockSpec((tm, 2 * LANES), lambda i: (i, 0)),
                   pl.BlockSpec((tm, LANES), lambda i: (i, 0))],
        compiler_params=_cparams(("parallel",)),
        name="attn_prep",
    )(P, P, P, Psm, cos_t, sin_t, q_norm_w.reshape(1, LANES), k_norm_w.reshape(1, LANES), ikw)


def _lane_fold(x, op):
    s = x[:, :LANES]
    for j in range(1, x.shape[1] // LANES):
        s = op(s, x[:, j * LANES:(j + 1) * LANES])
    return s


def _indexer_kernel(qi_ref, ki_ref, wi_ref, bias_ref, key_ref, *, n_sel, tq, tk, nchunks):
    q0 = pl.program_id(1) * tq
    nck = (q0 + tq + tk - 1) // tk
    wi = wi_ref[...]
    rowpos = q0 + lax.broadcasted_iota(jnp.int32, (tq, tk), 0)
    colpos0 = lax.broadcasted_iota(jnp.int32, (tq, tk), 1)

    def score_chunk(c, carry):
        mx, mn = carry
        ks = ki_ref[pl.ds(pl.multiple_of(c * tk, tk), tk), :]
        acc = jnp.zeros((tq, tk), jnp.float32)
        for h in range(IDX_HEADS):
            qp = qi_ref[:, (h // 2) * LANES:(h // 2 + 1) * LANES]
            kh = ks[:, (h % 2) * LANES:(h % 2 + 1) * LANES]
            d = lax.dot_general(qp, kh, (((1,), (1,)), ((), ())), preferred_element_type=jnp.float32)
            acc = acc + wi[:, SM_IW + h:SM_IW + h + 1] * jnp.maximum(d, 0.0)
        causal = colpos0 + c * tk <= rowpos
        key_ref[c] = jnp.where(causal, acc, -SCORE_MASKED)
        mx = jnp.maximum(mx, _lane_fold(jnp.where(causal, acc, -SCORE_MASKED), jnp.maximum))
        mn = jnp.minimum(mn, _lane_fold(jnp.where(causal, acc, SCORE_MASKED), jnp.minimum))
        return mx, mn

    mx, mn = lax.fori_loop(0, nck, score_chunk, (jnp.full((tq, LANES), -SCORE_MASKED, jnp.float32),
                                                 jnp.full((tq, LANES), SCORE_MASKED, jnp.float32)))

    ones = jnp.ones((LANES, LANES), jnp.bfloat16)

    def count_ge(probe):
        accs = []
        for r0 in range(0, tq, COUNT_ROWS):
            rows = slice(r0, min(r0 + COUNT_ROWS, tq))

            def body(c, acc, rows=rows):
                for j in range(tk // LANES):
                    acc = acc + jnp.where(key_ref[c, rows, j * LANES:(j + 1) * LANES] >= probe[rows], 1.0, 0.0)
                return acc

            accs.append(lax.fori_loop(0, nck, body, jnp.zeros((rows.stop - rows.start, LANES), jnp.float32)))
        acc = accs[0] if len(accs) == 1 else jnp.concatenate(accs, axis=0)
        return jnp.dot(acc.astype(jnp.bfloat16), ones, preferred_element_type=jnp.float32)

    lo = jnp.broadcast_to(jnp.min(mn, axis=1, keepdims=True), (tq, LANES))
    hi = jnp.broadcast_to(jnp.max(mx, axis=1, keepdims=True), (tq, LANES))
    ncausal = q0 + lax.broadcasted_iota(jnp.int32, (tq, LANES), 0) + 1
    hi = jnp.where(ncausal <= n_sel, lo, hi)

    def unfinished(carry):
        return (carry[0] < BISECT_MAX_STEPS) & (carry[3] > 0)

    def bisect(carry):
        it, lo, hi, _ = carry
        for _ in range(BISECT_UNROLL):
            mid = 0.5 * lo + 0.5 * hi
            cnt = count_ge(mid)
            ok = cnt >= n_sel
            open_row = (mid > lo) & (mid < hi)
            lo = jnp.where(ok, mid, lo)
            hi = jnp.where(cnt == n_sel, mid, jnp.where(ok, hi, mid))
            open_row = open_row & (hi > lo)
        return it + BISECT_UNROLL, lo, hi, jnp.sum(jnp.where(open_row, 1, 0))

    _, thr, _, _ = lax.while_loop(unfinished, bisect, (jnp.int32(0), lo, hi, jnp.int32(1)))

    def write_chunk(c, carry):
        for j in range(tk // LANES):
            cols = slice(j * LANES, (j + 1) * LANES)
            bias_ref[c, :, cols] = jnp.where(key_ref[c, :, cols] >= thr, 0.0, NEG_BIG).astype(bias_ref.dtype)
        return carry

    def write_masked(c, carry):
        bias_ref[c] = jnp.full((tq, tk), NEG_BIG, bias_ref.dtype)
        return carry

    lax.fori_loop(0, nck, write_chunk, 0)
    lax.fori_loop(nck, nchunks, write_masked, 0)


def indexer_mask(qi, ki, wi, B, S, tq=512, tk=512):
    tq, tk = _tile(S, tq), _tile(S, tk)
    n_sel = min(TOPK_MAX, S // 4)
    nq, nchunks = S // tq, S // tk
    return pl.pallas_call(
        functools.partial(_indexer_kernel, n_sel=n_sel, tq=tq, tk=tk, nchunks=nchunks),
        out_shape=jax.ShapeDtypeStruct((B, nq, nchunks, tq, tk), jnp.bfloat16),
        grid=(B, nq),
        in_specs=[pl.BlockSpec((tq, IDX_Q_WIDTH), lambda b, i: (b * nq + i, 0)),
                  pl.BlockSpec((S, 2 * LANES), lambda b, i: (b, 0)),
                  pl.BlockSpec((tq, LANES), lambda b, i: (b * nq + i, 0))],
        out_specs=pl.BlockSpec((None, None, nchunks, tq, tk), lambda b, i: (b, i, 0, 0, 0)),
        scratch_shapes=[pltpu.VMEM((nchunks, tq, tk), jnp.float32)],
        compiler_params=_cparams(("parallel", "parallel")),
        name="indexer_mask",
    )(qi, ki, wi)


def _attn_kernel(q_ref, k_ref, v_ref, bias_ref, o_ref, m_ref, acc_ref, *, tq, tk):
    i, j = pl.program_id(1), pl.program_id(2)
    grp = ATT_HEADS // ATT_KV_HEADS

    @pl.when(j == 0)
    def _():
        m_ref[...] = jnp.full(m_ref.shape, -1e38, jnp.float32)
        acc_ref[...] = jnp.zeros(acc_ref.shape, jnp.float32)

    @pl.when(j * tk <= i * tq + tq - 1)
    def _():
        bias = bias_ref[...].astype(jnp.float32)
        ones = jnp.ones((tk, LANES), v_ref.dtype)
        v1 = [jnp.concatenate([v_ref[:, g * HEAD_DIM:(g + 1) * HEAD_DIM], ones], axis=1) for g in range(ATT_KV_HEADS)]
        s, m_new, alpha, p = {}, {}, {}, {}

        def scores(h):
            g = h // grp
            qh = q_ref[:, h * HEAD_DIM:(h + 1) * HEAD_DIM]
            kh = k_ref[:, g * HEAD_DIM:(g + 1) * HEAD_DIM]
            s[h] = lax.dot_general(qh, kh, (((1,), (1,)), ((), ())), preferred_element_type=jnp.float32) + bias
            m_prev = m_ref[h]
            m_new[h] = jnp.maximum(m_prev, jnp.max(s[h], axis=1, keepdims=True))
            alpha[h] = jnp.exp2(m_prev - m_new[h])
            m_ref[h] = m_new[h]

        def probs(h):
            p[h] = jnp.exp2(s.pop(h) - m_new.pop(h)[:, :1]).astype(v_ref.dtype)

        def values(h):
            a2 = jnp.concatenate([alpha[h], alpha.pop(h)], axis=1)
            acc_ref[h] = a2 * acc_ref[h] + jnp.dot(p.pop(h), v1[h // grp], preferred_element_type=jnp.float32)

        for t in range(ATT_HEADS + 2):
            if t < ATT_HEADS:
                scores(t)
            if 0 <= t - 1 < ATT_HEADS:
                probs(t - 1)
            if 0 <= t - 2 < ATT_HEADS:
                values(t - 2)

    @pl.when(j == pl.num_programs(2) - 1)
    def _():
        for h in range(ATT_HEADS):
            acc = acc_ref[h]
            o_ref[:, h * HEAD_DIM:(h + 1) * HEAD_DIM] = (acc[:, :HEAD_DIM] / acc[:, HEAD_DIM:]).astype(o_ref.dtype)


def masked_attention(q, k, P, bias, B, S):
    _, nq, nkv, tq, tk = bias.shape

    def kv_idx(i, j):
        return jnp.minimum(j, (i * tq + tq - 1) // tk)

    return pl.pallas_call(
        functools.partial(_attn_kernel, tq=tq, tk=tk),
        out_shape=jax.ShapeDtypeStruct((B * S, ATT_Q_WIDTH), jnp.bfloat16),
        grid=(B, nq, nkv),
        in_specs=[pl.BlockSpec((tq, ATT_Q_WIDTH), lambda b, i, j: (b * nq + i, 0)),
                  pl.BlockSpec((tk, ATT_KV_WIDTH), lambda b, i, j: (b * nkv + kv_idx(i, j), 0)),
                  pl.BlockSpec((tk, ATT_KV_WIDTH), lambda b, i, j: (b * nkv + kv_idx(i, j), OFF_AV // ATT_KV_WIDTH)),
                  pl.BlockSpec((None, None, None, tq, tk), lambda b, i, j: (b, i, kv_idx(i, j), 0, 0))],
        out_specs=pl.BlockSpec((tq, ATT_Q_WIDTH), lambda b, i, j: (b * nq + i, 0)),
        scratch_shapes=[pltpu.VMEM((ATT_HEADS, tq, LANES), jnp.float32),
                        pltpu.VMEM((ATT_HEADS, tq, HEAD_DIM + LANES), jnp.float32)],
        compiler_params=_cparams(("parallel", "parallel", "arbitrary")),
        name="masked_attention",
    )(q, k, P, bias)


GDN_HALO = BF16_SUBLANES


def _gdn_kernel(xq_ref, xk_ref, xv_ref, z_ref, sm_ref, cw_ref, alog_ref, dtb_ref, nw_ref, o_ref, state_ref, prev_ref):
    C = CHUNK
    f32, bf16 = jnp.float32, jnp.bfloat16
    heads = range(GDN_HEADS)

    @pl.when(pl.program_id(1) == 0)
    def _():
        state_ref[...] = jnp.zeros(state_ref.shape, f32)
        prev_ref[...] = jnp.zeros(prev_ref.shape, prev_ref.dtype)

    srow = lax.broadcasted_iota(jnp.int32, (CONV_WIDTH * C, GDN_HALO + C), 0)
    scol = lax.broadcasted_iota(jnp.int32, (CONV_WIDTH * C, GDN_HALO + C), 1)
    select = (scol == (srow & (C - 1)) + (srow >> (C.bit_length() - 1)) + (GDN_HALO - (CONV_WIDTH - 1))).astype(bf16)

    def conv_silu(x_ref, part):
        x = x_ref[...]
        xp = jnp.concatenate([prev_ref[part], x], axis=0)
        prev_ref[part] = x[C - GDN_HALO:]
        taps = jnp.dot(select, xp, preferred_element_type=f32)
        w = cw_ref[:, part * GDN_WIDTH:(part + 1) * GDN_WIDTH]
        y = None
        for jj in range(CONV_WIDTH):
            term = w[jj:jj + 1, :] * taps[jj * C:(jj + 1) * C, :]
            y = term if y is None else y + term
        return y * jax.nn.sigmoid(y)

    def l2(yh):
        return yh * lax.rsqrt(jnp.sum(yh * yh, axis=-1, keepdims=True) + EPS)

    yq, yk, yv = conv_silu(xq_ref, 0), conv_silu(xk_ref, 1), conv_silu(xv_ref, 2)
    sm = sm_ref[...]
    a_in = sm + dtb_ref[...]
    softplus = jnp.maximum(a_in, 0.0) + jnp.log(1.0 + jnp.exp(-jnp.abs(a_in)))
    g_all = -jnp.exp(alog_ref[...]) * softplus
    beta_all = jax.nn.sigmoid(sm)

    row = lax.broadcasted_iota(jnp.int32, (C, C), 0)
    col = lax.broadcasted_iota(jnp.int32, (C, C), 1)
    tril, strict = row >= col, row > col
    eye = (row == col).astype(f32)
    hi = lax.Precision.HIGHEST
    gc_col = jnp.dot(tril.astype(f32), g_all, precision=hi, preferred_element_type=f32)
    gc_row = jnp.dot(g_all.T, (row <= col).astype(f32), precision=hi, preferred_element_type=f32)
    nt = (((1,), (1,)), ((), ()))
    dot = functools.partial(jnp.dot, preferred_element_type=f32)
    kq, a, intra, rhs, qd, kd, eglast = [], [], [], [], [], [], []
    for j in heads:
        sl = slice(j * GDN_DK, (j + 1) * GDN_DK)
        gc = gc_col[:, SM_GA + j:SM_GA + j + 1]
        glast = gc[C - 1:C, :]
        eg = jnp.exp(gc)
        beta = beta_all[:, SM_GB + j:SM_GB + j + 1]
        q = (l2(yq[:, sl]) * (GDN_DK ** -0.5)).astype(bf16)
        k = l2(yk[:, sl]).astype(bf16)
        v = yv[:, sl].astype(bf16)
        kf = k.astype(f32)
        kq.append(jnp.concatenate([k, q], axis=0))
        rhs.append(jnp.concatenate([(v.astype(f32) * beta).astype(bf16), (kf * (beta * eg)).astype(bf16)], axis=1))
        qd.append((q.astype(f32) * eg).astype(bf16))
        kd.append((kf * jnp.exp(glast - gc)).astype(bf16))
        eglast.append(jnp.exp(glast))
    skq = [lax.dot_general(kq[j], kq[j][:C], nt, preferred_element_type=f32) for j in heads]
    for j in heads:
        gc = gc_col[:, SM_GA + j:SM_GA + j + 1]
        decay = jnp.exp(jnp.where(tril, gc - gc_row[SM_GA + j:SM_GA + j + 1, :], NEG_BIG))
        a.append(jnp.where(strict, skq[j][:C] * beta_all[:, SM_GB + j:SM_GB + j + 1] * decay, 0.0))
        intra.append((skq[j][C:] * decay).astype(bf16))
    ab = [a[j].astype(bf16) for j in heads]
    xb = [dot(ab[j], ab[j]).astype(bf16) for j in heads]
    tinv = [eye - a[j] for j in heads]
    for it in range(5):
        last = it == 4
        lhs = [tinv[j].astype(bf16) if last else jnp.concatenate([tinv[j].astype(bf16), xb[j]], axis=0) for j in heads]
        prod = [dot(lhs[j], xb[j]) for j in heads]
        tinv = [tinv[j] + prod[j][:C] for j in heads]
        if not last:
            xb = [prod[j][C:].astype(bf16) for j in heads]
    uw = [dot(tinv[j].astype(bf16), rhs[j]) for j in heads]
    sb = [state_ref[j].astype(bf16) for j in heads]
    ws = [dot(jnp.concatenate([uw[j][:, GDN_DV:].astype(bf16), qd[j]], axis=0), sb[j]) for j in heads]
    vb = [(uw[j][:, :GDN_DV] - ws[j][:C]).astype(bf16) for j in heads]
    o = [ws[j][C:] + dot(intra[j], vb[j]) for j in heads]
    upd = [lax.dot_general(kd[j], vb[j], (((0,), (0,)), ((), ())), preferred_element_type=f32) for j in heads]
    for j in heads:
        sl = slice(j * GDN_DK, (j + 1) * GDN_DK)
        state_ref[j] = state_ref[j] * eglast[j] + upd[j]
        ms = jnp.mean(o[j] * o[j], axis=-1, keepdims=True)
        z = z_ref[:, sl].astype(f32)
        o_ref[:, sl] = (o[j] * lax.rsqrt(ms + EPS) * nw_ref[...] * (z * jax.nn.sigmoid(z))).astype(o_ref.dtype)


def gated_deltanet(P, Psm, conv_w, a_log, dt_bias, norm_w, B, S):
    T = B * S
    C, W = CHUNK, GDN_WIDTH
    n = S // C

    def pad_lane(v, off):
        return jnp.zeros((1, LANES), jnp.float32).at[0, off:off + v.shape[0]].set(v)

    blk = lambda off: pl.BlockSpec((C, W), lambda b, c: (b * n + c, off // W))
    row = lambda w: pl.BlockSpec((1, w), lambda b, c: (0, 0))
    return pl.pallas_call(
        _gdn_kernel,
        out_shape=jax.ShapeDtypeStruct((T, W), jnp.bfloat16),
        grid=(B, n),
        in_specs=[blk(OFF_GQ), blk(OFF_GK), blk(OFF_GV), blk(OFF_GZ),
                  pl.BlockSpec((C, LANES), lambda b, c: (b * n + c, 0)),
                  pl.BlockSpec((CONV_WIDTH, 3 * W), lambda b, c: (0, 0)),
                  row(LANES), row(LANES), row(GDN_DV)],
        out_specs=blk(0),
        scratch_shapes=[pltpu.VMEM((GDN_HEADS, GDN_DK, GDN_DV), jnp.float32),
                        pltpu.VMEM((3, GDN_HALO, W), jnp.bfloat16)],
        compiler_params=_cparams(("parallel", "arbitrary")),
        name="gated_deltanet",
    )(P, P, P, P, Psm, conv_w, pad_lane(a_log, SM_GA), pad_lane(dt_bias, SM_GA), norm_w.reshape(1, GDN_DV))


def _merge_kernel(ya_ref, yg_ref, wa_ref, wg_ref, ga_ref, gg_ref, o_ref):
    pa = jnp.dot(ya_ref[...], wa_ref[...], preferred_element_type=jnp.float32)
    pg = jnp.dot(yg_ref[...], wg_ref[...], preferred_element_type=jnp.float32)
    ga = jax.nn.sigmoid(ga_ref[...].astype(jnp.float32))
    gg = jax.nn.sigmoid(gg_ref[...].astype(jnp.float32))
    o_ref[...] = (ga * pa + gg * pg).astype(o_ref.dtype)


def gated_merge(ya, yg, wa, wg, P, D, tm=1024, tn=512):
    T = ya.shape[0]
    tm, tn = _tile(T, tm), _tile(D, tn)
    goff = OFF_GATES // tn
    return pl.pallas_call(
        _merge_kernel,
        out_shape=jax.ShapeDtypeStruct((T, D), jnp.bfloat16),
        grid=(D // tn, T // tm),
        in_specs=[pl.BlockSpec((tm, ATT_Q_WIDTH), lambda j, i: (i, 0)),
                  pl.BlockSpec((tm, GDN_WIDTH), lambda j, i: (i, 0)),
                  pl.BlockSpec((ATT_Q_WIDTH, tn), lambda j, i: (0, j)),
                  pl.BlockSpec((GDN_WIDTH, tn), lambda j, i: (0, j)),
                  pl.BlockSpec((tm, tn), lambda j, i: (i, goff + j)),
                  pl.BlockSpec((tm, tn), lambda j, i: (i, goff + D // tn + j))],
        out_specs=pl.BlockSpec((tm, tn), lambda j, i: (i, j)),
        compiler_params=_cparams(("parallel", "parallel")),
        name="gated_merge",
    )(ya, yg, wa, wg, P, P)


def _outproj_kernel(a_ref, w_ref, x_ref, o_ref):
    o_ref[...] = x_ref[...] + jnp.dot(a_ref[...], w_ref[...], preferred_element_type=jnp.float32)


def out_proj_residual(a, w, x, tm=1024, tn=512):
    T, D = x.shape
    K = a.shape[1]
    tm, tn = _tile(T, tm), _tile(D, tn)
    return pl.pallas_call(
        _outproj_kernel,
        out_shape=jax.ShapeDtypeStruct((T, D), jnp.float32),
        grid=(D // tn, T // tm),
        in_specs=[pl.BlockSpec((tm, K), lambda j, i: (i, 0)),
                  pl.BlockSpec((K, tn), lambda j, i: (0, j)),
                  pl.BlockSpec((tm, tn), lambda j, i: (i, j))],
        out_specs=pl.BlockSpec((tm, tn), lambda j, i: (i, j)),
        compiler_params=_cparams(("parallel", "parallel")),
        name="out_proj_residual",
    )(a, w, x)


def _ffn_norm_router_kernel(x_ref, w_ref, rhi_ref, rlo_ref, h_ref, lg_ref):
    x = x_ref[...]
    ms = jnp.mean(x * x, axis=-1, keepdims=True)
    h = x * lax.rsqrt(ms + EPS) * w_ref[...]
    h_ref[...] = h
    hh = h.astype(jnp.bfloat16)
    hl = (h - hh.astype(jnp.float32)).astype(jnp.bfloat16)
    f32 = jnp.float32
    lg_ref[...] = (jnp.dot(hh, rhi_ref[...], preferred_element_type=f32)
                   + jnp.dot(hh, rlo_ref[...], preferred_element_type=f32)
                   + jnp.dot(hl, rhi_ref[...], preferred_element_type=f32))


def ffn_norm_router(x1, norm_w, w_router, tm=256):
    T, D = x1.shape
    tm = _tile(T, tm)
    rhi = w_router.astype(jnp.bfloat16)
    rlo = (w_router - rhi.astype(jnp.float32)).astype(jnp.bfloat16)
    return pl.pallas_call(
        _ffn_norm_router_kernel,
        out_shape=[jax.ShapeDtypeStruct((T, D), jnp.float32), jax.ShapeDtypeStruct((T, LANES), jnp.float32)],
        grid=(T // tm,),
        in_specs=[pl.BlockSpec((tm, D), lambda i: (i, 0)), pl.BlockSpec((1, D), lambda i: (0, 0)),
                  pl.BlockSpec((D, LANES), lambda i: (0, 0)), pl.BlockSpec((D, LANES), lambda i: (0, 0))],
        out_specs=[pl.BlockSpec((tm, D), lambda i: (i, 0)), pl.BlockSpec((tm, LANES), lambda i: (i, 0))],
        compiler_params=_cparams(("parallel",)),
        name="ffn_norm_router",
    )(x1, norm_w.reshape(1, D), rhi, rlo)


def _routing_kernel(lg_ref, b_ref, eid_ref, wt_ref):
    lg = lg_ref[...] + b_ref[...]
    lane = lax.broadcasted_iota(jnp.int32, lg.shape, 1)
    ninf = -jnp.inf

    def first_argmax(vals, vmax):
        return jnp.min(jnp.where(vals == vmax, lane, LANES), axis=-1, keepdims=True)

    glog = jnp.where(lane < N_GROUPS, lg, ninf)
    gmax = jnp.max(glog, axis=-1, keepdims=True)
    p_grp = 1.0 / jnp.sum(jnp.exp(glog - gmax), axis=-1, keepdims=True)
    grp = first_argmax(glog, gmax)
    base = RT_EXP + grp * EXPERTS_PER_GROUP
    elog = jnp.where((lane >= base) & (lane < base + EXPERTS_PER_GROUP), lg, ninf)
    emax = jnp.max(elog, axis=-1, keepdims=True)
    idx1 = first_argmax(elog, emax)
    elog2 = jnp.where(lane == idx1, ninf, elog)
    emax2 = jnp.max(elog2, axis=-1, keepdims=True)
    idx2 = first_argmax(elog2, emax2)
    e2 = jnp.exp(emax2 - emax)
    w1 = p_grp / (1.0 + e2)
    w2 = p_grp * e2 / (1.0 + e2)
    eid_ref[...] = jnp.where(lane == 0, idx1 - RT_EXP, jnp.where(lane == 1, idx2 - RT_EXP, 0))
    wt_ref[...] = jnp.where(lane == 0, w1, jnp.where(lane == 1, w2, 0.0))


def routing(logits, bias_row, tm=512):
    T = logits.shape[0]
    tm = _tile(T, tm)
    spec = pl.BlockSpec((tm, LANES), lambda i: (i, 0))
    return pl.pallas_call(
        _routing_kernel,
        out_shape=[jax.ShapeDtypeStruct((T, LANES), jnp.int32), jax.ShapeDtypeStruct((T, LANES), jnp.float32)],
        grid=(T // tm,),
        in_specs=[spec, pl.BlockSpec((1, LANES), lambda i: (0, 0))],
        out_specs=[spec, spec],
        compiler_params=_cparams(("parallel",)),
        name="routing",
    )(logits, bias_row)


GATHER_UNROLL = 8


def _row_gather_copy(src_hbm, row, dst, r, sem):
    return pltpu.make_async_copy(src_hbm.at[pl.ds(row, 1), :], dst.at[pl.ds(r, 1), :], sem)


def _moe_ffn_kernel(texp_ref, nused_ref, rowtok_ref, h_hbm, wg_ref, wu_ref, wd_ref, o_ref,
                    xbuf, sem, *, tm):
    i = pl.program_id(0)
    nused = nused_ref[0]

    def start_gather(tile, slot):
        def body(r, carry):
            _row_gather_copy(h_hbm, rowtok_ref[tile * tm + r], xbuf.at[slot], r, sem.at[slot]).start()
            return carry
        lax.fori_loop(0, tm, body, 0, unroll=GATHER_UNROLL)

    def wait_gather(slot):
        pltpu.make_async_copy(h_hbm.at[pl.ds(0, tm), :], xbuf.at[slot], sem.at[slot]).wait()

    @pl.when((i == 0) & (nused > 0))
    def _():
        start_gather(0, 0)

    @pl.when(i + 1 < nused)
    def _():
        start_gather(i + 1, (i + 1) % 2)

    @pl.when(i < nused)
    def _():
        slot = i % 2
        wait_gather(slot)
        x = xbuf[slot].astype(jnp.bfloat16)
        g = jnp.dot(x, wg_ref[0], preferred_element_type=jnp.float32)
        u = jnp.dot(x, wu_ref[0], preferred_element_type=jnp.float32)
        hmid = (g * jax.nn.sigmoid(g) * u).astype(jnp.bfloat16)
        y = jnp.dot(hmid, wd_ref[0], preferred_element_type=jnp.float32)
        o_ref[...] = y

    @pl.when(i >= nused)
    def _():
        o_ref[...] = jnp.zeros(o_ref.shape, o_ref.dtype)


def moe_ffn(h2, tile_expert, n_used, row_token, wg, wu, wd, tm):
    T, D = h2.shape
    R = row_token.shape[0]
    FF = wg.shape[2]
    ntiles = R // tm
    grid_spec = pltpu.PrefetchScalarGridSpec(
        num_scalar_prefetch=3,
        grid=(ntiles,),
        in_specs=[pl.BlockSpec(memory_space=pl.ANY),
                  pl.BlockSpec((1, D, FF), lambda i, te, nu, rt: (te[i], 0, 0)),
                  pl.BlockSpec((1, D, FF), lambda i, te, nu, rt: (te[i], 0, 0)),
                  pl.BlockSpec((1, FF, D), lambda i, te, nu, rt: (te[i], 0, 0))],
        out_specs=pl.BlockSpec((tm, D), lambda i, te, nu, rt: (i, 0)),
        scratch_shapes=[pltpu.VMEM((2, tm, D), jnp.float32), pltpu.SemaphoreType.DMA((2,))],
    )
    return pl.pallas_call(
        functools.partial(_moe_ffn_kernel, tm=tm),
        out_shape=jax.ShapeDtypeStruct((R, D), jnp.float32),
        grid_spec=grid_spec,
        compiler_params=_cparams(("arbitrary",)),
        name="moe_ffn",
    )(tile_expert, n_used, row_token, h2, wg, wu, wd)


def _combine_kernel(dest_ref, x_ref, wt_ref, ys_hbm, o_ref, buf, sem, *, tm):
    i = pl.program_id(0)
    n = pl.num_programs(0)

    def start_gather(tile, slot):
        def body(r, carry):
            for kk in range(TOP_K_EXPERTS):
                row = dest_ref[(tile * tm + r) * TOP_K_EXPERTS + kk]
                _row_gather_copy(ys_hbm, row, buf.at[slot, kk], r, sem.at[slot]).start()
            return carry
        lax.fori_loop(0, tm, body, 0, unroll=GATHER_UNROLL)

    def wait_gather(slot):
        for kk in range(TOP_K_EXPERTS):
            pltpu.make_async_copy(ys_hbm.at[pl.ds(0, tm), :], buf.at[slot, kk], sem.at[slot]).wait()

    @pl.when(i == 0)
    def _():
        start_gather(0, 0)

    @pl.when(i + 1 < n)
    def _():
        start_gather(i + 1, (i + 1) % 2)

    slot = i % 2
    wait_gather(slot)
    wt = wt_ref[...]
    o_ref[...] = x_ref[...] + wt[:, 0:1] * buf[slot, 0] + wt[:, 1:2] * buf[slot, 1]


def moe_combine(x1, wt_lanes, ys, dest_row, tm=128):
    T, D = x1.shape
    tm = _tile(T, tm)
    grid_spec = pltpu.PrefetchScalarGridSpec(
        num_scalar_prefetch=1,
        grid=(T // tm,),
        in_specs=[pl.BlockSpec((tm, D), lambda i, d: (i, 0)), pl.BlockSpec((tm, LANES), lambda i, d: (i, 0)),
                  pl.BlockSpec(memory_space=pl.ANY)],
        out_specs=pl.BlockSpec((tm, D), lambda i, d: (i, 0)),
        scratch_shapes=[pltpu.VMEM((2, TOP_K_EXPERTS, tm, D), jnp.float32), pltpu.SemaphoreType.DMA((2,))],
    )
    return pl.pallas_call(
        functools.partial(_combine_kernel, tm=tm),
        out_shape=jax.ShapeDtypeStruct((T, D), jnp.float32),
        grid_spec=grid_spec,
        compiler_params=_cparams(("arbitrary",)),
        name="moe_combine",
    )(dest_row, x1, wt_lanes, ys)


PLAN_BLOCK = 128


def moe_dispatch_plan(eid, tm):
    T = eid.shape[0]
    A = T * TOP_K_EXPERTS
    e_flat = eid.reshape(A)
    blk = PLAN_BLOCK if A % PLAN_BLOCK == 0 else A
    onehot = (e_flat[:, None] == jnp.arange(N_EXPERTS, dtype=jnp.int32)[None, :]).astype(jnp.float32)
    oh = onehot.reshape(A // blk, blk, N_EXPERTS)
    before = jnp.tril(jnp.ones((blk, blk), jnp.float32), -1)
    within = jnp.einsum('ij,bje->bie', before, oh)
    totals = jnp.sum(oh, axis=1)
    offs = jnp.cumsum(totals, axis=0) - totals
    rank = jnp.sum((within + offs[:, None, :]) * oh, axis=2).reshape(A).astype(jnp.int32)
    counts = (offs[-1] + totals[-1]).astype(jnp.int32)
    padded = ((counts + tm - 1) // tm) * tm
    pend = jnp.cumsum(padded)
    pstart = pend - padded
    dest_row = (pstart[e_flat] + rank).astype(jnp.int32)
    R = ((A + N_EXPERTS * (tm - 1)) + tm - 1) // tm * tm
    row_token = jnp.zeros((R,), jnp.int32).at[dest_row].set(jnp.arange(A, dtype=jnp.int32) // TOP_K_EXPERTS)
    tile_start = jnp.arange(R // tm, dtype=jnp.int32) * tm
    tile_expert = jnp.minimum(jnp.sum(tile_start[:, None] >= pend[None, :], axis=1), N_EXPERTS - 1).astype(jnp.int32)
    n_used = (pend[-1] // tm).astype(jnp.int32).reshape(1)
    return dest_row, row_token, tile_expert, n_used


def _split_w_in(w, D):
    sizes = (ATT_Q_WIDTH, ATT_KV_WIDTH, ATT_KV_WIDTH, IDX_Q_WIDTH, IDX_DIM, IDX_HEADS,
             GDN_WIDTH, GDN_WIDTH, GDN_WIDTH, GDN_WIDTH, GDN_HEADS, GDN_HEADS, D, D)
    offs = [0]
    for s in sizes:
        offs.append(offs[-1] + s)
    w = w.astype(jnp.bfloat16)
    seg = [w[:, offs[i]:offs[i + 1]] for i in range(len(sizes))]
    aq, ak, av, iq, ik, iw, gq, gk, gv, gz, ga, gb, gate_a, gate_g = seg
    w_big = jnp.concatenate([aq, ak, av, iq, gq, gk, gv, gz, gate_a, gate_g], axis=1)
    pad = jnp.zeros((w.shape[0], LANES - (IDX_DIM + IDX_HEADS + 2 * GDN_HEADS)), w.dtype)
    w_small = jnp.concatenate([ik, iw, ga, gb, pad], axis=1)
    return w_big, w_small


def kernel(x, positions, mix_norm_w, w_in, q_norm_w, k_norm_w, idx_k_norm_w, conv_w, a_log, dt_bias, gdn_norm_w, w_proj_attn, w_proj_gdn, w_out, ffn_norm_w, w_router_group, b_router_group, w_router_expert, b_router_expert, w_gate, w_up, w_down):
    B, S, D = x.shape
    T = B * S
    bf16 = jnp.bfloat16
    moe_tm = 256 if T * TOP_K_EXPERTS >= 256 * N_EXPERTS else 64
    xt = x.reshape(T, D)
    cos_t, sin_t = rope_tables(positions)
    for l in range(w_in.shape[0]):
        w_big, w_small = _split_w_in(w_in[l], D)
        h = rmsnorm(xt, mix_norm_w[l], bf16)
        NE, _, FF = w_gate[l].shape
        later_weights = [w_proj_attn[l], w_proj_gdn[l], w_out[l], w_gate[l].reshape(NE * D, FF),
                         w_up[l].reshape(NE * D, FF), w_down[l].reshape(NE * FF, D)]
        P, (wpa, wpg, wo, wg, wu, wd) = matmul_with_casts(h, w_big, bf16, later_weights, name="in_proj")
        wg, wu, wd = wg.reshape(NE, D, FF), wu.reshape(NE, D, FF), wd.reshape(NE, FF, D)
        Psm = matmul(h, w_small, jnp.float32, name="in_proj_small")
        q, k, qi, ki, wi = attn_prep(P, Psm, cos_t, sin_t, q_norm_w[l], k_norm_w[l], idx_k_norm_w[l])
        bias = indexer_mask(qi, ki, wi, B, S)
        y_attn = masked_attention(q, k, P, bias, B, S)
        y_gdn = gated_deltanet(P, Psm, conv_w[l], a_log[l], dt_bias[l], gdn_norm_w[l], B, S)
        mixed = gated_merge(y_attn, y_gdn, wpa, wpg, P, D)
        x1 = out_proj_residual(mixed, wo, xt)
        w_router = jnp.zeros((D, LANES), jnp.float32)
        w_router = w_router.at[:, :N_GROUPS].set(w_router_group[l])
        w_router = w_router.at[:, RT_EXP:RT_EXP + N_EXPERTS].set(
            w_router_expert[l].transpose(1, 0, 2).reshape(D, N_EXPERTS))
        b_router = jnp.zeros((1, LANES), jnp.float32)
        b_router = b_router.at[0, :N_GROUPS].set(b_router_group[l])
        b_router = b_router.at[0, RT_EXP:RT_EXP + N_EXPERTS].set(b_router_expert[l].reshape(N_EXPERTS))
        h2, logits = ffn_norm_router(x1, ffn_norm_w[l], w_router)
        eid_l, wt_l = routing(logits, b_router)
        dest_row, row_token, tile_expert, n_used = moe_dispatch_plan(eid_l[:, :TOP_K_EXPERTS], moe_tm)
        ys = moe_ffn(h2, tile_expert, n_used, row_token, wg, wu, wd, moe_tm)
        xt = moe_combine(x1, wt_l, ys, dest_row)
    return xt.reshape(B, S, D)
```

```python
import functools
import math

import jax
import jax.numpy as jnp
from jax import lax
from jax.experimental import pallas as pl
from jax.experimental.pallas import tpu as pltpu

ATT_HEADS = 16
ATT_KV_HEADS = 4
HEAD_DIM = 128
IDX_HEADS = 16
IDX_DIM = 64
TOPK_MAX = 256
ROPE_THETA = 10000.0
GDN_HEADS = 16
GDN_DK = 128
GDN_DV = 128
CONV_WIDTH = 4
CHUNK = 64
N_GROUPS = 4
EXPERTS_PER_GROUP = 8
N_EXPERTS = N_GROUPS * EXPERTS_PER_GROUP
TOP_K_EXPERTS = 2
EPS = 1e-6

ATT_Q_WIDTH = ATT_HEADS * HEAD_DIM
ATT_KV_WIDTH = ATT_KV_HEADS * HEAD_DIM
IDX_Q_WIDTH = IDX_HEADS * IDX_DIM
GDN_WIDTH = GDN_HEADS * GDN_DK

LANES = 128
BF16_SUBLANES = 16
VMEM_LIMIT = 56 * 1024 * 1024
NEG_BIG = -1e30
LOG2E = math.log2(math.e)
SCORE_MASKED = 3.0e38
BISECT_MAX_STEPS = 192
BISECT_UNROLL = 4
COUNT_ROWS = 128

OFF_AQ = 0
OFF_AK = OFF_AQ + ATT_Q_WIDTH
OFF_AV = OFF_AK + ATT_KV_WIDTH
OFF_IQ = OFF_AV + ATT_KV_WIDTH
ATT_COLS = OFF_IQ + IDX_Q_WIDTH
OFF_GQ = 0
OFF_GK = OFF_GQ + GDN_WIDTH
OFF_GV = OFF_GK + GDN_WIDTH
OFF_GZ = OFF_GV + GDN_WIDTH
OFF_GATES = OFF_GZ + GDN_WIDTH
SM_IK = 0
SM_IW = SM_IK + IDX_DIM
SM_GA = SM_IW + IDX_HEADS
SM_GB = SM_GA + GDN_HEADS
RT_EXP = 8


def _cparams(sem):
    return pltpu.CompilerParams(dimension_semantics=sem, vmem_limit_bytes=VMEM_LIMIT)


def _tile(n, pref):
    t = min(n, pref)
    assert n % t == 0, (n, pref)
    return t


def _rmsnorm_kernel(x_ref, w_ref, o_ref):
    x = x_ref[...]
    ms = jnp.mean(x * x, axis=-1, keepdims=True)
    o_ref[...] = (x * lax.rsqrt(ms + EPS) * w_ref[...]).astype(o_ref.dtype)


def rmsnorm(x, w, out_dtype, tm=256):
    T, D = x.shape
    tm = _tile(T, tm)
    return pl.pallas_call(
        _rmsnorm_kernel,
        out_shape=jax.ShapeDtypeStruct((T, D), out_dtype),
        grid=(T // tm,),
        in_specs=[pl.BlockSpec((tm, D), lambda i: (i, 0)), pl.BlockSpec((1, D), lambda i: (0, 0))],
        out_specs=pl.BlockSpec((tm, D), lambda i: (i, 0)),
        compiler_params=_cparams(("parallel",)),
        name="rmsnorm",
    )(x, w.reshape(1, D))


def _matmul_kernel(a_ref, b_ref, o_ref):
    o_ref[...] = jnp.dot(a_ref[...], b_ref[...], preferred_element_type=jnp.float32).astype(o_ref.dtype)


def matmul(a, b, out_dtype, tm=1024, tn=1024, name="matmul"):
    M, K = a.shape
    _, N = b.shape
    tm, tn = _tile(M, tm), _tile(N, tn)
    return pl.pallas_call(
        _matmul_kernel,
        out_shape=jax.ShapeDtypeStruct((M, N), out_dtype),
        grid=(N // tn, M // tm),
        in_specs=[pl.BlockSpec((tm, K), lambda j, i: (i, 0)), pl.BlockSpec((K, tn), lambda j, i: (0, j))],
        out_specs=pl.BlockSpec((tm, tn), lambda j, i: (i, j)),
        compiler_params=_cparams(("parallel", "parallel")),
        name=name,
    )(a, b)


def _matmul_cast_kernel(a_ref, b_ref, *refs):
    ncast = (len(refs) - 1) // 2
    o_ref = refs[ncast]
    o_ref[...] = jnp.dot(a_ref[...], b_ref[...], preferred_element_type=jnp.float32).astype(o_ref.dtype)
    for x_ref, y_ref in zip(refs[:ncast], refs[ncast + 1:]):
        y_ref[...] = x_ref[...].astype(y_ref.dtype)


def _cast_block_rows(rows, nsteps):
    for rb in range(BF16_SUBLANES, rows + 1, BF16_SUBLANES):
        if rows % rb == 0 and rows // rb <= nsteps:
            return rb
    raise ValueError((rows, nsteps))


def matmul_with_casts(a, b, out_dtype, to_cast, tm=1024, tn=1024, name="matmul_casts"):
    M, K = a.shape
    _, N = b.shape
    tm, tn = _tile(M, tm), _tile(N, tn)
    ni = M // tm
    nsteps = (N // tn) * ni
    cast_in, cast_out, cast_shapes = [], [], []
    for x in to_cast:
        rows, cols = x.shape
        rb = _cast_block_rows(rows, nsteps)
        last = rows // rb - 1
        imap = functools.partial(lambda j, i, last: (jnp.minimum(j * ni + i, last), 0), last=last)
        cast_in.append(pl.BlockSpec((rb, cols), imap))
        cast_out.append(pl.BlockSpec((rb, cols), imap))
        cast_shapes.append(jax.ShapeDtypeStruct((rows, cols), jnp.bfloat16))
    outs = pl.pallas_call(
        _matmul_cast_kernel,
        out_shape=[jax.ShapeDtypeStruct((M, N), out_dtype)] + cast_shapes,
        grid=(N // tn, ni),
        in_specs=[pl.BlockSpec((tm, K), lambda j, i: (i, 0)), pl.BlockSpec((K, tn), lambda j, i: (0, j))] + cast_in,
        out_specs=[pl.BlockSpec((tm, tn), lambda j, i: (i, j))] + cast_out,
        compiler_params=_cparams(("arbitrary", "arbitrary")),
        name=name,
    )(a, b, *to_cast)
    return outs[0], outs[1:]


def _rope_table_kernel(pos_ref, inv_ref, sgn_ref, cos_ref, sin_ref):
    ang = pos_ref[...] * inv_ref[...]
    cos_ref[...] = jnp.cos(ang)
    sin_ref[...] = jnp.sin(ang) * sgn_ref[...]


def rope_tables(positions):
    T = positions.size
    pos = positions.reshape(T, 1).astype(jnp.float32)

    def inv(d):
        return ROPE_THETA ** (-jnp.arange(0, d, 2, dtype=jnp.float32) / d)

    i128, i64 = inv(HEAD_DIM), inv(IDX_DIM)
    inv_row = jnp.concatenate([i128, i128, i64, i64, i64, i64]).reshape(1, 2 * LANES)
    s128 = jnp.concatenate([-jnp.ones(HEAD_DIM // 2), jnp.ones(HEAD_DIM // 2)])
    s64 = jnp.concatenate([-jnp.ones(IDX_DIM // 2), jnp.ones(IDX_DIM // 2)])
    sgn_row = jnp.concatenate([s128, s64, s64]).astype(jnp.float32).reshape(1, 2 * LANES)
    tm = _tile(T, 512)
    spec = pl.BlockSpec((tm, 2 * LANES), lambda i: (i, 0))
    row = pl.BlockSpec((1, 2 * LANES), lambda i: (0, 0))
    return pl.pallas_call(
        _rope_table_kernel,
        out_shape=[jax.ShapeDtypeStruct((T, 2 * LANES), jnp.float32)] * 2,
        grid=(T // tm,),
        in_specs=[pl.BlockSpec((tm, 1), lambda i: (i, 0)), row, row],
        out_specs=[spec, spec],
        compiler_params=_cparams(("parallel",)),
        name="rope_tables",
    )(pos, inv_row, sgn_row)


def _rope128(x, cos, sin_signed):
    return x * cos + pltpu.roll(x, HEAD_DIM // 2, 1) * sin_signed


def _rope64x2(x, cos, sin_signed, lane):
    half = IDX_DIM // 2
    first = (lane % IDX_DIM) < half
    partner = jnp.where(first, pltpu.roll(x, LANES - half, 1), pltpu.roll(x, half, 1))
    return x * cos + partner * sin_signed


def _attn_prep_kernel(aq_ref, ak_ref, iq_ref, sm_ref, cos_ref, sin_ref, qw_ref, kw_ref, ikw_ref,
                      q_ref, k_ref, qi_ref, ki_ref, wi_ref):
    cos_a, sin_a = cos_ref[:, :LANES], sin_ref[:, :LANES]
    cos_i, sin_i = cos_ref[:, LANES:], sin_ref[:, LANES:]
    tm = cos_a.shape[0]
    lane = lax.broadcasted_iota(jnp.int32, (tm, LANES), 1)

    def head_norm(xh, w):
        ms = jnp.mean(xh * xh, axis=-1, keepdims=True)
        return xh * lax.rsqrt(ms + EPS) * w

    for h in range(ATT_HEADS):
        sl = slice(h * HEAD_DIM, (h + 1) * HEAD_DIM)
        xh = head_norm(aq_ref[:, sl].astype(jnp.float32), qw_ref[...])
        q_ref[:, sl] = (_rope128(xh, cos_a, sin_a) * (LOG2E * HEAD_DIM ** -0.5)).astype(q_ref.dtype)
    for h in range(ATT_KV_HEADS):
        sl = slice(h * HEAD_DIM, (h + 1) * HEAD_DIM)
        xh = head_norm(ak_ref[:, sl].astype(jnp.float32), kw_ref[...])
        k_ref[:, sl] = _rope128(xh, cos_a, sin_a).astype(k_ref.dtype)
    for p in range(IDX_Q_WIDTH // LANES):
        sl = slice(p * LANES, (p + 1) * LANES)
        xp = iq_ref[:, sl].astype(jnp.float32)
        qi_ref[:, sl] = (_rope64x2(xp, cos_i, sin_i, lane) * (IDX_DIM ** -0.5)).astype(qi_ref.dtype)
    sm = sm_ref[...]
    in_k = lane < IDX_DIM
    xk = jnp.where(in_k, sm, 0.0)
    ms = jnp.sum(xk * xk, axis=-1, keepdims=True) * (1.0 / IDX_DIM)
    kn = xk * lax.rsqrt(ms + EPS) * ikw_ref[...]
    kr = jnp.where(in_k, _rope64x2(kn, cos_i, sin_i, lane), 0.0)
    ki_ref[:, :LANES] = kr.astype(ki_ref.dtype)
    ki_ref[:, LANES:] = pltpu.roll(kr, IDX_DIM, 1).astype(ki_ref.dtype)
    wi_ref[...] = sm * (IDX_HEADS ** -0.5)


def attn_prep(P, Psm, cos_t, sin_t, q_norm_w, k_norm_w, idx_k_norm_w, tm=256):
    T = P.shape[0]
    tm = _tile(T, tm)
    ikw = jnp.concatenate([idx_k_norm_w, jnp.zeros((LANES - IDX_DIM,), jnp.float32)]).reshape(1, LANES)
    row = lambda w: pl.BlockSpec((1, w), lambda i: (0, 0))
    return pl.pallas_call(
        _attn_prep_kernel,
        out_shape=[jax.ShapeDtypeStruct((T, ATT_Q_WIDTH), jnp.bfloat16),
                   jax.ShapeDtypeStruct((T, ATT_KV_WIDTH), jnp.bfloat16),
                   jax.ShapeDtypeStruct((T, IDX_Q_WIDTH), jnp.bfloat16),
                   jax.ShapeDtypeStruct((T, 2 * LANES), jnp.bfloat16),
                   jax.ShapeDtypeStruct((T, LANES), jnp.float32)],
        grid=(T // tm,),
        in_specs=[pl.BlockSpec((tm, ATT_Q_WIDTH), lambda i: (i, OFF_AQ // ATT_Q_WIDTH)),
                  pl.BlockSpec((tm, ATT_KV_WIDTH), lambda i: (i, OFF_AK // ATT_KV_WIDTH)),
                  pl.BlockSpec((tm, IDX_Q_WIDTH), lambda i: (i, OFF_IQ // IDX_Q_WIDTH)),
                  pl.BlockSpec((tm, LANES), lambda i: (i, 0)),
                  pl.BlockSpec((tm, 2 * LANES), lambda i: (i, 0)),
                  pl.BlockSpec((tm, 2 * LANES), lambda i: (i, 0)),
                  row(LANES), row(LANES), row(LANES)],
        out_specs=[pl.BlockSpec((tm, ATT_Q_WIDTH), lambda i: (i, 0)),
                   pl.BlockSpec((tm, ATT_KV_WIDTH), lambda i: (i, 0)),
                   pl.BlockSpec((tm, IDX_Q_WIDTH), lambda i: (i, 0)),
                   pl.BlockSpec((tm, 2 * LANES), lambda i: (i, 0)),
                   pl.BlockSpec((tm, LANES), lambda i: (i, 0))],
        compiler_params=_cparams(("parallel",)),
        name="attn_prep",
    )(P, P, P, Psm, cos_t, sin_t, q_norm_w.reshape(1, LANES), k_norm_w.reshape(1, LANES), ikw)


def _lane_fold(x, op):
    s = x[:, :LANES]
    for j in range(1, x.shape[1] // LANES):
        s = op(s, x[:, j * LANES:(j + 1) * LANES])
    return s


def _indexer_kernel(qi_ref, ki_ref, wi_ref, bias_ref, key_ref, *, n_sel, tq, tk, nchunks):
    q0 = pl.program_id(1) * tq
    nck = (q0 + tq + tk - 1) // tk
    wi = wi_ref[...]
    rowpos = q0 + lax.broadcasted_iota(jnp.int32, (tq, tk), 0)
    colpos0 = lax.broadcasted_iota(jnp.int32, (tq, tk), 1)

    def score_chunk(c, carry):
        mx, mn = carry
        ks = ki_ref[pl.ds(pl.multiple_of(c * tk, tk), tk), :]
        acc = jnp.zeros((tq, tk), jnp.float32)
        for h in range(IDX_HEADS):
            qp = qi_ref[:, (h // 2) * LANES:(h // 2 + 1) * LANES]
            kh = ks[:, (h % 2) * LANES:(h % 2 + 1) * LANES]
            d = lax.dot_general(qp, kh, (((1,), (1,)), ((), ())), preferred_element_type=jnp.float32)
            acc = acc + wi[:, SM_IW + h:SM_IW + h + 1] * jnp.maximum(d, 0.0)
        causal = colpos0 + c * tk <= rowpos
        key_ref[c] = jnp.where(causal, acc, -SCORE_MASKED)
        mx = jnp.maximum(mx, _lane_fold(jnp.where(causal, acc, -SCORE_MASKED), jnp.maximum))
        mn = jnp.minimum(mn, _lane_fold(jnp.where(causal, acc, SCORE_MASKED), jnp.minimum))
        return mx, mn

    mx, mn = lax.fori_loop(0, nck, score_chunk, (jnp.full((tq, LANES), -SCORE_MASKED, jnp.float32),
                                                 jnp.full((tq, LANES), SCORE_MASKED, jnp.float32)))

    ones = jnp.ones((LANES, LANES), jnp.bfloat16)

    def count_ge(probe):
        accs = []
        for r0 in range(0, tq, COUNT_ROWS):
            rows = slice(r0, min(r0 + COUNT_ROWS, tq))

            def body(c, acc, rows=rows):
                for j in range(tk // LANES):
                    acc = acc + jnp.where(key_ref[c, rows, j * LANES:(j + 1) * LANES] >= probe[rows], 1.0, 0.0)
                return acc

            accs.append(lax.fori_loop(0, nck, body, jnp.zeros((rows.stop - rows.start, LANES), jnp.float32)))
        acc = accs[0] if len(accs) == 1 else jnp.concatenate(accs, axis=0)
        return jnp.dot(acc.astype(jnp.bfloat16), ones, preferred_element_type=jnp.float32)

    lo = jnp.broadcast_to(jnp.min(mn, axis=1, keepdims=True), (tq, LANES))
    hi = jnp.broadcast_to(jnp.max(mx, axis=1, keepdims=True), (tq, LANES))
    ncausal = q0 + lax.broadcasted_iota(jnp.int32, (tq, LANES), 0) + 1
    hi = jnp.where(ncausal <= n_sel, lo, hi)

    def unfinished(carry):
        return (carry[0] < BISECT_MAX_STEPS) & (carry[3] > 0)

    def bisect(carry):
        it, lo, hi, _ = carry
        for _ in range(BISECT_UNROLL):
            mid = 0.5 * lo + 0.5 * hi
            cnt = count_ge(mid)
            ok = cnt >= n_sel
            open_row = (mid > lo) & (mid < hi)
            lo = jnp.where(ok, mid, lo)
            hi = jnp.where(cnt == n_sel, mid, jnp.where(ok, hi, mid))
            open_row = open_row & (hi > lo)
        return it + BISECT_UNROLL, lo, hi, jnp.sum(jnp.where(open_row, 1, 0))

    _, thr, _, _ = lax.while_loop(unfinished, bisect, (jnp.int32(0), lo, hi, jnp.int32(1)))

    def write_chunk(c, carry):
        for j in range(tk // LANES):
            cols = slice(j * LANES, (j + 1) * LANES)
            bias_ref[c, :, cols] = jnp.where(key_ref[c, :, cols] >= thr, 0.0, NEG_BIG).astype(bias_ref.dtype)
        return carry

    def write_masked(c, carry):
        bias_ref[c] = jnp.full((tq, tk), NEG_BIG, bias_ref.dtype)
        return carry

    lax.fori_loop(0, nck, write_chunk, 0)
    lax.fori_loop(nck, nchunks, write_masked, 0)


def indexer_mask(qi, ki, wi, B, S, tq=512, tk=512):
    tq, tk = _tile(S, tq), _tile(S, tk)
    n_sel = min(TOPK_MAX, S // 4)
    nq, nchunks = S // tq, S // tk
    return pl.pallas_call(
        functools.partial(_indexer_kernel, n_sel=n_sel, tq=tq, tk=tk, nchunks=nchunks),
        out_shape=jax.ShapeDtypeStruct((B, nq, nchunks, tq, tk), jnp.bfloat16),
        grid=(B, nq),
        in_specs=[pl.BlockSpec((tq, IDX_Q_WIDTH), lambda b, i: (b * nq + i, 0)),
                  pl.BlockSpec((S, 2 * LANES), lambda b, i: (b, 0)),
                  pl.BlockSpec((tq, LANES), lambda b, i: (b * nq + i, 0))],
        out_specs=pl.BlockSpec((None, None, nchunks, tq, tk), lambda b, i: (b, i, 0, 0, 0)),
        scratch_shapes=[pltpu.VMEM((nchunks, tq, tk), jnp.float32)],
        compiler_params=_cparams(("parallel", "parallel")),
        name="indexer_mask",
    )(qi, ki, wi)


def _attn_kernel(q_ref, k_ref, v_ref, bias_ref, o_ref, m_ref, acc_ref, *, tq, tk):
    i, j = pl.program_id(1), pl.program_id(2)
    grp = ATT_HEADS // ATT_KV_HEADS

    @pl.when(j == 0)
    def _():
        m_ref[...] = jnp.full(m_ref.shape, -1e38, jnp.float32)
        acc_ref[...] = jnp.zeros(acc_ref.shape, jnp.float32)

    @pl.when(j * tk <= i * tq + tq - 1)
    def _():
        bias = bias_ref[...].astype(jnp.float32)
        ones = jnp.ones((tk, LANES), v_ref.dtype)
        v1 = [jnp.concatenate([v_ref[:, g * HEAD_DIM:(g + 1) * HEAD_DIM], ones], axis=1) for g in range(ATT_KV_HEADS)]
        s, m_new, alpha, p = {}, {}, {}, {}

        def scores(h):
            g = h // grp
            qh = q_ref[:, h * HEAD_DIM:(h + 1) * HEAD_DIM]
            kh = k_ref[:, g * HEAD_DIM:(g + 1) * HEAD_DIM]
            s[h] = lax.dot_general(qh, kh, (((1,), (1,)), ((), ())), preferred_element_type=jnp.float32) + bias
            m_prev = m_ref[h]
            m_new[h] = jnp.maximum(m_prev, jnp.max(s[h], axis=1, keepdims=True))
            alpha[h] = jnp.exp2(m_prev - m_new[h])
            m_ref[h] = m_new[h]

        def probs(h):
            p[h] = jnp.exp2(s.pop(h) - m_new.pop(h)[:, :1]).astype(v_ref.dtype)

        def values(h):
            a2 = jnp.concatenate([alpha[h], alpha.pop(h)], axis=1)
            acc_ref[h] = a2 * acc_ref[h] + jnp.dot(p.pop(h), v1[h // grp], preferred_element_type=jnp.float32)

        for t in range(ATT_HEADS + 2):
            if t < ATT_HEADS:
                scores(t)
            if 0 <= t - 1 < ATT_HEADS:
                probs(t - 1)
            if 0 <= t - 2 < ATT_HEADS:
                values(t - 2)

    @pl.when(j == pl.num_programs(2) - 1)
    def _():
        for h in range(ATT_HEADS):
            acc = acc_ref[h]
            o_ref[:, h * HEAD_DIM:(h + 1) * HEAD_DIM] = (acc[:, :HEAD_DIM] / acc[:, HEAD_DIM:]).astype(o_ref.dtype)


def masked_attention(q, k, P, bias, B, S):
    _, nq, nkv, tq, tk = bias.shape

    def kv_idx(i, j):
        return jnp.minimum(j, (i * tq + tq - 1) // tk)

    return pl.pallas_call(
        functools.partial(_attn_kernel, tq=tq, tk=tk),
        out_shape=jax.ShapeDtypeStruct((B * S, ATT_Q_WIDTH), jnp.bfloat16),
        grid=(B, nq, nkv),
        in_specs=[pl.BlockSpec((tq, ATT_Q_WIDTH), lambda b, i, j: (b * nq + i, 0)),
                  pl.BlockSpec((tk, ATT_KV_WIDTH), lambda b, i, j: (b * nkv + kv_idx(i, j), 0)),
                  pl.BlockSpec((tk, ATT_KV_WIDTH), lambda b, i, j: (b * nkv + kv_idx(i, j), OFF_AV // ATT_KV_WIDTH)),
                  pl.BlockSpec((None, None, None, tq, tk), lambda b, i, j: (b, i, kv_idx(i, j), 0, 0))],
        out_specs=pl.BlockSpec((tq, ATT_Q_WIDTH), lambda b, i, j: (b * nq + i, 0)),
        scratch_shapes=[pltpu.VMEM((ATT_HEADS, tq, LANES), jnp.float32),
                        pltpu.VMEM((ATT_HEADS, tq, HEAD_DIM + LANES), jnp.float32)],
        compiler_params=_cparams(("parallel", "parallel", "arbitrary")),
        name="masked_attention",
    )(q, k, P, bias)


GDN_HALO = BF16_SUBLANES


def _gdn_kernel(xq_ref, xk_ref, xv_ref, z_ref, sm_ref, cw_ref, alog_ref, dtb_ref, nw_ref, o_ref, state_ref, prev_ref):
    C = CHUNK
    f32, bf16 = jnp.float32, jnp.bfloat16
    heads = range(GDN_HEADS)

    @pl.when(pl.program_id(1) == 0)
    def _():
        state_ref[...] = jnp.zeros(state_ref.shape, f32)
        prev_ref[...] = jnp.zeros(prev_ref.shape, prev_ref.dtype)

    srow = lax.broadcasted_iota(jnp.int32, (CONV_WIDTH * C, GDN_HALO + C), 0)
    scol = lax.broadcasted_iota(jnp.int32, (CONV_WIDTH * C, GDN_HALO + C), 1)
    select = (scol == (srow & (C - 1)) + (srow >> (C.bit_length() - 1)) + (GDN_HALO - (CONV_WIDTH - 1))).astype(bf16)

    def conv_silu(x_ref, part):
        x = x_ref[...]
        xp = jnp.concatenate([prev_ref[part], x], axis=0)
        prev_ref[part] = x[C - GDN_HALO:]
        taps = jnp.dot(select, xp, preferred_element_type=f32)
        w = cw_ref[:, part * GDN_WIDTH:(part + 1) * GDN_WIDTH]
        y = None
        for jj in range(CONV_WIDTH):
            term = w[jj:jj + 1, :] * taps[jj * C:(jj + 1) * C, :]
            y = term if y is None else y + term
        return y * jax.nn.sigmoid(y)

    def l2(yh):
        return yh * lax.rsqrt(jnp.sum(yh * yh, axis=-1, keepdims=True) + EPS)

    yq, yk, yv = conv_silu(xq_ref, 0), conv_silu(xk_ref, 1), conv_silu(xv_ref, 2)
    sm = sm_ref[...]
    a_in = sm + dtb_ref[...]
    softplus = jnp.maximum(a_in, 0.0) + jnp.log(1.0 + jnp.exp(-jnp.abs(a_in)))
    g_all = -jnp.exp(alog_ref[...]) * softplus
    beta_all = jax.nn.sigmoid(sm)

    row = lax.broadcasted_iota(jnp.int32, (C, C), 0)
    col = lax.broadcasted_iota(jnp.int32, (C, C), 1)
    tril, strict = row >= col, row > col
    eye = (row == col).astype(f32)
    hi = lax.Precision.HIGHEST
    gc_col = jnp.dot(tril.astype(f32), g_all, precision=hi, preferred_element_type=f32)
    gc_row = jnp.dot(g_all.T, (row <= col).astype(f32), precision=hi, preferred_element_type=f32)
    nt = (((1,), (1,)), ((), ()))
    dot = functools.partial(jnp.dot, preferred_element_type=f32)
    kq, a, intra, rhs, qd, kd, eglast = [], [], [], [], [], [], []
    for j in heads:
        sl = slice(j * GDN_DK, (j + 1) * GDN_DK)
        gc = gc_col[:, SM_GA + j:SM_GA + j + 1]
        glast = gc[C - 1:C, :]
        eg = jnp.exp(gc)
        beta = beta_all[:, SM_GB + j:SM_GB + j + 1]
        q = (l2(yq[:, sl]) * (GDN_DK ** -0.5)).astype(bf16)
        k = l2(yk[:, sl]).astype(bf16)
        v = yv[:, sl].astype(bf16)
        kf = k.astype(f32)
        kq.append(jnp.concatenate([k, q], axis=0))
        rhs.append(jnp.concatenate([(v.astype(f32) * beta).astype(bf16), (kf * (beta * eg)).astype(bf16)], axis=1))
        qd.append((q.astype(f32) * eg).astype(bf16))
        kd.append((kf * jnp.exp(glast - gc)).astype(bf16))
        eglast.append(jnp.exp(glast))
    skq = [lax.dot_general(kq[j], kq[j][:C], nt, preferred_element_type=f32) for j in heads]
    for j in heads:
        gc = gc_col[:, SM_GA + j:SM_GA + j + 1]
        decay = jnp.exp(jnp.where(tril, gc - gc_row[SM_GA + j:SM_GA + j + 1, :], NEG_BIG))
        a.append(jnp.where(strict, skq[j][:C] * beta_all[:, SM_GB + j:SM_GB + j + 1] * decay, 0.0))
        intra.append((skq[j][C:] * decay).astype(bf16))
    ab = [a[j].astype(bf16) for j in heads]
    xb = [dot(ab[j], ab[j]).astype(bf16) for j in heads]
    tinv = [eye - a[j] for j in heads]
    for it in range(5):
        last = it == 4
        lhs = [tinv[j].astype(bf16) if last else jnp.concatenate([tinv[j].astype(bf16), xb[j]], axis=0) for j in heads]
        prod = [dot(lhs[j], xb[j]) for j in heads]
        tinv = [tinv[j] + prod[j][:C] for j in heads]
        if not last:
            xb = [prod[j][C:].astype(bf16) for j in heads]
    uw = [dot(tinv[j].astype(bf16), rhs[j]) for j in heads]
    sb = [state_ref[j].astype(bf16) for j in heads]
    ws = [dot(jnp.concatenate([uw[j][:, GDN_DV:].astype(bf16), qd[j]], axis=0), sb[j]) for j in heads]
    vb = [(uw[j][:, :GDN_DV] - ws[j][:C]).astype(bf16) for j in heads]
    o = [ws[j][C:] + dot(intra[j], vb[j]) for j in heads]
    upd = [lax.dot_general(kd[j], vb[j], (((0,), (0,)), ((), ())), preferred_element_type=f32) for j in heads]
    for j in heads:
        sl = slice(j * GDN_DK, (j + 1) * GDN_DK)
        state_ref[j] = state_ref[j] * eglast[j] + upd[j]
        ms = jnp.mean(o[j] * o[j], axis=-1, keepdims=True)
        z = z_ref[:, sl].astype(f32)
        o_ref[:, sl] = (o[j] * lax.rsqrt(ms + EPS) * nw_ref[...] * (z * jax.nn.sigmoid(z))).astype(o_ref.dtype)


def gated_deltanet(P, Psm, conv_w, a_log, dt_bias, norm_w, B, S):
    T = B * S
    C, W = CHUNK, GDN_WIDTH
    n = S // C

    def pad_lane(v, off):
        return jnp.zeros((1, LANES), jnp.float32).at[0, off:off + v.shape[0]].set(v)

    blk = lambda off: pl.BlockSpec((C, W), lambda b, c: (b * n + c, off // W))
    row = lambda w: pl.BlockSpec((1, w), lambda b, c: (0, 0))
    return pl.pallas_call(
        _gdn_kernel,
        out_shape=jax.ShapeDtypeStruct((T, W), jnp.bfloat16),
        grid=(B, n),
        in_specs=[blk(OFF_GQ), blk(OFF_GK), blk(OFF_GV), blk(OFF_GZ),
                  pl.BlockSpec((C, LANES), lambda b, c: (b * n + c, 0)),
                  pl.BlockSpec((CONV_WIDTH, 3 * W), lambda b, c: (0, 0)),
                  row(LANES), row(LANES), row(GDN_DV)],
        out_specs=blk(0),
        scratch_shapes=[pltpu.VMEM((GDN_HEADS, GDN_DK, GDN_DV), jnp.float32),
                        pltpu.VMEM((3, GDN_HALO, W), jnp.bfloat16)],
        compiler_params=_cparams(("parallel", "arbitrary")),
        name="gated_deltanet",
    )(P, P, P, P, Psm, conv_w, pad_lane(a_log, SM_GA), pad_lane(dt_bias, SM_GA), norm_w.reshape(1, GDN_DV))


def _merge_kernel(ya_ref, yg_ref, wa_ref, wg_ref, ga_ref, gg_ref, o_ref):
    pa = jnp.dot(ya_ref[...], wa_ref[...], preferred_element_type=jnp.float32)
    pg = jnp.dot(yg_ref[...], wg_ref[...], preferred_element_type=jnp.float32)
    ga = jax.nn.sigmoid(ga_ref[...].astype(jnp.float32))
    gg = jax.nn.sigmoid(gg_ref[...].astype(jnp.float32))
    o_ref[...] = (ga * pa + gg * pg).astype(o_ref.dtype)


def gated_merge(ya, yg, wa, wg, P, D, tm=1024, tn=512):
    T = ya.shape[0]
    tm, tn = _tile(T, tm), _tile(D, tn)
    goff = OFF_GATES // tn
    return pl.pallas_call(
        _merge_kernel,
        out_shape=jax.ShapeDtypeStruct((T, D), jnp.bfloat16),
        grid=(D // tn, T // tm),
        in_specs=[pl.BlockSpec((tm, ATT_Q_WIDTH), lambda j, i: (i, 0)),
                  pl.BlockSpec((tm, GDN_WIDTH), lambda j, i: (i, 0)),
                  pl.BlockSpec((ATT_Q_WIDTH, tn), lambda j, i: (0, j)),
                  pl.BlockSpec((GDN_WIDTH, tn), lambda j, i: (0, j)),
                  pl.BlockSpec((tm, tn), lambda j, i: (i, goff + j)),
                  pl.BlockSpec((tm, tn), lambda j, i: (i, goff + D // tn + j))],
        out_specs=pl.BlockSpec((tm, tn), lambda j, i: (i, j)),
        compiler_params=_cparams(("parallel", "parallel")),
        name="gated_merge",
    )(ya, yg, wa, wg, P, P)


def _outproj_kernel(a_ref, w_ref, x_ref, o_ref):
    o_ref[...] = x_ref[...] + jnp.dot(a_ref[...], w_ref[...], preferred_element_type=jnp.float32)


def out_proj_residual(a, w, x, tm=1024, tn=512):
    T, D = x.shape
    K = a.shape[1]
    tm, tn = _tile(T, tm), _tile(D, tn)
    return pl.pallas_call(
        _outproj_kernel,
        out_shape=jax.ShapeDtypeStruct((T, D), jnp.float32),
        grid=(D // tn, T // tm),
        in_specs=[pl.BlockSpec((tm, K), lambda j, i: (i, 0)),
                  pl.BlockSpec((K, tn), lambda j, i: (0, j)),
                  pl.BlockSpec((tm, tn), lambda j, i: (i, j))],
        out_specs=pl.BlockSpec((tm, tn), lambda j, i: (i, j)),
        compiler_params=_cparams(("parallel", "parallel")),
        name="out_proj_residual",
    )(a, w, x)


def _ffn_norm_router_kernel(x_ref, w_ref, rhi_ref, rlo_ref, h_ref, lg_ref):
    x = x_ref[...]
    ms = jnp.mean(x * x, axis=-1, keepdims=True)
    h = x * lax.rsqrt(ms + EPS) * w_ref[...]
    h_ref[...] = h
    hh = h.astype(jnp.bfloat16)
    hl = (h - hh.astype(jnp.float32)).astype(jnp.bfloat16)
    f32 = jnp.float32
    lg_ref[...] = (jnp.dot(hh, rhi_ref[...], preferred_element_type=f32)
                   + jnp.dot(hh, rlo_ref[...], preferred_element_type=f32)
                   + jnp.dot(hl, rhi_ref[...], preferred_element_type=f32))


def ffn_norm_router(x1, norm_w, w_router, tm=256):
    T, D = x1.shape
    tm = _tile(T, tm)
    rhi = w_router.astype(jnp.bfloat16)
    rlo = (w_router - rhi.astype(jnp.float32)).astype(jnp.bfloat16)
    return pl.pallas_call(
        _ffn_norm_router_kernel,
        out_shape=[jax.ShapeDtypeStruct((T, D), jnp.float32), jax.ShapeDtypeStruct((T, LANES), jnp.float32)],
        grid=(T // tm,),
        in_specs=[pl.BlockSpec((tm, D), lambda i: (i, 0)), pl.BlockSpec((1, D), lambda i: (0, 0)),
                  pl.BlockSpec((D, LANES), lambda i: (0, 0)), pl.BlockSpec((D, LANES), lambda i: (0, 0))],
        out_specs=[pl.BlockSpec((tm, D), lambda i: (i, 0)), pl.BlockSpec((tm, LANES), lambda i: (i, 0))],
        compiler_params=_cparams(("parallel",)),
        name="ffn_norm_router",
    )(x1, norm_w.reshape(1, D), rhi, rlo)


def _routing_kernel(lg_ref, b_ref, eid_ref, wt_ref):
    lg = lg_ref[...] + b_ref[...]
    lane = lax.broadcasted_iota(jnp.int32, lg.shape, 1)
    ninf = -jnp.inf

    def first_argmax(vals, vmax):
        return jnp.min(jnp.where(vals == vmax, lane, LANES), axis=-1, keepdims=True)

    glog = jnp.where(lane < N_GROUPS, lg, ninf)
    gmax = jnp.max(glog, axis=-1, keepdims=True)
    p_grp = 1.0 / jnp.sum(jnp.exp(glog - gmax), axis=-1, keepdims=True)
    grp = first_argmax(glog, gmax)
    base = RT_EXP + grp * EXPERTS_PER_GROUP
    elog = jnp.where((lane >= base) & (lane < base + EXPERTS_PER_GROUP), lg, ninf)
    emax = jnp.max(elog, axis=-1, keepdims=True)
    idx1 = first_argmax(elog, emax)
    elog2 = jnp.where(lane == idx1, ninf, elog)
    emax2 = jnp.max(elog2, axis=-1, keepdims=True)
    idx2 = first_argmax(elog2, emax2)
    e2 = jnp.exp(emax2 - emax)
    w1 = p_grp / (1.0 + e2)
    w2 = p_grp * e2 / (1.0 + e2)
    eid_ref[...] = jnp.where(lane == 0, idx1 - RT_EXP, jnp.where(lane == 1, idx2 - RT_EXP, 0))
    wt_ref[...] = jnp.where(lane == 0, w1, jnp.where(lane == 1, w2, 0.0))


def routing(logits, bias_row, tm=512):
    T = logits.shape[0]
    tm = _tile(T, tm)
    spec = pl.BlockSpec((tm, LANES), lambda i: (i, 0))
    return pl.pallas_call(
        _routing_kernel,
        out_shape=[jax.ShapeDtypeStruct((T, LANES), jnp.int32), jax.ShapeDtypeStruct((T, LANES), jnp.float32)],
        grid=(T // tm,),
        in_specs=[spec, pl.BlockSpec((1, LANES), lambda i: (0, 0))],
        out_specs=[spec, spec],
        compiler_params=_cparams(("parallel",)),
        name="routing",
    )(logits, bias_row)


GATHER_UNROLL = 8


def _row_gather_copy(src_hbm, row, dst, r, sem):
    return pltpu.make_async_copy(src_hbm.at[pl.ds(row, 1), :], dst.at[pl.ds(r, 1), :], sem)


def _moe_ffn_kernel(texp_ref, nused_ref, rowtok_ref, h_hbm, wg_ref, wu_ref, wd_ref, o_ref,
                    xbuf, sem, *, tm):
    i = pl.program_id(0)
    nused = nused_ref[0]

    def start_gather(tile, slot):
        def body(r, carry):
            _row_gather_copy(h_hbm, rowtok_ref[tile * tm + r], xbuf.at[slot], r, sem.at[slot]).start()
            return carry
        lax.fori_loop(0, tm, body, 0, unroll=GATHER_UNROLL)

    def wait_gather(slot):
        pltpu.make_async_copy(h_hbm.at[pl.ds(0, tm), :], xbuf.at[slot], sem.at[slot]).wait()

    @pl.when((i == 0) & (nused > 0))
    def _():
        start_gather(0, 0)

    @pl.when(i + 1 < nused)
    def _():
        start_gather(i + 1, (i + 1) % 2)

    @pl.when(i < nused)
    def _():
        slot = i % 2
        wait_gather(slot)
        x = xbuf[slot].astype(jnp.bfloat16)
        g = jnp.dot(x, wg_ref[0], preferred_element_type=jnp.float32)
        u = jnp.dot(x, wu_ref[0], preferred_element_type=jnp.float32)
        hmid = (g * jax.nn.sigmoid(g) * u).astype(jnp.bfloat16)
        y = jnp.dot(hmid, wd_ref[0], preferred_element_type=jnp.float32)
        o_ref[...] = y

    @pl.when(i >= nused)
    def _():
        o_ref[...] = jnp.zeros(o_ref.shape, o_ref.dtype)


def moe_ffn(h2, tile_expert, n_used, row_token, wg, wu, wd, tm):
    T, D = h2.shape
    R = row_token.shape[0]
    FF = wg.shape[2]
    ntiles = R // tm
    grid_spec = pltpu.PrefetchScalarGridSpec(
        num_scalar_prefetch=3,
        grid=(ntiles,),
        in_specs=[pl.BlockSpec(memory_space=pl.ANY),
                  pl.BlockSpec((1, D, FF), lambda i, te, nu, rt: (te[i], 0, 0)),
                  pl.BlockSpec((1, D, FF), lambda i, te, nu, rt: (te[i], 0, 0)),
                  pl.BlockSpec((1, FF, D), lambda i, te, nu, rt: (te[i], 0, 0))],
        out_specs=pl.BlockSpec((tm, D), lambda i, te, nu, rt: (i, 0)),
        scratch_shapes=[pltpu.VMEM((2, tm, D), jnp.float32), pltpu.SemaphoreType.DMA((2,))],
    )
    return pl.pallas_call(
        functools.partial(_moe_ffn_kernel, tm=tm),
        out_shape=jax.ShapeDtypeStruct((R, D), jnp.float32),
        grid_spec=grid_spec,
        compiler_params=_cparams(("arbitrary",)),
        name="moe_ffn",
    )(tile_expert, n_used, row_token, h2, wg, wu, wd)


def _combine_kernel(dest_ref, x_ref, wt_ref, ys_hbm, o_ref, buf, sem, *, tm):
    i = pl.program_id(0)
    n = pl.num_programs(0)

    def start_gather(tile, slot):
        def body(r, carry):
            for kk in range(TOP_K_EXPERTS):
                row = dest_ref[(tile * tm + r) * TOP_K_EXPERTS + kk]
                _row_gather_copy(ys_hbm, row, buf.at[slot, kk], r, sem.at[slot]).start()
            return carry
        lax.fori_loop(0, tm, body, 0, unroll=GATHER_UNROLL)

    def wait_gather(slot):
        for kk in range(TOP_K_EXPERTS):
            pltpu.make_async_copy(ys_hbm.at[pl.ds(0, tm), :], buf.at[slot, kk], sem.at[slot]).wait()

    @pl.when(i == 0)
    def _():
        start_gather(0, 0)

    @pl.when(i + 1 < n)
    def _():
        start_gather(i + 1, (i + 1) % 2)

    slot = i % 2
    wait_gather(slot)
    wt = wt_ref[...]
    o_ref[...] = x_ref[...] + wt[:, 0:1] * buf[slot, 0] + wt[:, 1:2] * buf[slot, 1]


def moe_combine(x1, wt_lanes, ys, dest_row, tm=128):
    T, D = x1.shape
    tm = _tile(T, tm)
    grid_spec = pltpu.PrefetchScalarGridSpec(
        num_scalar_prefetch=1,
        grid=(T // tm,),
        in_specs=[pl.BlockSpec((tm, D), lambda i, d: (i, 0)), pl.BlockSpec((tm, LANES), lambda i, d: (i, 0)),
                  pl.BlockSpec(memory_space=pl.ANY)],
        out_specs=pl.BlockSpec((tm, D), lambda i, d: (i, 0)),
        scratch_shapes=[pltpu.VMEM((2, TOP_K_EXPERTS, tm, D), jnp.float32), pltpu.SemaphoreType.DMA((2,))],
    )
    return pl.pallas_call(
        functools.partial(_combine_kernel, tm=tm),
        out_shape=jax.ShapeDtypeStruct((T, D), jnp.float32),
        grid_spec=grid_spec,
        compiler_params=_cparams(("arbitrary",)),
        name="moe_combine",
    )(dest_row, x1, wt_lanes, ys)


PLAN_BLOCK = 128


def moe_dispatch_plan(eid, tm):
    T = eid.shape[0]
    A = T * TOP_K_EXPERTS
    e_flat = eid.reshape(A)
    blk = PLAN_BLOCK if A % PLAN_BLOCK == 0 else A
    onehot = (e_flat[:, None] == jnp.arange(N_EXPERTS, dtype=jnp.int32)[None, :]).astype(jnp.float32)
    oh = onehot.reshape(A // blk, blk, N_EXPERTS)
    before = jnp.tril(jnp.ones((blk, blk), jnp.float32), -1)
    within = jnp.einsum('ij,bje->bie', before, oh)
    totals = jnp.sum(oh, axis=1)
    offs = jnp.cumsum(totals, axis=0) - totals
    rank = jnp.sum((within + offs[:, None, :]) * oh, axis=2).reshape(A).astype(jnp.int32)
    counts = (offs[-1] + totals[-1]).astype(jnp.int32)
    padded = ((counts + tm - 1) // tm) * tm
    pend = jnp.cumsum(padded)
    pstart = pend - padded
    dest_row = (pstart[e_flat] + rank).astype(jnp.int32)
    R = ((A + N_EXPERTS * (tm - 1)) + tm - 1) // tm * tm
    row_token = jnp.zeros((R,), jnp.int32).at[dest_row].set(jnp.arange(A, dtype=jnp.int32) // TOP_K_EXPERTS)
    tile_start = jnp.arange(R // tm, dtype=jnp.int32) * tm
    tile_expert = jnp.minimum(jnp.sum(tile_start[:, None] >= pend[None, :], axis=1), N_EXPERTS - 1).astype(jnp.int32)
    n_used = (pend[-1] // tm).astype(jnp.int32).reshape(1)
    return dest_row, row_token, tile_expert, n_used


REGROUP_TN = 1024
SKIP_IDX = IDX_DIM + IDX_HEADS
SKIP_GDN = SKIP_IDX + 2 * GDN_HEADS


def _in_proj_a_kernel(h_ref, w_ref, ra_ref, rb_ref, pa_ref, wb_ref, wscr, *, ncb, nblk, ni):
    j, i = pl.program_id(0), pl.program_id(1)
    tn = REGROUP_TN

    @pl.when(i == 0)
    def _():
        wscr[...] = w_ref[...].astype(wscr.dtype)

    pa_ref[...] = jnp.dot(h_ref[...], wscr[...], preferred_element_type=jnp.float32).astype(pa_ref.dtype)
    c = jnp.minimum(j * ni + i, nblk - 1) % ncb

    def shifted(skip):
        both = jnp.concatenate([ra_ref[...], rb_ref[...]], axis=1)
        return pltpu.roll(both, 2 * tn - skip, 1)[:, :tn].astype(wb_ref.dtype)

    @pl.when(c < OFF_GATES // tn)
    def _():
        wb_ref[...] = shifted(SKIP_IDX)

    @pl.when(c >= OFF_GATES // tn)
    def _():
        wb_ref[...] = shifted(SKIP_GDN)


def in_proj_attention_side(h, w, D, tm=1024, tn=512, tr=512):
    T, K = h.shape
    rt = REGROUP_TN
    tm, tr = _tile(T, tm), _tile(D, tr)
    nb_cols = OFF_GATES + 2 * D
    assert nb_cols % rt == 0 and w.shape[1] == ATT_COLS + nb_cols + SKIP_GDN and ATT_COLS % rt == 0
    ncb = nb_cols // rt
    nblk = ncb * (D // tr)
    ni = T // tm
    while (ATT_COLS // tn) * ni < nblk:
        tn //= 2
    assert tn >= LANES and ATT_COLS % tn == 0
    src0 = ATT_COLS // rt

    def blk(j, i):
        b = jnp.minimum(j * ni + i, nblk - 1)
        return b // ncb, b % ncb

    p_a, w_b = pl.pallas_call(
        functools.partial(_in_proj_a_kernel, ncb=ncb, nblk=nblk, ni=ni),
        out_shape=[jax.ShapeDtypeStruct((T, ATT_COLS), jnp.bfloat16), jax.ShapeDtypeStruct((D, nb_cols), jnp.bfloat16)],
        grid=(ATT_COLS // tn, ni),
        in_specs=[pl.BlockSpec((tm, K), lambda j, i: (i, 0)),
                  pl.BlockSpec((K, tn), lambda j, i: (0, j)),
                  pl.BlockSpec((tr, rt), lambda j, i: (blk(j, i)[0], src0 + blk(j, i)[1])),
                  pl.BlockSpec((tr, rt), lambda j, i: (blk(j, i)[0], src0 + blk(j, i)[1] + 1))],
        out_specs=[pl.BlockSpec((tm, tn), lambda j, i: (i, j)),
                   pl.BlockSpec((tr, rt), lambda j, i: blk(j, i))],
        scratch_shapes=[pltpu.VMEM((K, tn), jnp.bfloat16)],
        compiler_params=_cparams(("arbitrary", "arbitrary")),
        name="in_proj_a",
    )(h, w, w, w)
    i0, g0 = ATT_COLS, ATT_COLS + OFF_GATES + SKIP_IDX
    pad = jnp.zeros((D, LANES - SKIP_GDN), w.dtype)
    w_small = jnp.concatenate([w[:, i0:i0 + SKIP_IDX], w[:, g0:g0 + 2 * GDN_HEADS], pad], axis=1).astype(jnp.bfloat16)
    return p_a, w_b, w_small


def kernel(x, positions, mix_norm_w, w_in, q_norm_w, k_norm_w, idx_k_norm_w, conv_w, a_log, dt_bias, gdn_norm_w, w_proj_attn, w_proj_gdn, w_out, ffn_norm_w, w_router_group, b_router_group, w_router_expert, b_router_expert, w_gate, w_up, w_down):
    B, S, D = x.shape
    T = B * S
    bf16 = jnp.bfloat16
    moe_tm = 256 if T * TOP_K_EXPERTS >= 256 * N_EXPERTS else 64
    xt = x.reshape(T, D)
    cos_t, sin_t = rope_tables(positions)
    for l in range(w_in.shape[0]):
        h = rmsnorm(xt, mix_norm_w[l], bf16)
        P_a, w_b, w_small = in_proj_attention_side(h, w_in[l], D)
        NE, _, FF = w_gate[l].shape
        later_weights = [w_proj_attn[l], w_proj_gdn[l], w_out[l], w_gate[l].reshape(NE * D, FF),
                         w_up[l].reshape(NE * D, FF), w_down[l].reshape(NE * FF, D)]
        P_b, (wpa, wpg, wo, wg, wu, wd) = matmul_with_casts(h, w_b, bf16, later_weights, name="in_proj_b")
        wg, wu, wd = wg.reshape(NE, D, FF), wu.reshape(NE, D, FF), wd.reshape(NE, FF, D)
        Psm = matmul(h, w_small, jnp.float32, name="in_proj_small")
        q, k, qi, ki, wi = attn_prep(P_a, Psm, cos_t, sin_t, q_norm_w[l], k_norm_w[l], idx_k_norm_w[l])
        bias = indexer_mask(qi, ki, wi, B, S)
        y_attn = masked_attention(q, k, P_a, bias, B, S)
        y_gdn = gated_deltanet(P_b, Psm, conv_w[l], a_log[l], dt_bias[l], gdn_norm_w[l], B, S)
        mixed = gated_merge(y_attn, y_gdn, wpa, wpg, P_b, D)
        x1 = out_proj_residual(mixed, wo, xt)
        w_router = jnp.zeros((D, LANES), jnp.float32)
        w_router = w_router.at[:, :N_GROUPS].set(w_router_group[l])
        w_router = w_router.at[:, RT_EXP:RT_EXP + N_EXPERTS].set(
            w_router_expert[l].transpose(1, 0, 2).reshape(D, N_EXPERTS))
        b_router = jnp.zeros((1, LANES), jnp.float32)
        b_router = b_router.at[0, :N_GROUPS].set(b_router_group[l])
        b_router = b_router.at[0, RT_EXP:RT_EXP + N_EXPERTS].set(b_router_expert[l].reshape(N_EXPERTS))
        h2, logits = ffn_norm_router(x1, ffn_norm_w[l], w_router)
        eid_l, wt_l = routing(logits, b_router)
        dest_row, row_token, tile_expert, n_used = moe_dispatch_plan(eid_l[:, :TOP_K_EXPERTS], moe_tm)
        ys = moe_ffn(h2, tile_expert, n_used, row_token, wg, wu, wd, moe_tm)
        xt = moe_combine(x1, wt_l, ys, dest_row)
    return xt.reshape(B, S, D)
```

```python
import functools
import math

import jax
import jax.numpy as jnp
from jax import lax
from jax.experimental import pallas as pl
from jax.experimental.pallas import tpu as pltpu

ATT_HEADS = 16
ATT_KV_HEADS = 4
HEAD_DIM = 128
IDX_HEADS = 16
IDX_DIM = 64
TOPK_MAX = 256
ROPE_THETA = 10000.0
GDN_HEADS = 16
GDN_DK = 128
GDN_DV = 128
CONV_WIDTH = 4
CHUNK = 64
N_GROUPS = 4
EXPERTS_PER_GROUP = 8
N_EXPERTS = N_GROUPS * EXPERTS_PER_GROUP
TOP_K_EXPERTS = 2
EPS = 1e-6

ATT_Q_WIDTH = ATT_HEADS * HEAD_DIM
ATT_KV_WIDTH = ATT_KV_HEADS * HEAD_DIM
IDX_Q_WIDTH = IDX_HEADS * IDX_DIM
GDN_WIDTH = GDN_HEADS * GDN_DK

LANES = 128
BF16_SUBLANES = 16
VMEM_LIMIT = 56 * 1024 * 1024
NEG_BIG = -1e30
LOG2E = math.log2(math.e)
SCORE_MASKED = 3.0e38
BISECT_MAX_STEPS = 192
BISECT_UNROLL = 4
COUNT_ROWS = 128

OFF_AQ = 0
OFF_AK = OFF_AQ + ATT_Q_WIDTH
OFF_AV = OFF_AK + ATT_KV_WIDTH
OFF_IQ = OFF_AV + ATT_KV_WIDTH
OFF_GQ = OFF_IQ + IDX_Q_WIDTH
OFF_GK = OFF_GQ + GDN_WIDTH
OFF_GV = OFF_GK + GDN_WIDTH
OFF_GZ = OFF_GV + GDN_WIDTH
OFF_GATES = OFF_GZ + GDN_WIDTH
SM_IK = 0
SM_IW = SM_IK + IDX_DIM
SM_GA = SM_IW + IDX_HEADS
SM_GB = SM_GA + GDN_HEADS
RT_EXP = 8


def _cparams(sem):
    return pltpu.CompilerParams(dimension_semantics=sem, vmem_limit_bytes=VMEM_LIMIT)


def _tile(n, pref):
    t = min(n, pref)
    assert n % t == 0, (n, pref)
    return t


def _rmsnorm_kernel(x_ref, w_ref, o_ref):
    x = x_ref[...]
    ms = jnp.mean(x * x, axis=-1, keepdims=True)
    o_ref[...] = (x * lax.rsqrt(ms + EPS) * w_ref[...]).astype(o_ref.dtype)


def rmsnorm(x, w, out_dtype, tm=256):
    T, D = x.shape
    tm = _tile(T, tm)
    return pl.pallas_call(
        _rmsnorm_kernel,
        out_shape=jax.ShapeDtypeStruct((T, D), out_dtype),
        grid=(T // tm,),
        in_specs=[pl.BlockSpec((tm, D), lambda i: (i, 0)), pl.BlockSpec((1, D), lambda i: (0, 0))],
        out_specs=pl.BlockSpec((tm, D), lambda i: (i, 0)),
        compiler_params=_cparams(("parallel",)),
        name="rmsnorm",
    )(x, w.reshape(1, D))


def _matmul_kernel(a_ref, b_ref, o_ref):
    o_ref[...] = jnp.dot(a_ref[...], b_ref[...], preferred_element_type=jnp.float32).astype(o_ref.dtype)


def matmul(a, b, out_dtype, tm=1024, tn=1024, name="matmul"):
    M, K = a.shape
    _, N = b.shape
    tm, tn = _tile(M, tm), _tile(N, tn)
    return pl.pallas_call(
        _matmul_kernel,
        out_shape=jax.ShapeDtypeStruct((M, N), out_dtype),
        grid=(N // tn, M // tm),
        in_specs=[pl.BlockSpec((tm, K), lambda j, i: (i, 0)), pl.BlockSpec((K, tn), lambda j, i: (0, j))],
        out_specs=pl.BlockSpec((tm, tn), lambda j, i: (i, j)),
        compiler_params=_cparams(("parallel", "parallel")),
        name=name,
    )(a, b)


def _matmul_cast_kernel(a_ref, b_ref, *refs):
    ncast = (len(refs) - 1) // 2
    o_ref = refs[ncast]
    o_ref[...] = jnp.dot(a_ref[...], b_ref[...], preferred_element_type=jnp.float32).astype(o_ref.dtype)
    for x_ref, y_ref in zip(refs[:ncast], refs[ncast + 1:]):
        y_ref[...] = x_ref[...].astype(y_ref.dtype)


def _cast_block_rows(rows, nsteps):
    for rb in range(BF16_SUBLANES, rows + 1, BF16_SUBLANES):
        if rows % rb == 0 and rows // rb <= nsteps:
            return rb
    raise ValueError((rows, nsteps))


def matmul_with_casts(a, b, out_dtype, to_cast, tm=1024, tn=1024, name="matmul_casts"):
    M, K = a.shape
    _, N = b.shape
    tm, tn = _tile(M, tm), _tile(N, tn)
    ni = M // tm
    nsteps = (N // tn) * ni
    cast_in, cast_out, cast_shapes = [], [], []
    for x in to_cast:
        rows, cols = x.shape
        rb = _cast_block_rows(rows, nsteps)
        last = rows // rb - 1
        imap = functools.partial(lambda j, i, last: (jnp.minimum(j * ni + i, last), 0), last=last)
        cast_in.append(pl.BlockSpec((rb, cols), imap))
        cast_out.append(pl.BlockSpec((rb, cols), imap))
        cast_shapes.append(jax.ShapeDtypeStruct((rows, cols), jnp.bfloat16))
    outs = pl.pallas_call(
        _matmul_cast_kernel,
        out_shape=[jax.ShapeDtypeStruct((M, N), out_dtype)] + cast_shapes,
        grid=(N // tn, ni),
        in_specs=[pl.BlockSpec((tm, K), lambda j, i: (i, 0)), pl.BlockSpec((K, tn), lambda j, i: (0, j))] + cast_in,
        out_specs=[pl.BlockSpec((tm, tn), lambda j, i: (i, j))] + cast_out,
        compiler_params=_cparams(("arbitrary", "arbitrary")),
        name=name,
    )(a, b, *to_cast)
    return outs[0], outs[1:]


def _rope_table_kernel(pos_ref, inv_ref, sgn_ref, cos_ref, sin_ref):
    ang = pos_ref[...] * inv_ref[...]
    cos_ref[...] = jnp.cos(ang)
    sin_ref[...] = jnp.sin(ang) * sgn_ref[...]


def rope_tables(positions):
    T = positions.size
    pos = positions.reshape(T, 1).astype(jnp.float32)

    def inv(d):
        return ROPE_THETA ** (-jnp.arange(0, d, 2, dtype=jnp.float32) / d)

    i128, i64 = inv(HEAD_DIM), inv(IDX_DIM)
    inv_row = jnp.concatenate([i128, i128, i64, i64, i64, i64]).reshape(1, 2 * LANES)
    s128 = jnp.concatenate([-jnp.ones(HEAD_DIM // 2), jnp.ones(HEAD_DIM // 2)])
    s64 = jnp.concatenate([-jnp.ones(IDX_DIM // 2), jnp.ones(IDX_DIM // 2)])
    sgn_row = jnp.concatenate([s128, s64, s64]).astype(jnp.float32).reshape(1, 2 * LANES)
    tm = _tile(T, 512)
    spec = pl.BlockSpec((tm, 2 * LANES), lambda i: (i, 0))
    row = pl.BlockSpec((1, 2 * LANES), lambda i: (0, 0))
    return pl.pallas_call(
        _rope_table_kernel,
        out_shape=[jax.ShapeDtypeStruct((T, 2 * LANES), jnp.float32)] * 2,
        grid=(T // tm,),
        in_specs=[pl.BlockSpec((tm, 1), lambda i: (i, 0)), row, row],
        out_specs=[spec, spec],
        compiler_params=_cparams(("parallel",)),
        name="rope_tables",
    )(pos, inv_row, sgn_row)


def _rope128(x, cos, sin_signed):
    return x * cos + pltpu.roll(x, HEAD_DIM // 2, 1) * sin_signed


def _rope64x2(x, cos, sin_signed, lane):
    half = IDX_DIM // 2
    first = (lane % IDX_DIM) < half
    partner = jnp.where(first, pltpu.roll(x, LANES - half, 1), pltpu.roll(x, half, 1))
    return x * cos + partner * sin_signed


def _attn_prep_kernel(aq_ref, ak_ref, iq_ref, sm_ref, cos_ref, sin_ref, qw_ref, kw_ref, ikw_ref,
                      q_ref, k_ref, qi_ref, ki_ref, wi_ref):
    cos_a, sin_a = cos_ref[:, :LANES], sin_ref[:, :LANES]
    cos_i, sin_i = cos_ref[:, LANES:], sin_ref[:, LANES:]
    tm = cos_a.shape[0]
    lane = lax.broadcasted_iota(jnp.int32, (tm, LANES), 1)

    def head_norm(xh, w):
        ms = jnp.mean(xh * xh, axis=-1, keepdims=True)
        return xh * lax.rsqrt(ms + EPS) * w

    for h in range(ATT_HEADS):
        sl = slice(h * HEAD_DIM, (h + 1) * HEAD_DIM)
        xh = head_norm(aq_ref[:, sl].astype(jnp.float32), qw_ref[...])
        q_ref[:, sl] = (_rope128(xh, cos_a, sin_a) * (LOG2E * HEAD_DIM ** -0.5)).astype(q_ref.dtype)
    for h in range(ATT_KV_HEADS):
        sl = slice(h * HEAD_DIM, (h + 1) * HEAD_DIM)
        xh = head_norm(ak_ref[:, sl].astype(jnp.float32), kw_ref[...])
        k_ref[:, sl] = _rope128(xh, cos_a, sin_a).astype(k_ref.dtype)
    for p in range(IDX_Q_WIDTH // LANES):
        sl = slice(p * LANES, (p + 1) * LANES)
        xp = iq_ref[:, sl].astype(jnp.float32)
        qi_ref[:, sl] = (_rope64x2(xp, cos_i, sin_i, lane) * (IDX_DIM ** -0.5)).astype(qi_ref.dtype)
    sm = sm_ref[...]
    in_k = lane < IDX_DIM
    xk = jnp.where(in_k, sm, 0.0)
    ms = jnp.sum(xk * xk, axis=-1, keepdims=True) * (1.0 / IDX_DIM)
    kn = xk * lax.rsqrt(ms + EPS) * ikw_ref[...]
    kr = jnp.where(in_k, _rope64x2(kn, cos_i, sin_i, lane), 0.0)
    ki_ref[:, :LANES] = kr.astype(ki_ref.dtype)
    ki_ref[:, LANES:] = pltpu.roll(kr, IDX_DIM, 1).astype(ki_ref.dtype)
    wi_ref[...] = sm * (IDX_HEADS ** -0.5)


def attn_prep(P, Psm, cos_t, sin_t, q_norm_w, k_norm_w, idx_k_norm_w, tm=256):
    T = P.shape[0]
    tm = _tile(T, tm)
    ikw = jnp.concatenate([idx_k_norm_w, jnp.zeros((LANES - IDX_DIM,), jnp.float32)]).reshape(1, LANES)
    row = lambda w: pl.BlockSpec((1, w), lambda i: (0, 0))
    return pl.pallas_call(
        _attn_prep_kernel,
        out_shape=[jax.ShapeDtypeStruct((T, ATT_Q_WIDTH), jnp.bfloat16),
                   jax.ShapeDtypeStruct((T, ATT_KV_WIDTH), jnp.bfloat16),
                   jax.ShapeDtypeStruct((T, IDX_Q_WIDTH), jnp.bfloat16),
                   jax.ShapeDtypeStruct((T, 2 * LANES), jnp.bfloat16),
                   jax.ShapeDtypeStruct((T, LANES), jnp.float32)],
        grid=(T // tm,),
        in_specs=[pl.BlockSpec((tm, ATT_Q_WIDTH), lambda i: (i, OFF_AQ // ATT_Q_WIDTH)),
                  pl.BlockSpec((tm, ATT_KV_WIDTH), lambda i: (i, OFF_AK // ATT_KV_WIDTH)),
                  pl.BlockSpec((tm, IDX_Q_WIDTH), lambda i: (i, OFF_IQ // IDX_Q_WIDTH)),
                  pl.BlockSpec((tm, LANES), lambda i: (i, 0)),
                  pl.BlockSpec((tm, 2 * LANES), lambda i: (i, 0)),
                  pl.BlockSpec((tm, 2 * LANES), lambda i: (i, 0)),
                  row(LANES), row(LANES), row(LANES)],
        out_specs=[pl.BlockSpec((tm, ATT_Q_WIDTH), lambda i: (i, 0)),
                   pl.BlockSpec((tm, ATT_KV_WIDTH), lambda i: (i, 0)),
                   pl.BlockSpec((tm, IDX_Q_WIDTH), lambda i: (i, 0)),
                   pl.BlockSpec((tm, 2 * LANES), lambda i: (i, 0)),
                   pl.BlockSpec((tm, LANES), lambda i: (i, 0))],
        compiler_params=_cparams(("parallel",)),
        name="attn_prep",
    )(P, P, P, Psm, cos_t, sin_t, q_norm_w.reshape(1, LANES), k_norm_w.reshape(1, LANES), ikw)


def _lane_fold(x, op):
    s = x[:, :LANES]
    for j in range(1, x.shape[1] // LANES):
        s = op(s, x[:, j * LANES:(j + 1) * LANES])
    return s


def _indexer_kernel(qi_ref, ki_ref, wi_ref, bias_ref, key_ref, *, n_sel, tq, tk, nchunks):
    q0 = pl.program_id(1) * tq
    nck = (q0 + tq + tk - 1) // tk
    wi = wi_ref[...]
    rowpos = q0 + lax.broadcasted_iota(jnp.int32, (tq, tk), 0)
    colpos0 = lax.broadcasted_iota(jnp.int32, (tq, tk), 1)

    def score_chunk(c, carry):
        mx, mn = carry
        ks = ki_ref[pl.ds(pl.multiple_of(c * tk, tk), tk), :]
        acc = jnp.zeros((tq, tk), jnp.float32)
        for h in range(IDX_HEADS):
            qp = qi_ref[:, (h // 2) * LANES:(h // 2 + 1) * LANES]
            kh = ks[:, (h % 2) * LANES:(h % 2 + 1) * LANES]
            d = lax.dot_general(qp, kh, (((1,), (1,)), ((), ())), preferred_element_type=jnp.float32)
            acc = acc + wi[:, SM_IW + h:SM_IW + h + 1] * jnp.maximum(d, 0.0)
        causal = colpos0 + c * tk <= rowpos
        key_ref[c] = jnp.where(causal, acc, -SCORE_MASKED)
        mx = jnp.maximum(mx, _lane_fold(jnp.where(causal, acc, -SCORE_MASKED), jnp.maximum))
        mn = jnp.minimum(mn, _lane_fold(jnp.where(causal, acc, SCORE_MASKED), jnp.minimum))
        return mx, mn

    mx, mn = lax.fori_loop(0, nck, score_chunk, (jnp.full((tq, LANES), -SCORE_MASKED, jnp.float32),
                                                 jnp.full((tq, LANES), SCORE_MASKED, jnp.float32)))

    ones = jnp.ones((LANES, LANES), jnp.bfloat16)

    def count_ge(probe):
        accs = []
        for r0 in range(0, tq, COUNT_ROWS):
            rows = slice(r0, min(r0 + COUNT_ROWS, tq))

            def body(c, acc, rows=rows):
                for j in range(tk // LANES):
                    acc = acc + jnp.where(key_ref[c, rows, j * LANES:(j + 1) * LANES] >= probe[rows], 1.0, 0.0)
                return acc

            accs.append(lax.fori_loop(0, nck, body, jnp.zeros((rows.stop - rows.start, LANES), jnp.float32)))
        acc = accs[0] if len(accs) == 1 else jnp.concatenate(accs, axis=0)
        return jnp.dot(acc.astype(jnp.bfloat16), ones, preferred_element_type=jnp.float32)

    lo = jnp.broadcast_to(jnp.min(mn, axis=1, keepdims=True), (tq, LANES))
    hi = jnp.broadcast_to(jnp.max(mx, axis=1, keepdims=True), (tq, LANES))
    ncausal = q0 + lax.broadcasted_iota(jnp.int32, (tq, LANES), 0) + 1
    hi = jnp.where(ncausal <= n_sel, lo, hi)

    def unfinished(carry):
        return (carry[0] < BISECT_MAX_STEPS) & (carry[3] > 0)

    def bisect(carry):
        it, lo, hi, _ = carry
        for _ in range(BISECT_UNROLL):
            mid = 0.5 * lo + 0.5 * hi
            cnt = count_ge(mid)
            ok = cnt >= n_sel
            open_row = (mid > lo) & (mid < hi)
            lo = jnp.where(ok, mid, lo)
            hi = jnp.where(cnt == n_sel, mid, jnp.where(ok, hi, mid))
            open_row = open_row & (hi > lo)
        return it + BISECT_UNROLL, lo, hi, jnp.sum(jnp.where(open_row, 1, 0))

    _, thr, _, _ = lax.while_loop(unfinished, bisect, (jnp.int32(0), lo, hi, jnp.int32(1)))

    def write_chunk(c, carry):
        for j in range(tk // LANES):
            cols = slice(j * LANES, (j + 1) * LANES)
            bias_ref[c, :, cols] = jnp.where(key_ref[c, :, cols] >= thr, 0.0, NEG_BIG).astype(bias_ref.dtype)
        return carry

    def write_masked(c, carry):
        bias_ref[c] = jnp.full((tq, tk), NEG_BIG, bias_ref.dtype)
        return carry

    lax.fori_loop(0, nck, write_chunk, 0)
    lax.fori_loop(nck, nchunks, write_masked, 0)


def indexer_mask(qi, ki, wi, B, S, tq=512, tk=512):
    tq, tk = _tile(S, tq), _tile(S, tk)
    n_sel = min(TOPK_MAX, S // 4)
    nq, nchunks = S // tq, S // tk
    return pl.pallas_call(
        functools.partial(_indexer_kernel, n_sel=n_sel, tq=tq, tk=tk, nchunks=nchunks),
        out_shape=jax.ShapeDtypeStruct((B, nq, nchunks, tq, tk), jnp.bfloat16),
        grid=(B, nq),
        in_specs=[pl.BlockSpec((tq, IDX_Q_WIDTH), lambda b, i: (b * nq + i, 0)),
                  pl.BlockSpec((S, 2 * LANES), lambda b, i: (b, 0)),
                  pl.BlockSpec((tq, LANES), lambda b, i: (b * nq + i, 0))],
        out_specs=pl.BlockSpec((None, None, nchunks, tq, tk), lambda b, i: (b, i, 0, 0, 0)),
        scratch_shapes=[pltpu.VMEM((nchunks, tq, tk), jnp.float32)],
        compiler_params=_cparams(("parallel", "parallel")),
        name="indexer_mask",
    )(qi, ki, wi)


def _attn_kernel(q_ref, k_ref, v_ref, bias_ref, o_ref, m_ref, acc_ref, *, tq, tk):
    i, j = pl.program_id(1), pl.program_id(2)
    grp = ATT_HEADS // ATT_KV_HEADS

    @pl.when(j == 0)
    def _():
        m_ref[...] = jnp.full(m_ref.shape, -1e38, jnp.float32)
        acc_ref[...] = jnp.zeros(acc_ref.shape, jnp.float32)

    @pl.when(j * tk <= i * tq + tq - 1)
    def _():
        bias = bias_ref[...].astype(jnp.float32)
        ones = jnp.ones((tk, LANES), v_ref.dtype)
        v1 = [jnp.concatenate([v_ref[:, g * HEAD_DIM:(g + 1) * HEAD_DIM], ones], axis=1) for g in range(ATT_KV_HEADS)]
        s, m_new, alpha, p = {}, {}, {}, {}

        def scores(h):
            g = h // grp
            qh = q_ref[:, h * HEAD_DIM:(h + 1) * HEAD_DIM]
            kh = k_ref[:, g * HEAD_DIM:(g + 1) * HEAD_DIM]
            s[h] = lax.dot_general(qh, kh, (((1,), (1,)), ((), ())), preferred_element_type=jnp.float32) + bias
            m_prev = m_ref[h]
            m_new[h] = jnp.maximum(m_prev, jnp.max(s[h], axis=1, keepdims=True))
            alpha[h] = jnp.exp2(m_prev - m_new[h])
            m_ref[h] = m_new[h]

        def probs(h):
            p[h] = jnp.exp2(s.pop(h) - m_new.pop(h)[:, :1]).astype(v_ref.dtype)

        def values(h):
            a2 = jnp.concatenate([alpha[h], alpha.pop(h)], axis=1)
            acc_ref[h] = a2 * acc_ref[h] + jnp.dot(p.pop(h), v1[h // grp], preferred_element_type=jnp.float32)

        for t in range(ATT_HEADS + 2):
            if t < ATT_HEADS:
                scores(t)
            if 0 <= t - 1 < ATT_HEADS:
                probs(t - 1)
            if 0 <= t - 2 < ATT_HEADS:
                values(t - 2)

    @pl.when(j == pl.num_programs(2) - 1)
    def _():
        for h in range(ATT_HEADS):
            acc = acc_ref[h]
            o_ref[:, h * HEAD_DIM:(h + 1) * HEAD_DIM] = (acc[:, :HEAD_DIM] / acc[:, HEAD_DIM:]).astype(o_ref.dtype)


def masked_attention(q, k, P, bias, B, S):
    _, nq, nkv, tq, tk = bias.shape

    def kv_idx(i, j):
        return jnp.minimum(j, (i * tq + tq - 1) // tk)

    return pl.pallas_call(
        functools.partial(_attn_kernel, tq=tq, tk=tk),
        out_shape=jax.ShapeDtypeStruct((B * S, ATT_Q_WIDTH), jnp.bfloat16),
        grid=(B, nq, nkv),
        in_specs=[pl.BlockSpec((tq, ATT_Q_WIDTH), lambda b, i, j: (b * nq + i, 0)),
                  pl.BlockSpec((tk, ATT_KV_WIDTH), lambda b, i, j: (b * nkv + kv_idx(i, j), 0)),
                  pl.BlockSpec((tk, ATT_KV_WIDTH), lambda b, i, j: (b * nkv + kv_idx(i, j), OFF_AV // ATT_KV_WIDTH)),
                  pl.BlockSpec((None, None, None, tq, tk), lambda b, i, j: (b, i, kv_idx(i, j), 0, 0))],
        out_specs=pl.BlockSpec((tq, ATT_Q_WIDTH), lambda b, i, j: (b * nq + i, 0)),
        scratch_shapes=[pltpu.VMEM((ATT_HEADS, tq, LANES), jnp.float32),
                        pltpu.VMEM((ATT_HEADS, tq, HEAD_DIM + LANES), jnp.float32)],
        compiler_params=_cparams(("parallel", "parallel", "arbitrary")),
        name="masked_attention",
    )(q, k, P, bias)


GDN_HALO = BF16_SUBLANES


def _gdn_kernel(xq_ref, xk_ref, xv_ref, z_ref, sm_ref, cw_ref, alog_ref, dtb_ref, nw_ref, o_ref, state_ref, prev_ref):
    C = CHUNK
    f32, bf16 = jnp.float32, jnp.bfloat16
    heads = range(GDN_HEADS)

    @pl.when(pl.program_id(1) == 0)
    def _():
        state_ref[...] = jnp.zeros(state_ref.shape, f32)
        prev_ref[...] = jnp.zeros(prev_ref.shape, prev_ref.dtype)

    srow = lax.broadcasted_iota(jnp.int32, (CONV_WIDTH * C, GDN_HALO + C), 0)
    scol = lax.broadcasted_iota(jnp.int32, (CONV_WIDTH * C, GDN_HALO + C), 1)
    select = (scol == (srow & (C - 1)) + (srow >> (C.bit_length() - 1)) + (GDN_HALO - (CONV_WIDTH - 1))).astype(bf16)

    def conv_silu(x_ref, part):
        x = x_ref[...]
        xp = jnp.concatenate([prev_ref[part], x], axis=0)
        prev_ref[part] = x[C - GDN_HALO:]
        taps = jnp.dot(select, xp, preferred_element_type=f32)
        w = cw_ref[:, part * GDN_WIDTH:(part + 1) * GDN_WIDTH]
        y = None
        for jj in range(CONV_WIDTH):
            term = w[jj:jj + 1, :] * taps[jj * C:(jj + 1) * C, :]
            y = term if y is None else y + term
        return y * jax.nn.sigmoid(y)

    def l2(yh):
        return yh * lax.rsqrt(jnp.sum(yh * yh, axis=-1, keepdims=True) + EPS)

    yq, yk, yv = conv_silu(xq_ref, 0), conv_silu(xk_ref, 1), conv_silu(xv_ref, 2)
    sm = sm_ref[...]
    a_in = sm + dtb_ref[...]
    softplus = jnp.maximum(a_in, 0.0) + jnp.log(1.0 + jnp.exp(-jnp.abs(a_in)))
    g_all = -jnp.exp(alog_ref[...]) * softplus
    beta_all = jax.nn.sigmoid(sm)

    row = lax.broadcasted_iota(jnp.int32, (C, C), 0)
    col = lax.broadcasted_iota(jnp.int32, (C, C), 1)
    tril, strict = row >= col, row > col
    eye = (row == col).astype(f32)
    hi = lax.Precision.HIGHEST
    gc_col = jnp.dot(tril.astype(f32), g_all, precision=hi, preferred_element_type=f32)
    gc_row = jnp.dot(g_all.T, (row <= col).astype(f32), precision=hi, preferred_element_type=f32)
    nt = (((1,), (1,)), ((), ()))
    dot = functools.partial(jnp.dot, preferred_element_type=f32)
    kq, a, intra, rhs, qd, kd, eglast = [], [], [], [], [], [], []
    for j in heads:
        sl = slice(j * GDN_DK, (j + 1) * GDN_DK)
        gc = gc_col[:, SM_GA + j:SM_GA + j + 1]
        glast = gc[C - 1:C, :]
        eg = jnp.exp(gc)
        beta = beta_all[:, SM_GB + j:SM_GB + j + 1]
        q = (l2(yq[:, sl]) * (GDN_DK ** -0.5)).astype(bf16)
        k = l2(yk[:, sl]).astype(bf16)
        v = yv[:, sl].astype(bf16)
        kf = k.astype(f32)
        kq.append(jnp.concatenate([k, q], axis=0))
        rhs.append(jnp.concatenate([(v.astype(f32) * beta).astype(bf16), (kf * (beta * eg)).astype(bf16)], axis=1))
        qd.append((q.astype(f32) * eg).astype(bf16))
        kd.append((kf * jnp.exp(glast - gc)).astype(bf16))
        eglast.append(jnp.exp(glast))
    skq = [lax.dot_general(kq[j], kq[j][:C], nt, preferred_element_type=f32) for j in heads]
    for j in heads:
        gc = gc_col[:, SM_GA + j:SM_GA + j + 1]
        decay = jnp.exp(jnp.where(tril, gc - gc_row[SM_GA + j:SM_GA + j + 1, :], NEG_BIG))
        a.append(jnp.where(strict, skq[j][:C] * beta_all[:, SM_GB + j:SM_GB + j + 1] * decay, 0.0))
        intra.append((skq[j][C:] * decay).astype(bf16))
    ab = [a[j].astype(bf16) for j in heads]
    xb = [dot(ab[j], ab[j]).astype(bf16) for j in heads]
    tinv = [eye - a[j] for j in heads]
    for it in range(5):
        last = it == 4
        lhs = [tinv[j].astype(bf16) if last else jnp.concatenate([tinv[j].astype(bf16), xb[j]], axis=0) for j in heads]
        prod = [dot(lhs[j], xb[j]) for j in heads]
        tinv = [tinv[j] + prod[j][:C] for j in heads]
        if not last:
            xb = [prod[j][C:].astype(bf16) for j in heads]
    uw = [dot(tinv[j].astype(bf16), rhs[j]) for j in heads]
    sb = [state_ref[j].astype(bf16) for j in heads]
    ws = [dot(jnp.concatenate([uw[j][:, GDN_DV:].astype(bf16), qd[j]], axis=0), sb[j]) for j in heads]
    vb = [(uw[j][:, :GDN_DV] - ws[j][:C]).astype(bf16) for j in heads]
    o = [ws[j][C:] + dot(intra[j], vb[j]) for j in heads]
    upd = [lax.dot_general(kd[j], vb[j], (((0,), (0,)), ((), ())), preferred_element_type=f32) for j in heads]
    for j in heads:
        sl = slice(j * GDN_DK, (j + 1) * GDN_DK)
        state_ref[j] = state_ref[j] * eglast[j] + upd[j]
        ms = jnp.mean(o[j] * o[j], axis=-1, keepdims=True)
        z = z_ref[:, sl].astype(f32)
        o_ref[:, sl] = (o[j] * lax.rsqrt(ms + EPS) * nw_ref[...] * (z * jax.nn.sigmoid(z))).astype(o_ref.dtype)


def gated_deltanet(P, Psm, conv_w, a_log, dt_bias, norm_w, B, S):
    T = B * S
    C, W = CHUNK, GDN_WIDTH
    n = S // C

    def pad_lane(v, off):
        return jnp.zeros((1, LANES), jnp.float32).at[0, off:off + v.shape[0]].set(v)

    blk = lambda off: pl.BlockSpec((C, W), lambda b, c: (b * n + c, off // W))
    row = lambda w: pl.BlockSpec((1, w), lambda b, c: (0, 0))
    return pl.pallas_call(
        _gdn_kernel,
        out_shape=jax.ShapeDtypeStruct((T, W), jnp.bfloat16),
        grid=(B, n),
        in_specs=[blk(OFF_GQ), blk(OFF_GK), blk(OFF_GV), blk(OFF_GZ),
                  pl.BlockSpec((C, LANES), lambda b, c: (b * n + c, 0)),
                  pl.BlockSpec((CONV_WIDTH, 3 * W), lambda b, c: (0, 0)),
                  row(LANES), row(LANES), row(GDN_DV)],
        out_specs=blk(0),
        scratch_shapes=[pltpu.VMEM((GDN_HEADS, GDN_DK, GDN_DV), jnp.float32),
                        pltpu.VMEM((3, GDN_HALO, W), jnp.bfloat16)],
        compiler_params=_cparams(("parallel", "arbitrary")),
        name="gated_deltanet",
    )(P, P, P, P, Psm, conv_w, pad_lane(a_log, SM_GA), pad_lane(dt_bias, SM_GA), norm_w.reshape(1, GDN_DV))


def _merge_kernel(ya_ref, yg_ref, wa_ref, wg_ref, ga_ref, gg_ref, o_ref):
    pa = jnp.dot(ya_ref[...], wa_ref[...], preferred_element_type=jnp.float32)
    pg = jnp.dot(yg_ref[...], wg_ref[...], preferred_element_type=jnp.float32)
    ga = jax.nn.sigmoid(ga_ref[...].astype(jnp.float32))
    gg = jax.nn.sigmoid(gg_ref[...].astype(jnp.float32))
    o_ref[...] = (ga * pa + gg * pg).astype(o_ref.dtype)


def gated_merge(ya, yg, wa, wg, P, D, tm=1024, tn=512):
    T = ya.shape[0]
    tm, tn = _tile(T, tm), _tile(D, tn)
    goff = OFF_GATES // tn
    return pl.pallas_call(
        _merge_kernel,
        out_shape=jax.ShapeDtypeStruct((T, D), jnp.bfloat16),
        grid=(D // tn, T // tm),
        in_specs=[pl.BlockSpec((tm, ATT_Q_WIDTH), lambda j, i: (i, 0)),
                  pl.BlockSpec((tm, GDN_WIDTH), lambda j, i: (i, 0)),
                  pl.BlockSpec((ATT_Q_WIDTH, tn), lambda j, i: (0, j)),
                  pl.BlockSpec((GDN_WIDTH, tn), lambda j, i: (0, j)),
                  pl.BlockSpec((tm, tn), lambda j, i: (i, goff + j)),
                  pl.BlockSpec((tm, tn), lambda j, i: (i, goff + D // tn + j))],
        out_specs=pl.BlockSpec((tm, tn), lambda j, i: (i, j)),
        compiler_params=_cparams(("parallel", "parallel")),
        name="gated_merge",
    )(ya, yg, wa, wg, P, P)


def _outproj_kernel(a_ref, w_ref, x_ref, o_ref):
    o_ref[...] = x_ref[...] + jnp.dot(a_ref[...], w_ref[...], preferred_element_type=jnp.float32)


def out_proj_residual(a, w, x, tm=1024, tn=512):
    T, D = x.shape
    K = a.shape[1]
    tm, tn = _tile(T, tm), _tile(D, tn)
    return pl.pallas_call(
        _outproj_kernel,
        out_shape=jax.ShapeDtypeStruct((T, D), jnp.float32),
        grid=(D // tn, T // tm),
        in_specs=[pl.BlockSpec((tm, K), lambda j, i: (i, 0)),
                  pl.BlockSpec((K, tn), lambda j, i: (0, j)),
                  pl.BlockSpec((tm, tn), lambda j, i: (i, j))],
        out_specs=pl.BlockSpec((tm, tn), lambda j, i: (i, j)),
        compiler_params=_cparams(("parallel", "parallel")),
        name="out_proj_residual",
    )(a, w, x)


def _ffn_norm_router_kernel(x_ref, w_ref, rhi_ref, rlo_ref, h_ref, lg_ref):
    x = x_ref[...]
    ms = jnp.mean(x * x, axis=-1, keepdims=True)
    h = x * lax.rsqrt(ms + EPS) * w_ref[...]
    h_ref[...] = h
    hh = h.astype(jnp.bfloat16)
    hl = (h - hh.astype(jnp.float32)).astype(jnp.bfloat16)
    f32 = jnp.float32
    lg_ref[...] = (jnp.dot(hh, rhi_ref[...], preferred_element_type=f32)
                   + jnp.dot(hh, rlo_ref[...], preferred_element_type=f32)
                   + jnp.dot(hl, rhi_ref[...], preferred_element_type=f32))


def ffn_norm_router(x1, norm_w, w_router, tm=256):
    T, D = x1.shape
    tm = _tile(T, tm)
    rhi = w_router.astype(jnp.bfloat16)
    rlo = (w_router - rhi.astype(jnp.float32)).astype(jnp.bfloat16)
    return pl.pallas_call(
        _ffn_norm_router_kernel,
        out_shape=[jax.ShapeDtypeStruct((T, D), jnp.float32), jax.ShapeDtypeStruct((T, LANES), jnp.float32)],
        grid=(T // tm,),
        in_specs=[pl.BlockSpec((tm, D), lambda i: (i, 0)), pl.BlockSpec((1, D), lambda i: (0, 0)),
                  pl.BlockSpec((D, LANES), lambda i: (0, 0)), pl.BlockSpec((D, LANES), lambda i: (0, 0))],
        out_specs=[pl.BlockSpec((tm, D), lambda i: (i, 0)), pl.BlockSpec((tm, LANES), lambda i: (i, 0))],
        compiler_params=_cparams(("parallel",)),
        name="ffn_norm_router",
    )(x1, norm_w.reshape(1, D), rhi, rlo)


def _routing_kernel(lg_ref, b_ref, eid_ref, wt_ref):
    lg = lg_ref[...] + b_ref[...]
    lane = lax.broadcasted_iota(jnp.int32, lg.shape, 1)
    ninf = -jnp.inf

    def first_argmax(vals, vmax):
        return jnp.min(jnp.where(vals == vmax, lane, LANES), axis=-1, keepdims=True)

    glog = jnp.where(lane < N_GROUPS, lg, ninf)
    gmax = jnp.max(glog, axis=-1, keepdims=True)
    p_grp = 1.0 / jnp.sum(jnp.exp(glog - gmax), axis=-1, keepdims=True)
    grp = first_argmax(glog, gmax)
    base = RT_EXP + grp * EXPERTS_PER_GROUP
    elog = jnp.where((lane >= base) & (lane < base + EXPERTS_PER_GROUP), lg, ninf)
    emax = jnp.max(elog, axis=-1, keepdims=True)
    idx1 = first_argmax(elog, emax)
    elog2 = jnp.where(lane == idx1, ninf, elog)
    emax2 = jnp.max(elog2, axis=-1, keepdims=True)
    idx2 = first_argmax(elog2, emax2)
    e2 = jnp.exp(emax2 - emax)
    w1 = p_grp / (1.0 + e2)
    w2 = p_grp * e2 / (1.0 + e2)
    eid_ref[...] = jnp.where(lane == 0, idx1 - RT_EXP, jnp.where(lane == 1, idx2 - RT_EXP, 0))
    wt_ref[...] = jnp.where(lane == 0, w1, jnp.where(lane == 1, w2, 0.0))


def routing(logits, bias_row, tm=512):
    T = logits.shape[0]
    tm = _tile(T, tm)
    spec = pl.BlockSpec((tm, LANES), lambda i: (i, 0))
    return pl.pallas_call(
        _routing_kernel,
        out_shape=[jax.ShapeDtypeStruct((T, LANES), jnp.int32), jax.ShapeDtypeStruct((T, LANES), jnp.float32)],
        grid=(T // tm,),
        in_specs=[spec, pl.BlockSpec((1, LANES), lambda i: (0, 0))],
        out_specs=[spec, spec],
        compiler_params=_cparams(("parallel",)),
        name="routing",
    )(logits, bias_row)


GATHER_UNROLL = 8


def _row_gather_copy(src_hbm, row, dst, r, sem):
    return pltpu.make_async_copy(src_hbm.at[pl.ds(row, 1), :], dst.at[pl.ds(r, 1), :], sem)


def _moe_ffn_kernel(texp_ref, nused_ref, rowtok_ref, h_hbm, wg_ref, wu_ref, wd_ref, o_ref,
                    xbuf, sem, *, tm):
    i = pl.program_id(0)
    nused = nused_ref[0]

    def start_gather(tile, slot):
        def body(r, carry):
            _row_gather_copy(h_hbm, rowtok_ref[tile * tm + r], xbuf.at[slot], r, sem.at[slot]).start()
            return carry
        lax.fori_loop(0, tm, body, 0, unroll=GATHER_UNROLL)

    def wait_gather(slot):
        pltpu.make_async_copy(h_hbm.at[pl.ds(0, tm), :], xbuf.at[slot], sem.at[slot]).wait()

    @pl.when((i == 0) & (nused > 0))
    def _():
        start_gather(0, 0)

    @pl.when(i + 1 < nused)
    def _():
        start_gather(i + 1, (i + 1) % 2)

    @pl.when(i < nused)
    def _():
        slot = i % 2
        wait_gather(slot)
        x = xbuf[slot].astype(jnp.bfloat16)
        g = jnp.dot(x, wg_ref[0], preferred_element_type=jnp.float32)
        u = jnp.dot(x, wu_ref[0], preferred_element_type=jnp.float32)
        hmid = (g * jax.nn.sigmoid(g) * u).astype(jnp.bfloat16)
        y = jnp.dot(hmid, wd_ref[0], preferred_element_type=jnp.float32)
        o_ref[...] = y

    @pl.when(i >= nused)
    def _():
        o_ref[...] = jnp.zeros(o_ref.shape, o_ref.dtype)


def moe_ffn(h2, tile_expert, n_used, row_token, wg, wu, wd, tm):
    T, D = h2.shape
    R = row_token.shape[0]
    FF = wg.shape[2]
    ntiles = R // tm
    grid_spec = pltpu.PrefetchScalarGridSpec(
        num_scalar_prefetch=3,
        grid=(ntiles,),
        in_specs=[pl.BlockSpec(memory_space=pl.ANY),
                  pl.BlockSpec((1, D, FF), lambda i, te, nu, rt: (te[i], 0, 0)),
                  pl.BlockSpec((1, D, FF), lambda i, te, nu, rt: (te[i], 0, 0)),
                  pl.BlockSpec((1, FF, D), lambda i, te, nu, rt: (te[i], 0, 0))],
        out_specs=pl.BlockSpec((tm, D), lambda i, te, nu, rt: (i, 0)),
        scratch_shapes=[pltpu.VMEM((2, tm, D), jnp.float32), pltpu.SemaphoreType.DMA((2,))],
    )
    return pl.pallas_call(
        functools.partial(_moe_ffn_kernel, tm=tm),
        out_shape=jax.ShapeDtypeStruct((R, D), jnp.float32),
        grid_spec=grid_spec,
        compiler_params=_cparams(("arbitrary",)),
        name="moe_ffn",
    )(tile_expert, n_used, row_token, h2, wg, wu, wd)


def _combine_kernel(dest_ref, x_ref, wt_ref, ys_hbm, o_ref, buf, sem, *, tm):
    i = pl.program_id(0)
    n = pl.num_programs(0)

    def start_gather(tile, slot):
        def body(r, carry):
            for kk in range(TOP_K_EXPERTS):
                row = dest_ref[(tile * tm + r) * TOP_K_EXPERTS + kk]
                _row_gather_copy(ys_hbm, row, buf.at[slot, kk], r, sem.at[slot]).start()
            return carry
        lax.fori_loop(0, tm, body, 0, unroll=GATHER_UNROLL)

    def wait_gather(slot):
        for kk in range(TOP_K_EXPERTS):
            pltpu.make_async_copy(ys_hbm.at[pl.ds(0, tm), :], buf.at[slot, kk], sem.at[slot]).wait()

    @pl.when(i == 0)
    def _():
        start_gather(0, 0)

    @pl.when(i + 1 < n)
    def _():
        start_gather(i + 1, (i + 1) % 2)

    slot = i % 2
    wait_gather(slot)
    wt = wt_ref[...]
    o_ref[...] = x_ref[...] + wt[:, 0:1] * buf[slot, 0] + wt[:, 1:2] * buf[slot, 1]


def moe_combine(x1, wt_lanes, ys, dest_row, tm=128):
    T, D = x1.shape
    tm = _tile(T, tm)
    grid_spec = pltpu.PrefetchScalarGridSpec(
        num_scalar_prefetch=1,
        grid=(T // tm,),
        in_specs=[pl.BlockSpec((tm, D), lambda i, d: (i, 0)), pl.BlockSpec((tm, LANES), lambda i, d: (i, 0)),
                  pl.BlockSpec(memory_space=pl.ANY)],
        out_specs=pl.BlockSpec((tm, D), lambda i, d: (i, 0)),
        scratch_shapes=[pltpu.VMEM((2, TOP_K_EXPERTS, tm, D), jnp.float32), pltpu.SemaphoreType.DMA((2,))],
    )
    return pl.pallas_call(
        functools.partial(_combine_kernel, tm=tm),
        out_shape=jax.ShapeDtypeStruct((T, D), jnp.float32),
        grid_spec=grid_spec,
        compiler_params=_cparams(("arbitrary",)),
        name="moe_combine",
    )(dest_row, x1, wt_lanes, ys)


PLAN_BLOCK = 128


def moe_dispatch_plan(eid, tm):
    T = eid.shape[0]
    A = T * TOP_K_EXPERTS
    e_flat = eid.reshape(A)
    blk = PLAN_BLOCK if A % PLAN_BLOCK == 0 else A
    onehot = (e_flat[:, None] == jnp.arange(N_EXPERTS, dtype=jnp.int32)[None, :]).astype(jnp.float32)
    oh = onehot.reshape(A // blk, blk, N_EXPERTS)
    before = jnp.tril(jnp.ones((blk, blk), jnp.float32), -1)
    within = jnp.einsum('ij,bje->bie', before, oh)
    totals = jnp.sum(oh, axis=1)
    offs = jnp.cumsum(totals, axis=0) - totals
    rank = jnp.sum((within + offs[:, None, :]) * oh, axis=2).reshape(A).astype(jnp.int32)
    counts = (offs[-1] + totals[-1]).astype(jnp.int32)
    padded = ((counts + tm - 1) // tm) * tm
    pend = jnp.cumsum(padded)
    pstart = pend - padded
    dest_row = (pstart[e_flat] + rank).astype(jnp.int32)
    R = ((A + N_EXPERTS * (tm - 1)) + tm - 1) // tm * tm
    row_token = jnp.zeros((R,), jnp.int32).at[dest_row].set(jnp.arange(A, dtype=jnp.int32) // TOP_K_EXPERTS)
    tile_start = jnp.arange(R // tm, dtype=jnp.int32) * tm
    tile_expert = jnp.minimum(jnp.sum(tile_start[:, None] >= pend[None, :], axis=1), N_EXPERTS - 1).astype(jnp.int32)
    n_used = (pend[-1] // tm).astype(jnp.int32).reshape(1)
    return dest_row, row_token, tile_expert, n_used


REGROUP_T = 512
SKIP_IDX = IDX_DIM + IDX_HEADS
SKIP_GDN = SKIP_IDX + 2 * GDN_HEADS


def _regroup_kernel(a_ref, b_ref, o_ref):
    j = pl.program_id(0)
    t = REGROUP_T

    def window(skip):
        both = jnp.concatenate([a_ref[...], b_ref[...]], axis=0)
        return both[skip:skip + t].T.astype(o_ref.dtype)

    @pl.when(j < OFF_GQ // t)
    def _():
        o_ref[...] = a_ref[...].T.astype(o_ref.dtype)

    @pl.when((j >= OFF_GQ // t) & (j < OFF_GATES // t))
    def _():
        o_ref[...] = window(SKIP_IDX)

    @pl.when(j >= OFF_GATES // t)
    def _():
        o_ref[...] = window(SKIP_GDN)


def _regroup_small_kernel(a_ref, b_ref, o_ref):
    rows = jnp.concatenate([a_ref[:SKIP_IDX], b_ref[SKIP_IDX:SKIP_GDN],
                            jnp.zeros((LANES - SKIP_GDN, a_ref.shape[1]), jnp.float32)], axis=0)
    o_ref[...] = rows.T.astype(o_ref.dtype)


def _split_w_in(w, D):
    t = REGROUP_T
    n_big = OFF_GATES + 2 * D
    assert n_big % t == 0 and D % t == 0 and w.shape[1] == n_big + SKIP_GDN and SKIP_GDN % 8 == 0 and SKIP_IDX % 8 == 0
    wt = w.T
    w_big = pl.pallas_call(
        _regroup_kernel,
        out_shape=jax.ShapeDtypeStruct((D, n_big), jnp.bfloat16),
        grid=(n_big // t, D // t),
        in_specs=[pl.BlockSpec((t, t), lambda j, k: (j, k)),
                  pl.BlockSpec((LANES, t), lambda j, k: ((j + 1) * (t // LANES), k))],
        out_specs=pl.BlockSpec((t, t), lambda j, k: (k, j)),
        compiler_params=_cparams(("parallel", "parallel")),
        name="regroup_w_in",
    )(wt, wt)
    assert OFF_GQ % LANES == 0 and OFF_GATES % LANES == 0
    w_small = pl.pallas_call(
        _regroup_small_kernel,
        out_shape=jax.ShapeDtypeStruct((D, LANES), jnp.bfloat16),
        grid=(D // t,),
        in_specs=[pl.BlockSpec((LANES, t), lambda k: (OFF_GQ // LANES, k)),
                  pl.BlockSpec((LANES, t), lambda k: (OFF_GATES // LANES, k))],
        out_specs=pl.BlockSpec((t, LANES), lambda k: (k, 0)),
        compiler_params=_cparams(("parallel",)),
        name="regroup_w_small",
    )(wt, wt)
    return w_big, w_small


def kernel(x, positions, mix_norm_w, w_in, q_norm_w, k_norm_w, idx_k_norm_w, conv_w, a_log, dt_bias, gdn_norm_w, w_proj_attn, w_proj_gdn, w_out, ffn_norm_w, w_router_group, b_router_group, w_router_expert, b_router_expert, w_gate, w_up, w_down):
    B, S, D = x.shape
    T = B * S
    bf16 = jnp.bfloat16
    moe_tm = 256 if T * TOP_K_EXPERTS >= 256 * N_EXPERTS else 64
    xt = x.reshape(T, D)
    cos_t, sin_t = rope_tables(positions)
    for l in range(w_in.shape[0]):
        w_big, w_small = _split_w_in(w_in[l], D)
        h = rmsnorm(xt, mix_norm_w[l], bf16)
        NE, _, FF = w_gate[l].shape
        later_weights = [w_proj_attn[l], w_proj_gdn[l], w_out[l], w_gate[l].reshape(NE * D, FF),
                         w_up[l].reshape(NE * D, FF), w_down[l].reshape(NE * FF, D)]
        P, (wpa, wpg, wo, wg, wu, wd) = matmul_with_casts(h, w_big, bf16, later_weights, name="in_proj")
        wg, wu, wd = wg.reshape(NE, D, FF), wu.reshape(NE, D, FF), wd.reshape(NE, FF, D)
        Psm = matmul(h, w_small, jnp.float32, name="in_proj_small")
        q, k, qi, ki, wi = attn_prep(P, Psm, cos_t, sin_t, q_norm_w[l], k_norm_w[l], idx_k_norm_w[l])
        bias = indexer_mask(qi, ki, wi, B, S)
        y_attn = masked_attention(q, k, P, bias, B, S)
        y_gdn = gated_deltanet(P, Psm, conv_w[l], a_log[l], dt_bias[l], gdn_norm_w[l], B, S)
        mixed = gated_merge(y_attn, y_gdn, wpa, wpg, P, D)
        x1 = out_proj_residual(mixed, wo, xt)
        w_router = jnp.zeros((D, LANES), jnp.float32)
        w_router = w_router.at[:, :N_GROUPS].set(w_router_group[l])
        w_router = w_router.at[:, RT_EXP:RT_EXP + N_EXPERTS].set(
            w_router_expert[l].transpose(1, 0, 2).reshape(D, N_EXPERTS))
        b_router = jnp.zeros((1, LANES), jnp.float32)
        b_router = b_router.at[0, :N_GROUPS].set(b_router_group[l])
        b_router = b_router.at[0, RT_EXP:RT_EXP + N_EXPERTS].set(b_router_expert[l].reshape(N_EXPERTS))
        h2, logits = ffn_norm_router(x1, ffn_norm_w[l], w_router)
        eid_l, wt_l = routing(logits, b_router)
        dest_row, row_token, tile_expert, n_used = moe_dispatch_plan(eid_l[:, :TOP_K_EXPERTS], moe_tm)
        ys = moe_ffn(h2, tile_expert, n_used, row_token, wg, wu, wd, moe_tm)
        xt = moe_combine(x1, wt_l, ys, dest_row)
    return xt.reshape(B, S, D)
```

```python
import functools
import math

import jax
import jax.numpy as jnp
from jax import lax
from jax.experimental import pallas as pl
from jax.experimental.pallas import tpu as pltpu

ATT_HEADS = 16
ATT_KV_HEADS = 4
HEAD_DIM = 128
IDX_HEADS = 16
IDX_DIM = 64
TOPK_MAX = 256
ROPE_THETA = 10000.0
GDN_HEADS = 16
GDN_DK = 128
GDN_DV = 128
CONV_WIDTH = 4
CHUNK = 64
N_GROUPS = 4
EXPERTS_PER_GROUP = 8
N_EXPERTS = N_GROUPS * EXPERTS_PER_GROUP
TOP_K_EXPERTS = 2
EPS = 1e-6

ATT_Q_WIDTH = ATT_HEADS * HEAD_DIM
ATT_KV_WIDTH = ATT_KV_HEADS * HEAD_DIM
IDX_Q_WIDTH = IDX_HEADS * IDX_DIM
GDN_WIDTH = GDN_HEADS * GDN_DK

LANES = 128
BF16_SUBLANES = 16
VMEM_LIMIT = 56 * 1024 * 1024
IN_PROJ_VMEM_LIMIT = 60 * 1024 * 1024
NEG_BIG = -1e30
LOG2E = math.log2(math.e)
SCORE_MASKED = 3.0e38
BISECT_MAX_STEPS = 192
BISECT_UNROLL = 4
COUNT_ROWS = 128

OFF_AQ = 0
OFF_AK = OFF_AQ + ATT_Q_WIDTH
OFF_AV = OFF_AK + ATT_KV_WIDTH
OFF_IQ = OFF_AV + ATT_KV_WIDTH
OFF_GQ = OFF_IQ + IDX_Q_WIDTH
OFF_GK = OFF_GQ + GDN_WIDTH
OFF_GV = OFF_GK + GDN_WIDTH
OFF_GZ = OFF_GV + GDN_WIDTH
OFF_GATES = OFF_GZ + GDN_WIDTH
SM_IK = 0
SM_IW = SM_IK + IDX_DIM
SM_GA = SM_IW + IDX_HEADS
SM_GB = SM_GA + GDN_HEADS
RT_EXP = 8


def _cparams(sem):
    return pltpu.CompilerParams(dimension_semantics=sem, vmem_limit_bytes=VMEM_LIMIT)


def _tile(n, pref):
    t = min(n, pref)
    assert n % t == 0, (n, pref)
    return t


def _rmsnorm_kernel(x_ref, w_ref, o_ref):
    x = x_ref[...]
    ms = jnp.mean(x * x, axis=-1, keepdims=True)
    o_ref[...] = (x * lax.rsqrt(ms + EPS) * w_ref[...]).astype(o_ref.dtype)


def rmsnorm(x, w, out_dtype, tm=256):
    T, D = x.shape
    tm = _tile(T, tm)
    return pl.pallas_call(
        _rmsnorm_kernel,
        out_shape=jax.ShapeDtypeStruct((T, D), out_dtype),
        grid=(T // tm,),
        in_specs=[pl.BlockSpec((tm, D), lambda i: (i, 0)), pl.BlockSpec((1, D), lambda i: (0, 0))],
        out_specs=pl.BlockSpec((tm, D), lambda i: (i, 0)),
        compiler_params=_cparams(("parallel",)),
        name="rmsnorm",
    )(x, w.reshape(1, D))


def _matmul_kernel(a_ref, b_ref, o_ref):
    o_ref[...] = jnp.dot(a_ref[...], b_ref[...], preferred_element_type=jnp.float32).astype(o_ref.dtype)


def matmul(a, b, out_dtype, tm=1024, tn=1024, name="matmul"):
    M, K = a.shape
    _, N = b.shape
    tm, tn = _tile(M, tm), _tile(N, tn)
    return pl.pallas_call(
        _matmul_kernel,
        out_shape=jax.ShapeDtypeStruct((M, N), out_dtype),
        grid=(N // tn, M // tm),
        in_specs=[pl.BlockSpec((tm, K), lambda j, i: (i, 0)), pl.BlockSpec((K, tn), lambda j, i: (0, j))],
        out_specs=pl.BlockSpec((tm, tn), lambda j, i: (i, j)),
        compiler_params=_cparams(("parallel", "parallel")),
        name=name,
    )(a, b)


def _cast_block_rows(rows, nsteps):
    for rb in range(BF16_SUBLANES, rows + 1, BF16_SUBLANES):
        if rows % rb == 0 and rows // rb <= nsteps:
            return rb
    raise ValueError((rows, nsteps))


def _rope_table_kernel(pos_ref, inv_ref, sgn_ref, cos_ref, sin_ref):
    ang = pos_ref[...] * inv_ref[...]
    cos_ref[...] = jnp.cos(ang)
    sin_ref[...] = jnp.sin(ang) * sgn_ref[...]


def rope_tables(positions):
    T = positions.size
    pos = positions.reshape(T, 1).astype(jnp.float32)

    def inv(d):
        return ROPE_THETA ** (-jnp.arange(0, d, 2, dtype=jnp.float32) / d)

    i128, i64 = inv(HEAD_DIM), inv(IDX_DIM)
    inv_row = jnp.concatenate([i128, i128, i64, i64, i64, i64]).reshape(1, 2 * LANES)
    s128 = jnp.concatenate([-jnp.ones(HEAD_DIM // 2), jnp.ones(HEAD_DIM // 2)])
    s64 = jnp.concatenate([-jnp.ones(IDX_DIM // 2), jnp.ones(IDX_DIM // 2)])
    sgn_row = jnp.concatenate([s128, s64, s64]).astype(jnp.float32).reshape(1, 2 * LANES)
    tm = _tile(T, 512)
    spec = pl.BlockSpec((tm, 2 * LANES), lambda i: (i, 0))
    row = pl.BlockSpec((1, 2 * LANES), lambda i: (0, 0))
    return pl.pallas_call(
        _rope_table_kernel,
        out_shape=[jax.ShapeDtypeStruct((T, 2 * LANES), jnp.float32)] * 2,
        grid=(T // tm,),
        in_specs=[pl.BlockSpec((tm, 1), lambda i: (i, 0)), row, row],
        out_specs=[spec, spec],
        compiler_params=_cparams(("parallel",)),
        name="rope_tables",
    )(pos, inv_row, sgn_row)


def _rope128(x, cos, sin_signed):
    return x * cos + pltpu.roll(x, HEAD_DIM // 2, 1) * sin_signed


def _rope64x2(x, cos, sin_signed, lane):
    half = IDX_DIM // 2
    first = (lane % IDX_DIM) < half
    partner = jnp.where(first, pltpu.roll(x, LANES - half, 1), pltpu.roll(x, half, 1))
    return x * cos + partner * sin_signed


def _attn_prep_kernel(aq_ref, ak_ref, iq_ref, sm_ref, cos_ref, sin_ref, qw_ref, kw_ref, ikw_ref,
                      q_ref, k_ref, qi_ref, ki_ref, wi_ref):
    cos_a, sin_a = cos_ref[:, :LANES], sin_ref[:, :LANES]
    cos_i, sin_i = cos_ref[:, LANES:], sin_ref[:, LANES:]
    tm = cos_a.shape[0]
    lane = lax.broadcasted_iota(jnp.int32, (tm, LANES), 1)

    def head_norm(xh, w):
        ms = jnp.mean(xh * xh, axis=-1, keepdims=True)
        return xh * lax.rsqrt(ms + EPS) * w

    for h in range(ATT_HEADS):
        sl = slice(h * HEAD_DIM, (h + 1) * HEAD_DIM)
        xh = head_norm(aq_ref[:, sl].astype(jnp.float32), qw_ref[...])
        q_ref[:, sl] = (_rope128(xh, cos_a, sin_a) * (LOG2E * HEAD_DIM ** -0.5)).astype(q_ref.dtype)
    for h in range(ATT_KV_HEADS):
        sl = slice(h * HEAD_DIM, (h + 1) * HEAD_DIM)
        xh = head_norm(ak_ref[:, sl].astype(jnp.float32), kw_ref[...])
        k_ref[:, sl] = _rope128(xh, cos_a, sin_a).astype(k_ref.dtype)
    for p in range(IDX_Q_WIDTH // LANES):
        sl = slice(p * LANES, (p + 1) * LANES)
        xp = iq_ref[:, sl].astype(jnp.float32)
        qi_ref[:, sl] = (_rope64x2(xp, cos_i, sin_i, lane) * (IDX_DIM ** -0.5)).astype(qi_ref.dtype)
    sm = sm_ref[...]
    in_k = lane < IDX_DIM
    xk = jnp.where(in_k, sm, 0.0)
    ms = jnp.sum(xk * xk, axis=-1, keepdims=True) * (1.0 / IDX_DIM)
    kn = xk * lax.rsqrt(ms + EPS) * ikw_ref[...]
    kr = jnp.where(in_k, _rope64x2(kn, cos_i, sin_i, lane), 0.0)
    ki_ref[:, :LANES] = kr.astype(ki_ref.dtype)
    ki_ref[:, LANES:] = pltpu.roll(kr, IDX_DIM, 1).astype(ki_ref.dtype)
    wi_ref[...] = sm * (IDX_HEADS ** -0.5)


def attn_prep(P, Psm, cos_t, sin_t, q_norm_w, k_norm_w, idx_k_norm_w, tm=256):
    T = P.shape[0]
    tm = _tile(T, tm)
    ikw = jnp.concatenate([idx_k_norm_w, jnp.zeros((LANES - IDX_DIM,), jnp.float32)]).reshape(1, LANES)
    row = lambda w: pl.BlockSpec((1, w), lambda i: (0, 0))
    return pl.pallas_call(
        _attn_prep_kernel,
        out_shape=[jax.ShapeDtypeStruct((T, ATT_Q_WIDTH), jnp.bfloat16),
                   jax.ShapeDtypeStruct((T, ATT_KV_WIDTH), jnp.bfloat16),
                   jax.ShapeDtypeStruct((T, IDX_Q_WIDTH), jnp.bfloat16),
                   jax.ShapeDtypeStruct((T, 2 * LANES), jnp.bfloat16),
                   jax.ShapeDtypeStruct((T, LANES), jnp.float32)],
        grid=(T // tm,),
        in_specs=[pl.BlockSpec((tm, ATT_Q_WIDTH), lambda i: (i, OFF_AQ // ATT_Q_WIDTH)),
                  pl.BlockSpec((tm, ATT_KV_WIDTH), lambda i: (i, OFF_AK // ATT_KV_WIDTH)),
                  pl.BlockSpec((tm, IDX_Q_WIDTH), lambda i: (i, OFF_IQ // IDX_Q_WIDTH)),
                  pl.BlockSpec((tm, LANES), lambda i: (i, 0)),
                  pl.BlockSpec((tm, 2 * LANES), lambda i: (i, 0)),
                  pl.BlockSpec((tm, 2 * LANES), lambda i: (i, 0)),
                  row(LANES), row(LANES), row(LANES)],
        out_specs=[pl.BlockSpec((tm, ATT_Q_WIDTH), lambda i: (i, 0)),
                   pl.BlockSpec((tm, ATT_KV_WIDTH), lambda i: (i, 0)),
                   pl.BlockSpec((tm, IDX_Q_WIDTH), lambda i: (i, 0)),
                   pl.BlockSpec((tm, 2 * LANES), lambda i: (i, 0)),
                   pl.BlockSpec((tm, LANES), lambda i: (i, 0))],
        compiler_params=_cparams(("parallel",)),
        name="attn_prep",
    )(P, P, P, Psm, cos_t, sin_t, q_norm_w.reshape(1, LANES), k_norm_w.reshape(1, LANES), ikw)


def _lane_fold(x, op):
    s = x[:, :LANES]
    for j in range(1, x.shape[1] // LANES):
        s = op(s, x[:, j * LANES:(j + 1) * LANES])
    return s


def _indexer_kernel(qi_ref, ki_ref, wi_ref, bias_ref, key_ref, *, n_sel, tq, tk, nchunks):
    q0 = pl.program_id(1) * tq
    nck = (q0 + tq + tk - 1) // tk
    wi = wi_ref[...]
    rowpos = q0 + lax.broadcasted_iota(jnp.int32, (tq, tk), 0)
    colpos0 = lax.broadcasted_iota(jnp.int32, (tq, tk), 1)

    def score_chunk(c, carry):
        mx, mn = carry
        ks = ki_ref[pl.ds(pl.multiple_of(c * tk, tk), tk), :]
        acc = jnp.zeros((tq, tk), jnp.float32)
        for h in range(IDX_HEADS):
            qp = qi_ref[:, (h // 2) * LANES:(h // 2 + 1) * LANES]
            kh = ks[:, (h % 2) * LANES:(h % 2 + 1) * LANES]
            d = lax.dot_general(qp, kh, (((1,), (1,)), ((), ())), preferred_element_type=jnp.float32)
            acc = acc + wi[:, SM_IW + h:SM_IW + h + 1] * jnp.maximum(d, 0.0)
        causal = colpos0 + c * tk <= rowpos
        key_ref[c] = jnp.where(causal, acc, -SCORE_MASKED)
        mx = jnp.maximum(mx, _lane_fold(jnp.where(causal, acc, -SCORE_MASKED), jnp.maximum))
        mn = jnp.minimum(mn, _lane_fold(jnp.where(causal, acc, SCORE_MASKED), jnp.minimum))
        return mx, mn

    mx, mn = lax.fori_loop(0, nck, score_chunk, (jnp.full((tq, LANES), -SCORE_MASKED, jnp.float32),
                                                 jnp.full((tq, LANES), SCORE_MASKED, jnp.float32)))

    ones = jnp.ones((LANES, LANES), jnp.bfloat16)

    def count_ge(probe):
        accs = []
        for r0 in range(0, tq, COUNT_ROWS):
            rows = slice(r0, min(r0 + COUNT_ROWS, tq))

            def body(c, acc, rows=rows):
                for j in range(tk // LANES):
                    acc = acc + jnp.where(key_ref[c, rows, j * LANES:(j + 1) * LANES] >= probe[rows], 1.0, 0.0)
                return acc

            accs.append(lax.fori_loop(0, nck, body, jnp.zeros((rows.stop - rows.start, LANES), jnp.float32)))
        acc = accs[0] if len(accs) == 1 else jnp.concatenate(accs, axis=0)
        return jnp.dot(acc.astype(jnp.bfloat16), ones, preferred_element_type=jnp.float32)

    lo = jnp.broadcast_to(jnp.min(mn, axis=1, keepdims=True), (tq, LANES))
    hi = jnp.broadcast_to(jnp.max(mx, axis=1, keepdims=True), (tq, LANES))
    ncausal = q0 + lax.broadcasted_iota(jnp.int32, (tq, LANES), 0) + 1
    hi = jnp.where(ncausal <= n_sel, lo, hi)

    def unfinished(carry):
        return (carry[0] < BISECT_MAX_STEPS) & (carry[3] > 0)

    def bisect(carry):
        it, lo, hi, _ = carry
        for _ in range(BISECT_UNROLL):
            mid = 0.5 * lo + 0.5 * hi
            cnt = count_ge(mid)
            ok = cnt >= n_sel
            open_row = (mid > lo) & (mid < hi)
            lo = jnp.where(ok, mid, lo)
            hi = jnp.where(cnt == n_sel, mid, jnp.where(ok, hi, mid))
            open_row = open_row & (hi > lo)
        return it + BISECT_UNROLL, lo, hi, jnp.sum(jnp.where(open_row, 1, 0))

    _, thr, _, _ = lax.while_loop(unfinished, bisect, (jnp.int32(0), lo, hi, jnp.int32(1)))

    def write_chunk(c, carry):
        for j in range(tk // LANES):
            cols = slice(j * LANES, (j + 1) * LANES)
            bias_ref[c, :, cols] = jnp.where(key_ref[c, :, cols] >= thr, 0.0, NEG_BIG).astype(bias_ref.dtype)
        return carry

    def write_masked(c, carry):
        bias_ref[c] = jnp.full((tq, tk), NEG_BIG, bias_ref.dtype)
        return carry

    lax.fori_loop(0, nck, write_chunk, 0)
    lax.fori_loop(nck, nchunks, write_masked, 0)


def indexer_mask(qi, ki, wi, B, S, tq=512, tk=512):
    tq, tk = _tile(S, tq), _tile(S, tk)
    n_sel = min(TOPK_MAX, S // 4)
    nq, nchunks = S // tq, S // tk
    return pl.pallas_call(
        functools.partial(_indexer_kernel, n_sel=n_sel, tq=tq, tk=tk, nchunks=nchunks),
        out_shape=jax.ShapeDtypeStruct((B, nq, nchunks, tq, tk), jnp.bfloat16),
        grid=(B, nq),
        in_specs=[pl.BlockSpec((tq, IDX_Q_WIDTH), lambda b, i: (b * nq + i, 0)),
                  pl.BlockSpec((S, 2 * LANES), lambda b, i: (b, 0)),
                  pl.BlockSpec((tq, LANES), lambda b, i: (b * nq + i, 0))],
        out_specs=pl.BlockSpec((None, None, nchunks, tq, tk), lambda b, i: (b, i, 0, 0, 0)),
        scratch_shapes=[pltpu.VMEM((nchunks, tq, tk), jnp.float32)],
        compiler_params=_cparams(("parallel", "parallel")),
        name="indexer_mask",
    )(qi, ki, wi)


def _attn_kernel(q_ref, k_ref, v_ref, bias_ref, o_ref, m_ref, acc_ref, *, tq, tk):
    i, j = pl.program_id(1), pl.program_id(2)
    grp = ATT_HEADS // ATT_KV_HEADS

    @pl.when(j == 0)
    def _():
        m_ref[...] = jnp.full(m_ref.shape, -1e38, jnp.float32)
        acc_ref[...] = jnp.zeros(acc_ref.shape, jnp.float32)

    @pl.when(j * tk <= i * tq + tq - 1)
    def _():
        bias = bias_ref[...].astype(jnp.float32)
        ones = jnp.ones((tk, LANES), v_ref.dtype)
        v1 = [jnp.concatenate([v_ref[:, g * HEAD_DIM:(g + 1) * HEAD_DIM], ones], axis=1) for g in range(ATT_KV_HEADS)]
        s, m_new, alpha, p = {}, {}, {}, {}

        def scores(h):
            g = h // grp
            qh = q_ref[:, h * HEAD_DIM:(h + 1) * HEAD_DIM]
            kh = k_ref[:, g * HEAD_DIM:(g + 1) * HEAD_DIM]
            s[h] = lax.dot_general(qh, kh, (((1,), (1,)), ((), ())), preferred_element_type=jnp.float32) + bias
            m_prev = m_ref[h]
            m_new[h] = jnp.maximum(m_prev, jnp.max(s[h], axis=1, keepdims=True))
            alpha[h] = jnp.exp2(m_prev - m_new[h])
            m_ref[h] = m_new[h]

        def probs(h):
            p[h] = jnp.exp2(s.pop(h) - m_new.pop(h)[:, :1]).astype(v_ref.dtype)

        def values(h):
            a2 = jnp.concatenate([alpha[h], alpha.pop(h)], axis=1)
            acc_ref[h] = a2 * acc_ref[h] + jnp.dot(p.pop(h), v1[h // grp], preferred_element_type=jnp.float32)

        for t in range(ATT_HEADS + 2):
            if t < ATT_HEADS:
                scores(t)
            if 0 <= t - 1 < ATT_HEADS:
                probs(t - 1)
            if 0 <= t - 2 < ATT_HEADS:
                values(t - 2)

    @pl.when(j == pl.num_programs(2) - 1)
    def _():
        for h in range(ATT_HEADS):
            acc = acc_ref[h]
            o_ref[:, h * HEAD_DIM:(h + 1) * HEAD_DIM] = (acc[:, :HEAD_DIM] / acc[:, HEAD_DIM:]).astype(o_ref.dtype)


def masked_attention(q, k, P, bias, B, S):
    _, nq, nkv, tq, tk = bias.shape

    def kv_idx(i, j):
        return jnp.minimum(j, (i * tq + tq - 1) // tk)

    return pl.pallas_call(
        functools.partial(_attn_kernel, tq=tq, tk=tk),
        out_shape=jax.ShapeDtypeStruct((B * S, ATT_Q_WIDTH), jnp.bfloat16),
        grid=(B, nq, nkv),
        in_specs=[pl.BlockSpec((tq, ATT_Q_WIDTH), lambda b, i, j: (b * nq + i, 0)),
                  pl.BlockSpec((tk, ATT_KV_WIDTH), lambda b, i, j: (b * nkv + kv_idx(i, j), 0)),
                  pl.BlockSpec((tk, ATT_KV_WIDTH), lambda b, i, j: (b * nkv + kv_idx(i, j), OFF_AV // ATT_KV_WIDTH)),
                  pl.BlockSpec((None, None, None, tq, tk), lambda b, i, j: (b, i, kv_idx(i, j), 0, 0))],
        out_specs=pl.BlockSpec((tq, ATT_Q_WIDTH), lambda b, i, j: (b * nq + i, 0)),
        scratch_shapes=[pltpu.VMEM((ATT_HEADS, tq, LANES), jnp.float32),
                        pltpu.VMEM((ATT_HEADS, tq, HEAD_DIM + LANES), jnp.float32)],
        compiler_params=_cparams(("parallel", "parallel", "arbitrary")),
        name="masked_attention",
    )(q, k, P, bias)


GDN_HALO = BF16_SUBLANES


def _gdn_kernel(xq_ref, xk_ref, xv_ref, z_ref, sm_ref, cw_ref, alog_ref, dtb_ref, nw_ref, o_ref, state_ref, prev_ref):
    C = CHUNK
    f32, bf16 = jnp.float32, jnp.bfloat16
    heads = range(GDN_HEADS)

    @pl.when(pl.program_id(1) == 0)
    def _():
        state_ref[...] = jnp.zeros(state_ref.shape, f32)
        prev_ref[...] = jnp.zeros(prev_ref.shape, prev_ref.dtype)

    srow = lax.broadcasted_iota(jnp.int32, (CONV_WIDTH * C, GDN_HALO + C), 0)
    scol = lax.broadcasted_iota(jnp.int32, (CONV_WIDTH * C, GDN_HALO + C), 1)
    select = (scol == (srow & (C - 1)) + (srow >> (C.bit_length() - 1)) + (GDN_HALO - (CONV_WIDTH - 1))).astype(bf16)

    def conv_silu(x_ref, part):
        x = x_ref[...]
        xp = jnp.concatenate([prev_ref[part], x], axis=0)
        prev_ref[part] = x[C - GDN_HALO:]
        taps = jnp.dot(select, xp, preferred_element_type=f32)
        w = cw_ref[:, part * GDN_WIDTH:(part + 1) * GDN_WIDTH]
        y = None
        for jj in range(CONV_WIDTH):
            term = w[jj:jj + 1, :] * taps[jj * C:(jj + 1) * C, :]
            y = term if y is None else y + term
        return y * jax.nn.sigmoid(y)

    def l2(yh):
        return yh * lax.rsqrt(jnp.sum(yh * yh, axis=-1, keepdims=True) + EPS)

    yq, yk, yv = conv_silu(xq_ref, 0), conv_silu(xk_ref, 1), conv_silu(xv_ref, 2)
    sm = sm_ref[...]
    a_in = sm + dtb_ref[...]
    softplus = jnp.maximum(a_in, 0.0) + jnp.log(1.0 + jnp.exp(-jnp.abs(a_in)))
    g_all = -jnp.exp(alog_ref[...]) * softplus
    beta_all = jax.nn.sigmoid(sm)

    row = lax.broadcasted_iota(jnp.int32, (C, C), 0)
    col = lax.broadcasted_iota(jnp.int32, (C, C), 1)
    tril, strict = row >= col, row > col
    eye = (row == col).astype(f32)
    hi = lax.Precision.HIGHEST
    gc_col = jnp.dot(tril.astype(f32), g_all, precision=hi, preferred_element_type=f32)
    gc_row = jnp.dot(g_all.T, (row <= col).astype(f32), precision=hi, preferred_element_type=f32)
    nt = (((1,), (1,)), ((), ()))
    dot = functools.partial(jnp.dot, preferred_element_type=f32)
    kq, a, intra, rhs, qd, kd, eglast = [], [], [], [], [], [], []
    for j in heads:
        sl = slice(j * GDN_DK, (j + 1) * GDN_DK)
        gc = gc_col[:, SM_GA + j:SM_GA + j + 1]
        glast = gc[C - 1:C, :]
        eg = jnp.exp(gc)
        beta = beta_all[:, SM_GB + j:SM_GB + j + 1]
        q = (l2(yq[:, sl]) * (GDN_DK ** -0.5)).astype(bf16)
        k = l2(yk[:, sl]).astype(bf16)
        v = yv[:, sl].astype(bf16)
        kf = k.astype(f32)
        kq.append(jnp.concatenate([k, q], axis=0))
        rhs.append(jnp.concatenate([(v.astype(f32) * beta).astype(bf16), (kf * (beta * eg)).astype(bf16)], axis=1))
        qd.append((q.astype(f32) * eg).astype(bf16))
        kd.append((kf * jnp.exp(glast - gc)).astype(bf16))
        eglast.append(jnp.exp(glast))
    skq = [lax.dot_general(kq[j], kq[j][:C], nt, preferred_element_type=f32) for j in heads]
    for j in heads:
        gc = gc_col[:, SM_GA + j:SM_GA + j + 1]
        decay = jnp.exp(jnp.where(tril, gc - gc_row[SM_GA + j:SM_GA + j + 1, :], NEG_BIG))
        a.append(jnp.where(strict, skq[j][:C] * beta_all[:, SM_GB + j:SM_GB + j + 1] * decay, 0.0))
        intra.append((skq[j][C:] * decay).astype(bf16))
    ab = [a[j].astype(bf16) for j in heads]
    xb = [dot(ab[j], ab[j]).astype(bf16) for j in heads]
    tinv = [eye - a[j] for j in heads]
    for it in range(5):
        last = it == 4
        lhs = [tinv[j].astype(bf16) if last else jnp.concatenate([tinv[j].astype(bf16), xb[j]], axis=0) for j in heads]
        prod = [dot(lhs[j], xb[j]) for j in heads]
        tinv = [tinv[j] + prod[j][:C] for j in heads]
        if not last:
            xb = [prod[j][C:].astype(bf16) for j in heads]
    uw = [dot(tinv[j].astype(bf16), rhs[j]) for j in heads]
    sb = [state_ref[j].astype(bf16) for j in heads]
    ws = [dot(jnp.concatenate([uw[j][:, GDN_DV:].astype(bf16), qd[j]], axis=0), sb[j]) for j in heads]
    vb = [(uw[j][:, :GDN_DV] - ws[j][:C]).astype(bf16) for j in heads]
    o = [ws[j][C:] + dot(intra[j], vb[j]) for j in heads]
    upd = [lax.dot_general(kd[j], vb[j], (((0,), (0,)), ((), ())), preferred_element_type=f32) for j in heads]
    for j in heads:
        sl = slice(j * GDN_DK, (j + 1) * GDN_DK)
        state_ref[j] = state_ref[j] * eglast[j] + upd[j]
        ms = jnp.mean(o[j] * o[j], axis=-1, keepdims=True)
        z = z_ref[:, sl].astype(f32)
        o_ref[:, sl] = (o[j] * lax.rsqrt(ms + EPS) * nw_ref[...] * (z * jax.nn.sigmoid(z))).astype(o_ref.dtype)


def gated_deltanet(P, Psm, conv_w, a_log, dt_bias, norm_w, B, S):
    T = B * S
    C, W = CHUNK, GDN_WIDTH
    n = S // C

    def pad_lane(v, off):
        return jnp.zeros((1, LANES), jnp.float32).at[0, off:off + v.shape[0]].set(v)

    blk = lambda off: pl.BlockSpec((C, W), lambda b, c: (b * n + c, off // W))
    row = lambda w: pl.BlockSpec((1, w), lambda b, c: (0, 0))
    return pl.pallas_call(
        _gdn_kernel,
        out_shape=jax.ShapeDtypeStruct((T, W), jnp.bfloat16),
        grid=(B, n),
        in_specs=[blk(OFF_GQ), blk(OFF_GK), blk(OFF_GV), blk(OFF_GZ),
                  pl.BlockSpec((C, LANES), lambda b, c: (b * n + c, 0)),
                  pl.BlockSpec((CONV_WIDTH, 3 * W), lambda b, c: (0, 0)),
                  row(LANES), row(LANES), row(GDN_DV)],
        out_specs=blk(0),
        scratch_shapes=[pltpu.VMEM((GDN_HEADS, GDN_DK, GDN_DV), jnp.float32),
                        pltpu.VMEM((3, GDN_HALO, W), jnp.bfloat16)],
        compiler_params=_cparams(("parallel", "arbitrary")),
        name="gated_deltanet",
    )(P, P, P, P, Psm, conv_w, pad_lane(a_log, SM_GA), pad_lane(dt_bias, SM_GA), norm_w.reshape(1, GDN_DV))


def _merge_kernel(ya_ref, yg_ref, wa_ref, wg_ref, ga_ref, gg_ref, o_ref):
    pa = jnp.dot(ya_ref[...], wa_ref[...], preferred_element_type=jnp.float32)
    pg = jnp.dot(yg_ref[...], wg_ref[...], preferred_element_type=jnp.float32)
    ga = jax.nn.sigmoid(ga_ref[...].astype(jnp.float32))
    gg = jax.nn.sigmoid(gg_ref[...].astype(jnp.float32))
    o_ref[...] = (ga * pa + gg * pg).astype(o_ref.dtype)


def gated_merge(ya, yg, wa, wg, P, D, tm=1024, tn=512):
    T = ya.shape[0]
    tm, tn = _tile(T, tm), _tile(D, tn)
    goff = OFF_GATES // tn
    return pl.pallas_call(
        _merge_kernel,
        out_shape=jax.ShapeDtypeStruct((T, D), jnp.bfloat16),
        grid=(D // tn, T // tm),
        in_specs=[pl.BlockSpec((tm, ATT_Q_WIDTH), lambda j, i: (i, 0)),
                  pl.BlockSpec((tm, GDN_WIDTH), lambda j, i: (i, 0)),
                  pl.BlockSpec((ATT_Q_WIDTH, tn), lambda j, i: (0, j)),
                  pl.BlockSpec((GDN_WIDTH, tn), lambda j, i: (0, j)),
                  pl.BlockSpec((tm, tn), lambda j, i: (i, goff + j)),
                  pl.BlockSpec((tm, tn), lambda j, i: (i, goff + D // tn + j))],
        out_specs=pl.BlockSpec((tm, tn), lambda j, i: (i, j)),
        compiler_params=_cparams(("parallel", "parallel")),
        name="gated_merge",
    )(ya, yg, wa, wg, P, P)


def _outproj_kernel(a_ref, w_ref, x_ref, o_ref):
    o_ref[...] = x_ref[...] + jnp.dot(a_ref[...], w_ref[...], preferred_element_type=jnp.float32)


def out_proj_residual(a, w, x, tm=1024, tn=512):
    T, D = x.shape
    K = a.shape[1]
    tm, tn = _tile(T, tm), _tile(D, tn)
    return pl.pallas_call(
        _outproj_kernel,
        out_shape=jax.ShapeDtypeStruct((T, D), jnp.float32),
        grid=(D // tn, T // tm),
        in_specs=[pl.BlockSpec((tm, K), lambda j, i: (i, 0)),
                  pl.BlockSpec((K, tn), lambda j, i: (0, j)),
                  pl.BlockSpec((tm, tn), lambda j, i: (i, j))],
        out_specs=pl.BlockSpec((tm, tn), lambda j, i: (i, j)),
        compiler_params=_cparams(("parallel", "parallel")),
        name="out_proj_residual",
    )(a, w, x)


def _ffn_norm_router_kernel(x_ref, w_ref, rhi_ref, rlo_ref, h_ref, lg_ref):
    x = x_ref[...]
    ms = jnp.mean(x * x, axis=-1, keepdims=True)
    h = x * lax.rsqrt(ms + EPS) * w_ref[...]
    h_ref[...] = h
    hh = h.astype(jnp.bfloat16)
    hl = (h - hh.astype(jnp.float32)).astype(jnp.bfloat16)
    f32 = jnp.float32
    lg_ref[...] = (jnp.dot(hh, rhi_ref[...], preferred_element_type=f32)
                   + jnp.dot(hh, rlo_ref[...], preferred_element_type=f32)
                   + jnp.dot(hl, rhi_ref[...], preferred_element_type=f32))


def ffn_norm_router(x1, norm_w, w_router, tm=256):
    T, D = x1.shape
    tm = _tile(T, tm)
    rhi = w_router.astype(jnp.bfloat16)
    rlo = (w_router - rhi.astype(jnp.float32)).astype(jnp.bfloat16)
    return pl.pallas_call(
        _ffn_norm_router_kernel,
        out_shape=[jax.ShapeDtypeStruct((T, D), jnp.float32), jax.ShapeDtypeStruct((T, LANES), jnp.float32)],
        grid=(T // tm,),
        in_specs=[pl.BlockSpec((tm, D), lambda i: (i, 0)), pl.BlockSpec((1, D), lambda i: (0, 0)),
                  pl.BlockSpec((D, LANES), lambda i: (0, 0)), pl.BlockSpec((D, LANES), lambda i: (0, 0))],
        out_specs=[pl.BlockSpec((tm, D), lambda i: (i, 0)), pl.BlockSpec((tm, LANES), lambda i: (i, 0))],
        compiler_params=_cparams(("parallel",)),
        name="ffn_norm_router",
    )(x1, norm_w.reshape(1, D), rhi, rlo)


def _routing_kernel(lg_ref, b_ref, eid_ref, wt_ref):
    lg = lg_ref[...] + b_ref[...]
    lane = lax.broadcasted_iota(jnp.int32, lg.shape, 1)
    ninf = -jnp.inf

    def first_argmax(vals, vmax):
        return jnp.min(jnp.where(vals == vmax, lane, LANES), axis=-1, keepdims=True)

    glog = jnp.where(lane < N_GROUPS, lg, ninf)
    gmax = jnp.max(glog, axis=-1, keepdims=True)
    p_grp = 1.0 / jnp.sum(jnp.exp(glog - gmax), axis=-1, keepdims=True)
    grp = first_argmax(glog, gmax)
    base = RT_EXP + grp * EXPERTS_PER_GROUP
    elog = jnp.where((lane >= base) & (lane < base + EXPERTS_PER_GROUP), lg, ninf)
    emax = jnp.max(elog, axis=-1, keepdims=True)
    idx1 = first_argmax(elog, emax)
    elog2 = jnp.where(lane == idx1, ninf, elog)
    emax2 = jnp.max(elog2, axis=-1, keepdims=True)
    idx2 = first_argmax(elog2, emax2)
    e2 = jnp.exp(emax2 - emax)
    w1 = p_grp / (1.0 + e2)
    w2 = p_grp * e2 / (1.0 + e2)
    eid_ref[...] = jnp.where(lane == 0, idx1 - RT_EXP, jnp.where(lane == 1, idx2 - RT_EXP, 0))
    wt_ref[...] = jnp.where(lane == 0, w1, jnp.where(lane == 1, w2, 0.0))


def routing(logits, bias_row, tm=512):
    T = logits.shape[0]
    tm = _tile(T, tm)
    spec = pl.BlockSpec((tm, LANES), lambda i: (i, 0))
    return pl.pallas_call(
        _routing_kernel,
        out_shape=[jax.ShapeDtypeStruct((T, LANES), jnp.int32), jax.ShapeDtypeStruct((T, LANES), jnp.float32)],
        grid=(T // tm,),
        in_specs=[spec, pl.BlockSpec((1, LANES), lambda i: (0, 0))],
        out_specs=[spec, spec],
        compiler_params=_cparams(("parallel",)),
        name="routing",
    )(logits, bias_row)


GATHER_UNROLL = 8


def _row_gather_copy(src_hbm, row, dst, r, sem):
    return pltpu.make_async_copy(src_hbm.at[pl.ds(row, 1), :], dst.at[pl.ds(r, 1), :], sem)


def _moe_ffn_kernel(texp_ref, nused_ref, rowtok_ref, h_hbm, wg_ref, wu_ref, wd_ref, o_ref,
                    xbuf, sem, *, tm):
    i = pl.program_id(0)
    nused = nused_ref[0]

    def start_gather(tile, slot):
        def body(r, carry):
            _row_gather_copy(h_hbm, rowtok_ref[tile * tm + r], xbuf.at[slot], r, sem.at[slot]).start()
            return carry
        lax.fori_loop(0, tm, body, 0, unroll=GATHER_UNROLL)

    def wait_gather(slot):
        pltpu.make_async_copy(h_hbm.at[pl.ds(0, tm), :], xbuf.at[slot], sem.at[slot]).wait()

    @pl.when((i == 0) & (nused > 0))
    def _():
        start_gather(0, 0)

    @pl.when(i + 1 < nused)
    def _():
        start_gather(i + 1, (i + 1) % 2)

    @pl.when(i < nused)
    def _():
        slot = i % 2
        wait_gather(slot)
        x = xbuf[slot].astype(jnp.bfloat16)
        g = jnp.dot(x, wg_ref[0], preferred_element_type=jnp.float32)
        u = jnp.dot(x, wu_ref[0], preferred_element_type=jnp.float32)
        hmid = (g * jax.nn.sigmoid(g) * u).astype(jnp.bfloat16)
        y = jnp.dot(hmid, wd_ref[0], preferred_element_type=jnp.float32)
        o_ref[...] = y

    @pl.when(i >= nused)
    def _():
        o_ref[...] = jnp.zeros(o_ref.shape, o_ref.dtype)


def moe_ffn(h2, tile_expert, n_used, row_token, wg, wu, wd, tm):
    T, D = h2.shape
    R = row_token.shape[0]
    FF = wg.shape[2]
    ntiles = R // tm
    grid_spec = pltpu.PrefetchScalarGridSpec(
        num_scalar_prefetch=3,
        grid=(ntiles,),
        in_specs=[pl.BlockSpec(memory_space=pl.ANY),
                  pl.BlockSpec((1, D, FF), lambda i, te, nu, rt: (te[i], 0, 0)),
                  pl.BlockSpec((1, D, FF), lambda i, te, nu, rt: (te[i], 0, 0)),
                  pl.BlockSpec((1, FF, D), lambda i, te, nu, rt: (te[i], 0, 0))],
        out_specs=pl.BlockSpec((tm, D), lambda i, te, nu, rt: (i, 0)),
        scratch_shapes=[pltpu.VMEM((2, tm, D), jnp.float32), pltpu.SemaphoreType.DMA((2,))],
    )
    return pl.pallas_call(
        functools.partial(_moe_ffn_kernel, tm=tm),
        out_shape=jax.ShapeDtypeStruct((R, D), jnp.float32),
        grid_spec=grid_spec,
        compiler_params=_cparams(("arbitrary",)),
        name="moe_ffn",
    )(tile_expert, n_used, row_token, h2, wg, wu, wd)


def _combine_kernel(dest_ref, x_ref, wt_ref, ys_hbm, o_ref, buf, sem, *, tm):
    i = pl.program_id(0)
    n = pl.num_programs(0)

    def start_gather(tile, slot):
        def body(r, carry):
            for kk in range(TOP_K_EXPERTS):
                row = dest_ref[(tile * tm + r) * TOP_K_EXPERTS + kk]
                _row_gather_copy(ys_hbm, row, buf.at[slot, kk], r, sem.at[slot]).start()
            return carry
        lax.fori_loop(0, tm, body, 0, unroll=GATHER_UNROLL)

    def wait_gather(slot):
        for kk in range(TOP_K_EXPERTS):
            pltpu.make_async_copy(ys_hbm.at[pl.ds(0, tm), :], buf.at[slot, kk], sem.at[slot]).wait()

    @pl.when(i == 0)
    def _():
        start_gather(0, 0)

    @pl.when(i + 1 < n)
    def _():
        start_gather(i + 1, (i + 1) % 2)

    slot = i % 2
    wait_gather(slot)
    wt = wt_ref[...]
    o_ref[...] = x_ref[...] + wt[:, 0:1] * buf[slot, 0] + wt[:, 1:2] * buf[slot, 1]


def moe_combine(x1, wt_lanes, ys, dest_row, tm=128):
    T, D = x1.shape
    tm = _tile(T, tm)
    grid_spec = pltpu.PrefetchScalarGridSpec(
        num_scalar_prefetch=1,
        grid=(T // tm,),
        in_specs=[pl.BlockSpec((tm, D), lambda i, d: (i, 0)), pl.BlockSpec((tm, LANES), lambda i, d: (i, 0)),
                  pl.BlockSpec(memory_space=pl.ANY)],
        out_specs=pl.BlockSpec((tm, D), lambda i, d: (i, 0)),
        scratch_shapes=[pltpu.VMEM((2, TOP_K_EXPERTS, tm, D), jnp.float32), pltpu.SemaphoreType.DMA((2,))],
    )
    return pl.pallas_call(
        functools.partial(_combine_kernel, tm=tm),
        out_shape=jax.ShapeDtypeStruct((T, D), jnp.float32),
        grid_spec=grid_spec,
        compiler_params=_cparams(("arbitrary",)),
        name="moe_combine",
    )(dest_row, x1, wt_lanes, ys)


PLAN_BLOCK = 128


def moe_dispatch_plan(eid, tm):
    T = eid.shape[0]
    A = T * TOP_K_EXPERTS
    e_flat = eid.reshape(A)
    blk = PLAN_BLOCK if A % PLAN_BLOCK == 0 else A
    onehot = (e_flat[:, None] == jnp.arange(N_EXPERTS, dtype=jnp.int32)[None, :]).astype(jnp.float32)
    oh = onehot.reshape(A // blk, blk, N_EXPERTS)
    before = jnp.tril(jnp.ones((blk, blk), jnp.float32), -1)
    within = jnp.einsum('ij,bje->bie', before, oh)
    totals = jnp.sum(oh, axis=1)
    offs = jnp.cumsum(totals, axis=0) - totals
    rank = jnp.sum((within + offs[:, None, :]) * oh, axis=2).reshape(A).astype(jnp.int32)
    counts = (offs[-1] + totals[-1]).astype(jnp.int32)
    padded = ((counts + tm - 1) // tm) * tm
    pend = jnp.cumsum(padded)
    pstart = pend - padded
    dest_row = (pstart[e_flat] + rank).astype(jnp.int32)
    R = ((A + N_EXPERTS * (tm - 1)) + tm - 1) // tm * tm
    row_token = jnp.zeros((R,), jnp.int32).at[dest_row].set(jnp.arange(A, dtype=jnp.int32) // TOP_K_EXPERTS)
    tile_start = jnp.arange(R // tm, dtype=jnp.int32) * tm
    tile_expert = jnp.minimum(jnp.sum(tile_start[:, None] >= pend[None, :], axis=1), N_EXPERTS - 1).astype(jnp.int32)
    n_used = (pend[-1] // tm).astype(jnp.int32).reshape(1)
    return dest_row, row_token, tile_expert, n_used


SKIP_IDX = IDX_DIM + IDX_HEADS
SKIP_GDN = SKIP_IDX + 2 * GDN_HEADS


def _regroup_small_kernel(a_ref, b_ref, o_ref):
    rows = jnp.concatenate([a_ref[:SKIP_IDX], b_ref[SKIP_IDX:SKIP_GDN],
                            jnp.zeros((LANES - SKIP_GDN, a_ref.shape[1]), jnp.float32)], axis=0)
    o_ref[...] = rows.T.astype(o_ref.dtype)


def _in_proj_kernel(h_ref, a_ref, b_ref, *refs, tn, kc, nj, ncast):
    j, i = pl.program_id(0), pl.program_id(1)
    o_ref, wscr, stage = refs[ncast], refs[2 * ncast + 1], refs[2 * ncast + 2]
    jj = jnp.minimum(j, nj - 1)
    skip = jnp.where(jj < OFF_GQ // tn, 0, jnp.where(jj < OFF_GATES // tn, SKIP_IDX, SKIP_GDN))

    def background():
        stage[:tn] = a_ref[...]
        stage[tn:] = b_ref[...]
        window = stage[pl.ds(pl.multiple_of(skip, 8), tn), :]
        wscr[j % 2, pl.ds(pl.multiple_of(i * kc, kc), kc), :] = window.T.astype(wscr.dtype)
        for x_ref, y_ref in zip(refs[:ncast], refs[ncast + 1:2 * ncast + 1]):
            y_ref[...] = x_ref[...].astype(y_ref.dtype)

    @pl.when(j == 0)
    def _():
        background()
        o_ref[...] = jnp.zeros(o_ref.shape, o_ref.dtype)

    @pl.when(j > 0)
    def _():
        background()
        o_ref[...] = jnp.dot(h_ref[...], wscr[(j - 1) % 2], preferred_element_type=jnp.float32).astype(o_ref.dtype)


def in_proj_regrouping(h, w, to_cast, D, tm=1024, tn=1024):
    T, K = h.shape
    n_big = OFF_GATES + 2 * D
    tm, tn = _tile(T, tm), _tile(n_big, tn)
    ni, nj = T // tm, n_big // tn
    kc = K // ni
    assert K % ni == 0 and kc % LANES == 0 and tn % LANES == 0 and w.shape[1] == n_big + SKIP_GDN
    assert OFF_GQ % tn == 0 and OFF_GATES % tn == 0 and SKIP_GDN <= LANES and SKIP_IDX % 8 == 0 and SKIP_GDN % 8 == 0
    wt = w.T
    nsteps = (nj + 1) * ni
    cast_in, cast_out, cast_shapes = [], [], []
    for x in to_cast:
        rows, cols = x.shape
        rb = _cast_block_rows(rows, nsteps)
        imap = functools.partial(lambda j, i, last: (jnp.minimum(j * ni + i, last), 0), last=rows // rb - 1)
        cast_in.append(pl.BlockSpec((rb, cols), imap))
        cast_out.append(pl.BlockSpec((rb, cols), imap))
        cast_shapes.append(jax.ShapeDtypeStruct((rows, cols), jnp.bfloat16))
    src = lambda j: jnp.minimum(j, nj - 1)
    outs = pl.pallas_call(
        functools.partial(_in_proj_kernel, tn=tn, kc=kc, nj=nj, ncast=len(to_cast)),
        out_shape=[jax.ShapeDtypeStruct((T, n_big), jnp.bfloat16)] + cast_shapes,
        grid=(nj + 1, ni),
        in_specs=[pl.BlockSpec((tm, K), lambda j, i: (i, 0)),
                  pl.BlockSpec((tn, kc), lambda j, i: (src(j), i)),
                  pl.BlockSpec((LANES, kc), lambda j, i: ((src(j) + 1) * (tn // LANES), i))] + cast_in,
        out_specs=[pl.BlockSpec((tm, tn), lambda j, i: (jnp.where(j == 0, 0, i), jnp.maximum(j - 1, 0)))] + cast_out,
        scratch_shapes=[pltpu.VMEM((2, K, tn), jnp.bfloat16), pltpu.VMEM((tn + LANES, kc), jnp.float32)],
        compiler_params=pltpu.CompilerParams(dimension_semantics=("arbitrary", "arbitrary"),
                                             vmem_limit_bytes=IN_PROJ_VMEM_LIMIT),
        name="in_proj",
    )(h, wt, wt, *to_cast)
    return outs[0], outs[1:]


def small_group_weights(w, D, t=512):
    assert OFF_GQ % LANES == 0 and OFF_GATES % LANES == 0
    t = _tile(D, t)
    wt = w.T
    return pl.pallas_call(
        _regroup_small_kernel,
        out_shape=jax.ShapeDtypeStruct((D, LANES), jnp.bfloat16),
        grid=(D // t,),
        in_specs=[pl.BlockSpec((LANES, t), lambda k: (OFF_GQ // LANES, k)),
                  pl.BlockSpec((LANES, t), lambda k: (OFF_GATES // LANES, k))],
        out_specs=pl.BlockSpec((t, LANES), lambda k: (k, 0)),
        compiler_params=_cparams(("parallel",)),
        name="regroup_w_small",
    )(wt, wt)


def kernel(x, positions, mix_norm_w, w_in, q_norm_w, k_norm_w, idx_k_norm_w, conv_w, a_log, dt_bias, gdn_norm_w, w_proj_attn, w_proj_gdn, w_out, ffn_norm_w, w_router_group, b_router_group, w_router_expert, b_router_expert, w_gate, w_up, w_down):
    B, S, D = x.shape
    T = B * S
    bf16 = jnp.bfloat16
    moe_tm = 256 if T * TOP_K_EXPERTS >= 256 * N_EXPERTS else 64
    xt = x.reshape(T, D)
    cos_t, sin_t = rope_tables(positions)
    for l in range(w_in.shape[0]):
        w_small = small_group_weights(w_in[l], D)
        h = rmsnorm(xt, mix_norm_w[l], bf16)
        NE, _, FF = w_gate[l].shape
        later_weights = [w_proj_attn[l], w_proj_gdn[l], w_out[l], w_gate[l].reshape(NE * D, FF),
                         w_up[l].reshape(NE * D, FF), w_down[l].reshape(NE * FF, D)]
        P, (wpa, wpg, wo, wg, wu, wd) = in_proj_regrouping(h, w_in[l], later_weights, D)
        wg, wu, wd = wg.reshape(NE, D, FF), wu.reshape(NE, D, FF), wd.reshape(NE, FF, D)
        Psm = matmul(h, w_small, jnp.float32, name="in_proj_small")
        q, k, qi, ki, wi = attn_prep(P, Psm, cos_t, sin_t, q_norm_w[l], k_norm_w[l], idx_k_norm_w[l])
        bias = indexer_mask(qi, ki, wi, B, S)
        y_attn = masked_attention(q, k, P, bias, B, S)
        y_gdn = gated_deltanet(P, Psm, conv_w[l], a_log[l], dt_bias[l], gdn_norm_w[l], B, S)
        mixed = gated_merge(y_attn, y_gdn, wpa, wpg, P, D)
        x1 = out_proj_residual(mixed, wo, xt)
        w_router = jnp.zeros((D, LANES), jnp.float32)
        w_router = w_router.at[:, :N_GROUPS].set(w_router_group[l])
        w_router = w_router.at[:, RT_EXP:RT_EXP + N_EXPERTS].set(
            w_router_expert[l].transpose(1, 0, 2).reshape(D, N_EXPERTS))
        b_router = jnp.zeros((1, LANES), jnp.float32)
        b_router = b_router.at[0, :N_GROUPS].set(b_router_group[l])
        b_router = b_router.at[0, RT_EXP:RT_EXP + N_EXPERTS].set(b_router_expert[l].reshape(N_EXPERTS))
        h2, logits = ffn_norm_router(x1, ffn_norm_w[l], w_router)
        eid_l, wt_l = routing(logits, b_router)
        dest_row, row_token, tile_expert, n_used = moe_dispatch_plan(eid_l[:, :TOP_K_EXPERTS], moe_tm)
        ys = moe_ffn(h2, tile_expert, n_used, row_token, wg, wu, wd, moe_tm)
        xt = moe_combine(x1, wt_l, ys, dest_row)
    return xt.reshape(B, S, D)
```

```python
import functools
import math

import jax
import jax.numpy as jnp
from jax import lax
from jax.experimental import pallas as pl
from jax.experimental.pallas import tpu as pltpu

ATT_HEADS = 16
ATT_KV_HEADS = 4
HEAD_DIM = 128
IDX_HEADS = 16
IDX_DIM = 64
TOPK_MAX = 256
ROPE_THETA = 10000.0
GDN_HEADS = 16
GDN_DK = 128
GDN_DV = 128
CONV_WIDTH = 4
CHUNK = 64
N_GROUPS = 4
EXPERTS_PER_GROUP = 8
N_EXPERTS = N_GROUPS * EXPERTS_PER_GROUP
TOP_K_EXPERTS = 2
EPS = 1e-6

ATT_Q_WIDTH = ATT_HEADS * HEAD_DIM
ATT_KV_WIDTH = ATT_KV_HEADS * HEAD_DIM
IDX_Q_WIDTH = IDX_HEADS * IDX_DIM
GDN_WIDTH = GDN_HEADS * GDN_DK

LANES = 128
BF16_SUBLANES = 16
VMEM_LIMIT = 56 * 1024 * 1024
IN_PROJ_VMEM_LIMIT = 60 * 1024 * 1024
NEG_BIG = -1e30
LOG2E = math.log2(math.e)
SCORE_MASKED = 3.0e38
BISECT_MAX_STEPS = 192
BISECT_UNROLL = 4
COUNT_ROWS = 128

OFF_AQ = 0
OFF_AK = OFF_AQ + ATT_Q_WIDTH
OFF_AV = OFF_AK + ATT_KV_WIDTH
OFF_IQ = OFF_AV + ATT_KV_WIDTH
OFF_GQ = OFF_IQ + IDX_Q_WIDTH
OFF_GK = OFF_GQ + GDN_WIDTH
OFF_GV = OFF_GK + GDN_WIDTH
OFF_GZ = OFF_GV + GDN_WIDTH
OFF_GATES = OFF_GZ + GDN_WIDTH
SM_IK = 0
SM_IW = SM_IK + IDX_DIM
SM_GA = SM_IW + IDX_HEADS
SM_GB = SM_GA + GDN_HEADS
RT_EXP = 8


def _cparams(sem):
    return pltpu.CompilerParams(dimension_semantics=sem, vmem_limit_bytes=VMEM_LIMIT)


def _tile(n, pref):
    t = min(n, pref)
    assert n % t == 0, (n, pref)
    return t


def _rmsnorm_kernel(x_ref, w_ref, o_ref):
    x = x_ref[...]
    ms = jnp.mean(x * x, axis=-1, keepdims=True)
    o_ref[...] = (x * lax.rsqrt(ms + EPS) * w_ref[...]).astype(o_ref.dtype)


def rmsnorm(x, w, out_dtype, tm=256):
    T, D = x.shape
    tm = _tile(T, tm)
    return pl.pallas_call(
        _rmsnorm_kernel,
        out_shape=jax.ShapeDtypeStruct((T, D), out_dtype),
        grid=(T // tm,),
        in_specs=[pl.BlockSpec((tm, D), lambda i: (i, 0)), pl.BlockSpec((1, D), lambda i: (0, 0))],
        out_specs=pl.BlockSpec((tm, D), lambda i: (i, 0)),
        compiler_params=_cparams(("parallel",)),
        name="rmsnorm",
    )(x, w.reshape(1, D))


def _matmul_kernel(a_ref, b_ref, o_ref):
    o_ref[...] = jnp.dot(a_ref[...], b_ref[...], preferred_element_type=jnp.float32).astype(o_ref.dtype)


def matmul(a, b, out_dtype, tm=1024, tn=1024, name="matmul"):
    M, K = a.shape
    _, N = b.shape
    tm, tn = _tile(M, tm), _tile(N, tn)
    return pl.pallas_call(
        _matmul_kernel,
        out_shape=jax.ShapeDtypeStruct((M, N), out_dtype),
        grid=(N // tn, M // tm),
        in_specs=[pl.BlockSpec((tm, K), lambda j, i: (i, 0)), pl.BlockSpec((K, tn), lambda j, i: (0, j))],
        out_specs=pl.BlockSpec((tm, tn), lambda j, i: (i, j)),
        compiler_params=_cparams(("parallel", "parallel")),
        name=name,
    )(a, b)


def _cast_block_rows(rows, nsteps):
    for rb in range(BF16_SUBLANES, rows + 1, BF16_SUBLANES):
        if rows % rb == 0 and rows // rb <= nsteps:
            return rb
    raise ValueError((rows, nsteps))


def _rope_table_kernel(pos_ref, inv_ref, sgn_ref, cos_ref, sin_ref):
    ang = pos_ref[...] * inv_ref[...]
    cos_ref[...] = jnp.cos(ang)
    sin_ref[...] = jnp.sin(ang) * sgn_ref[...]


def rope_tables(positions):
    T = positions.size
    pos = positions.reshape(T, 1).astype(jnp.float32)

    def inv(d):
        return ROPE_THETA ** (-jnp.arange(0, d, 2, dtype=jnp.float32) / d)

    i128, i64 = inv(HEAD_DIM), inv(IDX_DIM)
    inv_row = jnp.concatenate([i128, i128, i64, i64, i64, i64]).reshape(1, 2 * LANES)
    s128 = jnp.concatenate([-jnp.ones(HEAD_DIM // 2), jnp.ones(HEAD_DIM // 2)])
    s64 = jnp.concatenate([-jnp.ones(IDX_DIM // 2), jnp.ones(IDX_DIM // 2)])
    sgn_row = jnp.concatenate([s128, s64, s64]).astype(jnp.float32).reshape(1, 2 * LANES)
    tm = _tile(T, 512)
    spec = pl.BlockSpec((tm, 2 * LANES), lambda i: (i, 0))
    row = pl.BlockSpec((1, 2 * LANES), lambda i: (0, 0))
    return pl.pallas_call(
        _rope_table_kernel,
        out_shape=[jax.ShapeDtypeStruct((T, 2 * LANES), jnp.float32)] * 2,
        grid=(T // tm,),
        in_specs=[pl.BlockSpec((tm, 1), lambda i: (i, 0)), row, row],
        out_specs=[spec, spec],
        compiler_params=_cparams(("parallel",)),
        name="rope_tables",
    )(pos, inv_row, sgn_row)


def _rope128(x, cos, sin_signed):
    return x * cos + pltpu.roll(x, HEAD_DIM // 2, 1) * sin_signed


def _rope64x2(x, cos, sin_signed, lane):
    half = IDX_DIM // 2
    first = (lane % IDX_DIM) < half
    partner = jnp.where(first, pltpu.roll(x, LANES - half, 1), pltpu.roll(x, half, 1))
    return x * cos + partner * sin_signed


def _attn_prep_kernel(aq_ref, ak_ref, iq_ref, sm_ref, cos_ref, sin_ref, qw_ref, kw_ref, ikw_ref,
                      q_ref, k_ref, qi_ref, ki_ref, wi_ref):
    cos_a, sin_a = cos_ref[:, :LANES], sin_ref[:, :LANES]
    cos_i, sin_i = cos_ref[:, LANES:], sin_ref[:, LANES:]
    tm = cos_a.shape[0]
    lane = lax.broadcasted_iota(jnp.int32, (tm, LANES), 1)

    def head_norm(xh, w):
        ms = jnp.mean(xh * xh, axis=-1, keepdims=True)
        return xh * lax.rsqrt(ms + EPS) * w

    for h in range(ATT_HEADS):
        sl = slice(h * HEAD_DIM, (h + 1) * HEAD_DIM)
        xh = head_norm(aq_ref[:, sl].astype(jnp.float32), qw_ref[...])
        q_ref[:, sl] = (_rope128(xh, cos_a, sin_a) * (LOG2E * HEAD_DIM ** -0.5)).astype(q_ref.dtype)
    for h in range(ATT_KV_HEADS):
        sl = slice(h * HEAD_DIM, (h + 1) * HEAD_DIM)
        xh = head_norm(ak_ref[:, sl].astype(jnp.float32), kw_ref[...])
        k_ref[:, sl] = _rope128(xh, cos_a, sin_a).astype(k_ref.dtype)
    for p in range(IDX_Q_WIDTH // LANES):
        sl = slice(p * LANES, (p + 1) * LANES)
        xp = iq_ref[:, sl].astype(jnp.float32)
        qi_ref[:, sl] = (_rope64x2(xp, cos_i, sin_i, lane) * (IDX_DIM ** -0.5)).astype(qi_ref.dtype)
    sm = sm_ref[...]
    in_k = lane < IDX_DIM
    xk = jnp.where(in_k, sm, 0.0)
    ms = jnp.sum(xk * xk, axis=-1, keepdims=True) * (1.0 / IDX_DIM)
    kn = xk * lax.rsqrt(ms + EPS) * ikw_ref[...]
    kr = jnp.where(in_k, _rope64x2(kn, cos_i, sin_i, lane), 0.0)
    ki_ref[:, :LANES] = kr.astype(ki_ref.dtype)
    ki_ref[:, LANES:] = pltpu.roll(kr, IDX_DIM, 1).astype(ki_ref.dtype)
    wi_ref[...] = sm * (IDX_HEADS ** -0.5)


def attn_prep(P, Psm, cos_t, sin_t, q_norm_w, k_norm_w, idx_k_norm_w, tm=256):
    T = P.shape[0]
    tm = _tile(T, tm)
    ikw = jnp.concatenate([idx_k_norm_w, jnp.zeros((LANES - IDX_DIM,), jnp.float32)]).reshape(1, LANES)
    row = lambda w: pl.BlockSpec((1, w), lambda i: (0, 0))
    return pl.pallas_call(
        _attn_prep_kernel,
        out_shape=[jax.ShapeDtypeStruct((T, ATT_Q_WIDTH), jnp.bfloat16),
                   jax.ShapeDtypeStruct((T, ATT_KV_WIDTH), jnp.bfloat16),
                   jax.ShapeDtypeStruct((T, IDX_Q_WIDTH), jnp.bfloat16),
                   jax.ShapeDtypeStruct((T, 2 * LANES), jnp.bfloat16),
                   jax.ShapeDtypeStruct((T, LANES), jnp.float32)],
        grid=(T // tm,),
        in_specs=[pl.BlockSpec((tm, ATT_Q_WIDTH), lambda i: (i, OFF_AQ // ATT_Q_WIDTH)),
                  pl.BlockSpec((tm, ATT_KV_WIDTH), lambda i: (i, OFF_AK // ATT_KV_WIDTH)),
                  pl.BlockSpec((tm, IDX_Q_WIDTH), lambda i: (i, OFF_IQ // IDX_Q_WIDTH)),
                  pl.BlockSpec((tm, LANES), lambda i: (i, 0)),
                  pl.BlockSpec((tm, 2 * LANES), lambda i: (i, 0)),
                  pl.BlockSpec((tm, 2 * LANES), lambda i: (i, 0)),
                  row(LANES), row(LANES), row(LANES)],
        out_specs=[pl.BlockSpec((tm, ATT_Q_WIDTH), lambda i: (i, 0)),
                   pl.BlockSpec((tm, ATT_KV_WIDTH), lambda i: (i, 0)),
                   pl.BlockSpec((tm, IDX_Q_WIDTH), lambda i: (i, 0)),
                   pl.BlockSpec((tm, 2 * LANES), lambda i: (i, 0)),
                   pl.BlockSpec((tm, LANES), lambda i: (i, 0))],
        compiler_params=_cparams(("parallel",)),
        name="attn_prep",
    )(P, P, P, Psm, cos_t, sin_t, q_norm_w.reshape(1, LANES), k_norm_w.reshape(1, LANES), ikw)


def _lane_fold(x, op):
    s = x[:, :LANES]
    for j in range(1, x.shape[1] // LANES):
        s = op(s, x[:, j * LANES:(j + 1) * LANES])
    return s


def _indexer_kernel(qi_ref, ki_ref, wi_ref, bias_ref, key_ref, *, n_sel, tq, tk, nchunks):
    q0 = pl.program_id(1) * tq
    nck = (q0 + tq + tk - 1) // tk
    wi = wi_ref[...]
    rowpos = q0 + lax.broadcasted_iota(jnp.int32, (tq, tk), 0)
    colpos0 = lax.broadcasted_iota(jnp.int32, (tq, tk), 1)

    def score_chunk(c, carry):
        mx, mn = carry
        ks = ki_ref[pl.ds(pl.multiple_of(c * tk, tk), tk), :]
        acc = jnp.zeros((tq, tk), jnp.float32)
        for h in range(IDX_HEADS):
            qp = qi_ref[:, (h // 2) * LANES:(h // 2 + 1) * LANES]
            kh = ks[:, (h % 2) * LANES:(h % 2 + 1) * LANES]
            d = lax.dot_general(qp, kh, (((1,), (1,)), ((), ())), preferred_element_type=jnp.float32)
            acc = acc + wi[:, SM_IW + h:SM_IW + h + 1] * jnp.maximum(d, 0.0)
        causal = colpos0 + c * tk <= rowpos
        key_ref[c] = jnp.where(causal, acc, -SCORE_MASKED)
        mx = jnp.maximum(mx, _lane_fold(jnp.where(causal, acc, -SCORE_MASKED), jnp.maximum))
        mn = jnp.minimum(mn, _lane_fold(jnp.where(causal, acc, SCORE_MASKED), jnp.minimum))
        return mx, mn

    mx, mn = lax.fori_loop(0, nck, score_chunk, (jnp.full((tq, LANES), -SCORE_MASKED, jnp.float32),
                                                 jnp.full((tq, LANES), SCORE_MASKED, jnp.float32)))

    ones = jnp.ones((LANES, LANES), jnp.bfloat16)

    def count_ge(probe):
        accs = []
        for r0 in range(0, tq, COUNT_ROWS):
            rows = slice(r0, min(r0 + COUNT_ROWS, tq))

            def body(c, acc, rows=rows):
                for j in range(tk // LANES):
                    acc = acc + jnp.where(key_ref[c, rows, j * LANES:(j + 1) * LANES] >= probe[rows], 1.0, 0.0)
                return acc

            accs.append(lax.fori_loop(0, nck, body, jnp.zeros((rows.stop - rows.start, LANES), jnp.float32)))
        acc = accs[0] if len(accs) == 1 else jnp.concatenate(accs, axis=0)
        return jnp.dot(acc.astype(jnp.bfloat16), ones, preferred_element_type=jnp.float32)

    lo = jnp.broadcast_to(jnp.min(mn, axis=1, keepdims=True), (tq, LANES))
    hi = jnp.broadcast_to(jnp.max(mx, axis=1, keepdims=True), (tq, LANES))
    ncausal = q0 + lax.broadcasted_iota(jnp.int32, (tq, LANES), 0) + 1
    hi = jnp.where(ncausal <= n_sel, lo, hi)

    def unfinished(carry):
        return (carry[0] < BISECT_MAX_STEPS) & (carry[3] > 0)

    def bisect(carry):
        it, lo, hi, _ = carry
        for _ in range(BISECT_UNROLL):
            mid = 0.5 * lo + 0.5 * hi
            cnt = count_ge(mid)
            ok = cnt >= n_sel
            open_row = (mid > lo) & (mid < hi)
            lo = jnp.where(ok, mid, lo)
            hi = jnp.where(cnt == n_sel, mid, jnp.where(ok, hi, mid))
            open_row = open_row & (hi > lo)
        return it + BISECT_UNROLL, lo, hi, jnp.sum(jnp.where(open_row, 1, 0))

    _, thr, _, _ = lax.while_loop(unfinished, bisect, (jnp.int32(0), lo, hi, jnp.int32(1)))

    def write_chunk(c, carry):
        for j in range(tk // LANES):
            cols = slice(j * LANES, (j + 1) * LANES)
            bias_ref[c, :, cols] = jnp.where(key_ref[c, :, cols] >= thr, 0.0, NEG_BIG).astype(bias_ref.dtype)
        return carry

    def write_masked(c, carry):
        bias_ref[c] = jnp.full((tq, tk), NEG_BIG, bias_ref.dtype)
        return carry

    lax.fori_loop(0, nck, write_chunk, 0)
    lax.fori_loop(nck, nchunks, write_masked, 0)


def indexer_mask(qi, ki, wi, B, S, tq=512, tk=512):
    tq, tk = _tile(S, tq), _tile(S, tk)
    n_sel = min(TOPK_MAX, S // 4)
    nq, nchunks = S // tq, S // tk
    return pl.pallas_call(
        functools.partial(_indexer_kernel, n_sel=n_sel, tq=tq, tk=tk, nchunks=nchunks),
        out_shape=jax.ShapeDtypeStruct((B, nq, nchunks, tq, tk), jnp.bfloat16),
        grid=(B, nq),
        in_specs=[pl.BlockSpec((tq, IDX_Q_WIDTH), lambda b, i: (b * nq + i, 0)),
                  pl.BlockSpec((S, 2 * LANES), lambda b, i: (b, 0)),
                  pl.BlockSpec((tq, LANES), lambda b, i: (b * nq + i, 0))],
        out_specs=pl.BlockSpec((None, None, nchunks, tq, tk), lambda b, i: (b, i, 0, 0, 0)),
        scratch_shapes=[pltpu.VMEM((nchunks, tq, tk), jnp.float32)],
        compiler_params=_cparams(("parallel", "parallel")),
        name="indexer_mask",
    )(qi, ki, wi)


def _attn_kernel(q_ref, k_ref, v_ref, bias_ref, o_ref, m_ref, acc_ref, *, tq, tk):
    i, j = pl.program_id(1), pl.program_id(2)
    grp = ATT_HEADS // ATT_KV_HEADS

    @pl.when(j == 0)
    def _():
        m_ref[...] = jnp.full(m_ref.shape, -1e38, jnp.float32)
        acc_ref[...] = jnp.zeros(acc_ref.shape, jnp.float32)

    @pl.when(j * tk <= i * tq + tq - 1)
    def _():
        bias = bias_ref[...].astype(jnp.float32)
        ones = jnp.ones((tk, LANES), v_ref.dtype)
        v1 = [jnp.concatenate([v_ref[:, g * HEAD_DIM:(g + 1) * HEAD_DIM], ones], axis=1) for g in range(ATT_KV_HEADS)]
        s, m_new, alpha, p = {}, {}, {}, {}

        def scores(h):
            g = h // grp
            qh = q_ref[:, h * HEAD_DIM:(h + 1) * HEAD_DIM]
            kh = k_ref[:, g * HEAD_DIM:(g + 1) * HEAD_DIM]
            s[h] = lax.dot_general(qh, kh, (((1,), (1,)), ((), ())), preferred_element_type=jnp.float32) + bias
            m_prev = m_ref[h]
            m_new[h] = jnp.maximum(m_prev, jnp.max(s[h], axis=1, keepdims=True))
            alpha[h] = jnp.exp2(m_prev - m_new[h])
            m_ref[h] = m_new[h]

        def probs(h):
            p[h] = jnp.exp2(s.pop(h) - m_new.pop(h)[:, :1]).astype(v_ref.dtype)

        def values(h):
            a2 = jnp.concatenate([alpha[h], alpha.pop(h)], axis=1)
            acc_ref[h] = a2 * acc_ref[h] + jnp.dot(p.pop(h), v1[h // grp], preferred_element_type=jnp.float32)

        for t in range(ATT_HEADS + 2):
            if t < ATT_HEADS:
                scores(t)
            if 0 <= t - 1 < ATT_HEADS:
                probs(t - 1)
            if 0 <= t - 2 < ATT_HEADS:
                values(t - 2)

    @pl.when(j == pl.num_programs(2) - 1)
    def _():
        for h in range(ATT_HEADS):
            acc = acc_ref[h]
            o_ref[:, h * HEAD_DIM:(h + 1) * HEAD_DIM] = (acc[:, :HEAD_DIM] / acc[:, HEAD_DIM:]).astype(o_ref.dtype)


def masked_attention(q, k, P, bias, B, S):
    _, nq, nkv, tq, tk = bias.shape

    def kv_idx(i, j):
        return jnp.minimum(j, (i * tq + tq - 1) // tk)

    return pl.pallas_call(
        functools.partial(_attn_kernel, tq=tq, tk=tk),
        out_shape=jax.ShapeDtypeStruct((B * S, ATT_Q_WIDTH), jnp.bfloat16),
        grid=(B, nq, nkv),
        in_specs=[pl.BlockSpec((tq, ATT_Q_WIDTH), lambda b, i, j: (b * nq + i, 0)),
                  pl.BlockSpec((tk, ATT_KV_WIDTH), lambda b, i, j: (b * nkv + kv_idx(i, j), 0)),
                  pl.BlockSpec((tk, ATT_KV_WIDTH), lambda b, i, j: (b * nkv + kv_idx(i, j), OFF_AV // ATT_KV_WIDTH)),
                  pl.BlockSpec((None, None, None, tq, tk), lambda b, i, j: (b, i, kv_idx(i, j), 0, 0))],
        out_specs=pl.BlockSpec((tq, ATT_Q_WIDTH), lambda b, i, j: (b * nq + i, 0)),
        scratch_shapes=[pltpu.VMEM((ATT_HEADS, tq, LANES), jnp.float32),
                        pltpu.VMEM((ATT_HEADS, tq, HEAD_DIM + LANES), jnp.float32)],
        compiler_params=_cparams(("parallel", "parallel", "arbitrary")),
        name="masked_attention",
    )(q, k, P, bias)


GDN_HALO = BF16_SUBLANES


def _gdn_kernel(xq_ref, xk_ref, xv_ref, z_ref, sm_ref, cw_ref, alog_ref, dtb_ref, nw_ref, o_ref, state_ref, prev_ref):
    C = CHUNK
    f32, bf16 = jnp.float32, jnp.bfloat16
    heads = range(GDN_HEADS)

    @pl.when(pl.program_id(1) == 0)
    def _():
        state_ref[...] = jnp.zeros(state_ref.shape, f32)
        prev_ref[...] = jnp.zeros(prev_ref.shape, prev_ref.dtype)

    srow = lax.broadcasted_iota(jnp.int32, (CONV_WIDTH * C, GDN_HALO + C), 0)
    scol = lax.broadcasted_iota(jnp.int32, (CONV_WIDTH * C, GDN_HALO + C), 1)
    select = (scol == (srow & (C - 1)) + (srow >> (C.bit_length() - 1)) + (GDN_HALO - (CONV_WIDTH - 1))).astype(bf16)

    def conv_silu(x_ref, part):
        x = x_ref[...]
        xp = jnp.concatenate([prev_ref[part], x], axis=0)
        prev_ref[part] = x[C - GDN_HALO:]
        taps = jnp.dot(select, xp, preferred_element_type=f32)
        w = cw_ref[:, part * GDN_WIDTH:(part + 1) * GDN_WIDTH]
        y = None
        for jj in range(CONV_WIDTH):
            term = w[jj:jj + 1, :] * taps[jj * C:(jj + 1) * C, :]
            y = term if y is None else y + term
        return y * jax.nn.sigmoid(y)

    def l2(yh):
        return yh * lax.rsqrt(jnp.sum(yh * yh, axis=-1, keepdims=True) + EPS)

    yq, yk, yv = conv_silu(xq_ref, 0), conv_silu(xk_ref, 1), conv_silu(xv_ref, 2)
    sm = sm_ref[...]
    a_in = sm + dtb_ref[...]
    softplus = jnp.maximum(a_in, 0.0) + jnp.log(1.0 + jnp.exp(-jnp.abs(a_in)))
    g_all = -jnp.exp(alog_ref[...]) * softplus
    beta_all = jax.nn.sigmoid(sm)

    row = lax.broadcasted_iota(jnp.int32, (C, C), 0)
    col = lax.broadcasted_iota(jnp.int32, (C, C), 1)
    tril, strict = row >= col, row > col
    eye = (row == col).astype(f32)
    hi = lax.Precision.HIGHEST
    gc_col = jnp.dot(tril.astype(f32), g_all, precision=hi, preferred_element_type=f32)
    gc_row = jnp.dot(g_all.T, (row <= col).astype(f32), precision=hi, preferred_element_type=f32)
    nt = (((1,), (1,)), ((), ()))
    dot = functools.partial(jnp.dot, preferred_element_type=f32)
    kq, a, intra, rhs, qd, kd, eglast = [], [], [], [], [], [], []
    for j in heads:
        sl = slice(j * GDN_DK, (j + 1) * GDN_DK)
        gc = gc_col[:, SM_GA + j:SM_GA + j + 1]
        glast = gc[C - 1:C, :]
        eg = jnp.exp(gc)
        beta = beta_all[:, SM_GB + j:SM_GB + j + 1]
        q = (l2(yq[:, sl]) * (GDN_DK ** -0.5)).astype(bf16)
        k = l2(yk[:, sl]).astype(bf16)
        v = yv[:, sl].astype(bf16)
        kf = k.astype(f32)
        kq.append(jnp.concatenate([k, q], axis=0))
        rhs.append(jnp.concatenate([(v.astype(f32) * beta).astype(bf16), (kf * (beta * eg)).astype(bf16)], axis=1))
        qd.append((q.astype(f32) * eg).astype(bf16))
        kd.append((kf * jnp.exp(glast - gc)).astype(bf16))
        eglast.append(jnp.exp(glast))
    skq = [lax.dot_general(kq[j], kq[j][:C], nt, preferred_element_type=f32) for j in heads]
    for j in heads:
        gc = gc_col[:, SM_GA + j:SM_GA + j + 1]
        decay = jnp.exp(jnp.where(tril, gc - gc_row[SM_GA + j:SM_GA + j + 1, :], NEG_BIG))
        a.append(jnp.where(strict, skq[j][:C] * beta_all[:, SM_GB + j:SM_GB + j + 1] * decay, 0.0))
        intra.append((skq[j][C:] * decay).astype(bf16))
    ab = [a[j].astype(bf16) for j in heads]
    xb = [dot(ab[j], ab[j]).astype(bf16) for j in heads]
    tinv = [eye - a[j] for j in heads]
    for it in range(5):
        last = it == 4
        lhs = [tinv[j].astype(bf16) if last else jnp.concatenate([tinv[j].astype(bf16), xb[j]], axis=0) for j in heads]
        prod = [dot(lhs[j], xb[j]) for j in heads]
        tinv = [tinv[j] + prod[j][:C] for j in heads]
        if not last:
            xb = [prod[j][C:].astype(bf16) for j in heads]
    uw = [dot(tinv[j].astype(bf16), rhs[j]) for j in heads]
    sb = [state_ref[j].astype(bf16) for j in heads]
    ws = [dot(jnp.concatenate([uw[j][:, GDN_DV:].astype(bf16), qd[j]], axis=0), sb[j]) for j in heads]
    vb = [(uw[j][:, :GDN_DV] - ws[j][:C]).astype(bf16) for j in heads]
    o = [ws[j][C:] + dot(intra[j], vb[j]) for j in heads]
    upd = [lax.dot_general(kd[j], vb[j], (((0,), (0,)), ((), ())), preferred_element_type=f32) for j in heads]
    for j in heads:
        sl = slice(j * GDN_DK, (j + 1) * GDN_DK)
        state_ref[j] = state_ref[j] * eglast[j] + upd[j]
        ms = jnp.mean(o[j] * o[j], axis=-1, keepdims=True)
        z = z_ref[:, sl].astype(f32)
        o_ref[:, sl] = (o[j] * lax.rsqrt(ms + EPS) * nw_ref[...] * (z * jax.nn.sigmoid(z))).astype(o_ref.dtype)


def gated_deltanet(P, Psm, conv_w, a_log, dt_bias, norm_w, B, S):
    T = B * S
    C, W = CHUNK, GDN_WIDTH
    n = S // C

    def pad_lane(v, off):
        return jnp.zeros((1, LANES), jnp.float32).at[0, off:off + v.shape[0]].set(v)

    blk = lambda off: pl.BlockSpec((C, W), lambda b, c: (b * n + c, off // W))
    row = lambda w: pl.BlockSpec((1, w), lambda b, c: (0, 0))
    return pl.pallas_call(
        _gdn_kernel,
        out_shape=jax.ShapeDtypeStruct((T, W), jnp.bfloat16),
        grid=(B, n),
        in_specs=[blk(OFF_GQ), blk(OFF_GK), blk(OFF_GV), blk(OFF_GZ),
                  pl.BlockSpec((C, LANES), lambda b, c: (b * n + c, 0)),
                  pl.BlockSpec((CONV_WIDTH, 3 * W), lambda b, c: (0, 0)),
                  row(LANES), row(LANES), row(GDN_DV)],
        out_specs=blk(0),
        scratch_shapes=[pltpu.VMEM((GDN_HEADS, GDN_DK, GDN_DV), jnp.float32),
                        pltpu.VMEM((3, GDN_HALO, W), jnp.bfloat16)],
        compiler_params=_cparams(("parallel", "arbitrary")),
        name="gated_deltanet",
    )(P, P, P, P, Psm, conv_w, pad_lane(a_log, SM_GA), pad_lane(dt_bias, SM_GA), norm_w.reshape(1, GDN_DV))


def _merge_kernel(ya_ref, yg_ref, wa_ref, wg_ref, ga_ref, gg_ref, o_ref):
    pa = jnp.dot(ya_ref[...], wa_ref[...], preferred_element_type=jnp.float32)
    pg = jnp.dot(yg_ref[...], wg_ref[...], preferred_element_type=jnp.float32)
    ga = jax.nn.sigmoid(ga_ref[...].astype(jnp.float32))
    gg = jax.nn.sigmoid(gg_ref[...].astype(jnp.float32))
    o_ref[...] = (ga * pa + gg * pg).astype(o_ref.dtype)


def gated_merge(ya, yg, wa, wg, P, D, tm=1024, tn=512):
    T = ya.shape[0]
    tm, tn = _tile(T, tm), _tile(D, tn)
    goff = OFF_GATES // tn
    return pl.pallas_call(
        _merge_kernel,
        out_shape=jax.ShapeDtypeStruct((T, D), jnp.bfloat16),
        grid=(D // tn, T // tm),
        in_specs=[pl.BlockSpec((tm, ATT_Q_WIDTH), lambda j, i: (i, 0)),
                  pl.BlockSpec((tm, GDN_WIDTH), lambda j, i: (i, 0)),
                  pl.BlockSpec((ATT_Q_WIDTH, tn), lambda j, i: (0, j)),
                  pl.BlockSpec((GDN_WIDTH, tn), lambda j, i: (0, j)),
                  pl.BlockSpec((tm, tn), lambda j, i: (i, goff + j)),
                  pl.BlockSpec((tm, tn), lambda j, i: (i, goff + D // tn + j))],
        out_specs=pl.BlockSpec((tm, tn), lambda j, i: (i, j)),
        compiler_params=_cparams(("parallel", "parallel")),
        name="gated_merge",
    )(ya, yg, wa, wg, P, P)


def _outproj_kernel(a_ref, w_ref, x_ref, o_ref):
    o_ref[...] = x_ref[...] + jnp.dot(a_ref[...], w_ref[...], preferred_element_type=jnp.float32)


def out_proj_residual(a, w, x, tm=1024, tn=512):
    T, D = x.shape
    K = a.shape[1]
    tm, tn = _tile(T, tm), _tile(D, tn)
    return pl.pallas_call(
        _outproj_kernel,
        out_shape=jax.ShapeDtypeStruct((T, D), jnp.float32),
        grid=(D // tn, T // tm),
        in_specs=[pl.BlockSpec((tm, K), lambda j, i: (i, 0)),
                  pl.BlockSpec((K, tn), lambda j, i: (0, j)),
                  pl.BlockSpec((tm, tn), lambda j, i: (i, j))],
        out_specs=pl.BlockSpec((tm, tn), lambda j, i: (i, j)),
        compiler_params=_cparams(("parallel", "parallel")),
        name="out_proj_residual",
    )(a, w, x)


def _ffn_norm_router_kernel(x_ref, w_ref, rhi_ref, rlo_ref, h_ref, lg_ref):
    x = x_ref[...]
    ms = jnp.mean(x * x, axis=-1, keepdims=True)
    h = x * lax.rsqrt(ms + EPS) * w_ref[...]
    h_ref[...] = h
    hh = h.astype(jnp.bfloat16)
    hl = (h - hh.astype(jnp.float32)).astype(jnp.bfloat16)
    f32 = jnp.float32
    lg_ref[...] = (jnp.dot(hh, rhi_ref[...], preferred_element_type=f32)
                   + jnp.dot(hh, rlo_ref[...], preferred_element_type=f32)
                   + jnp.dot(hl, rhi_ref[...], preferred_element_type=f32))


def ffn_norm_router(x1, norm_w, w_router, tm=256):
    T, D = x1.shape
    tm = _tile(T, tm)
    rhi = w_router.astype(jnp.bfloat16)
    rlo = (w_router - rhi.astype(jnp.float32)).astype(jnp.bfloat16)
    return pl.pallas_call(
        _ffn_norm_router_kernel,
        out_shape=[jax.ShapeDtypeStruct((T, D), jnp.float32), jax.ShapeDtypeStruct((T, LANES), jnp.float32)],
        grid=(T // tm,),
        in_specs=[pl.BlockSpec((tm, D), lambda i: (i, 0)), pl.BlockSpec((1, D), lambda i: (0, 0)),
                  pl.BlockSpec((D, LANES), lambda i: (0, 0)), pl.BlockSpec((D, LANES), lambda i: (0, 0))],
        out_specs=[pl.BlockSpec((tm, D), lambda i: (i, 0)), pl.BlockSpec((tm, LANES), lambda i: (i, 0))],
        compiler_params=_cparams(("parallel",)),
        name="ffn_norm_router",
    )(x1, norm_w.reshape(1, D), rhi, rlo)


def _routing_kernel(lg_ref, b_ref, eid_ref, wt_ref):
    lg = lg_ref[...] + b_ref[...]
    lane = lax.broadcasted_iota(jnp.int32, lg.shape, 1)
    ninf = -jnp.inf

    def first_argmax(vals, vmax):
        return jnp.min(jnp.where(vals == vmax, lane, LANES), axis=-1, keepdims=True)

    glog = jnp.where(lane < N_GROUPS, lg, ninf)
    gmax = jnp.max(glog, axis=-1, keepdims=True)
    p_grp = 1.0 / jnp.sum(jnp.exp(glog - gmax), axis=-1, keepdims=True)
    grp = first_argmax(glog, gmax)
    base = RT_EXP + grp * EXPERTS_PER_GROUP
    elog = jnp.where((lane >= base) & (lane < base + EXPERTS_PER_GROUP), lg, ninf)
    emax = jnp.max(elog, axis=-1, keepdims=True)
    idx1 = first_argmax(elog, emax)
    elog2 = jnp.where(lane == idx1, ninf, elog)
    emax2 = jnp.max(elog2, axis=-1, keepdims=True)
    idx2 = first_argmax(elog2, emax2)
    e2 = jnp.exp(emax2 - emax)
    w1 = p_grp / (1.0 + e2)
    w2 = p_grp * e2 / (1.0 + e2)
    eid_ref[...] = jnp.where(lane == 0, idx1 - RT_EXP, jnp.where(lane == 1, idx2 - RT_EXP, 0))
    wt_ref[...] = jnp.where(lane == 0, w1, jnp.where(lane == 1, w2, 0.0))


def routing(logits, bias_row, tm=512):
    T = logits.shape[0]
    tm = _tile(T, tm)
    spec = pl.BlockSpec((tm, LANES), lambda i: (i, 0))
    return pl.pallas_call(
        _routing_kernel,
        out_shape=[jax.ShapeDtypeStruct((T, LANES), jnp.int32), jax.ShapeDtypeStruct((T, LANES), jnp.float32)],
        grid=(T // tm,),
        in_specs=[spec, pl.BlockSpec((1, LANES), lambda i: (0, 0))],
        out_specs=[spec, spec],
        compiler_params=_cparams(("parallel",)),
        name="routing",
    )(logits, bias_row)


GATHER_UNROLL = 8


def _row_gather_copy(src_hbm, row, dst, r, sem):
    return pltpu.make_async_copy(src_hbm.at[pl.ds(row, 1), :], dst.at[pl.ds(r, 1), :], sem)


def _moe_ffn_kernel(texp_ref, nused_ref, rowtok_ref, h_hbm, wg_ref, wu_ref, wd_ref, o_ref,
                    xbuf, sem, *, tm):
    i = pl.program_id(0)
    nused = nused_ref[0]

    def start_gather(tile, slot):
        def body(r, carry):
            _row_gather_copy(h_hbm, rowtok_ref[tile * tm + r], xbuf.at[slot], r, sem.at[slot]).start()
            return carry
        lax.fori_loop(0, tm, body, 0, unroll=GATHER_UNROLL)

    def wait_gather(slot):
        pltpu.make_async_copy(h_hbm.at[pl.ds(0, tm), :], xbuf.at[slot], sem.at[slot]).wait()

    @pl.when((i == 0) & (nused > 0))
    def _():
        start_gather(0, 0)

    @pl.when(i + 1 < nused)
    def _():
        start_gather(i + 1, (i + 1) % 2)

    @pl.when(i < nused)
    def _():
        slot = i % 2
        wait_gather(slot)
        x = xbuf[slot].astype(jnp.bfloat16)
        g = jnp.dot(x, wg_ref[0], preferred_element_type=jnp.float32)
        u = jnp.dot(x, wu_ref[0], preferred_element_type=jnp.float32)
        hmid = (g * jax.nn.sigmoid(g) * u).astype(jnp.bfloat16)
        y = jnp.dot(hmid, wd_ref[0], preferred_element_type=jnp.float32)
        o_ref[...] = y

    @pl.when(i >= nused)
    def _():
        o_ref[...] = jnp.zeros(o_ref.shape, o_ref.dtype)


def moe_ffn(h2, tile_expert, n_used, row_token, wg, wu, wd, tm):
    T, D = h2.shape
    R = row_token.shape[0]
    FF = wg.shape[2]
    ntiles = R // tm
    grid_spec = pltpu.PrefetchScalarGridSpec(
        num_scalar_prefetch=3,
        grid=(ntiles,),
        in_specs=[pl.BlockSpec(memory_space=pl.ANY),
                  pl.BlockSpec((1, D, FF), lambda i, te, nu, rt: (te[i], 0, 0)),
                  pl.BlockSpec((1, D, FF), lambda i, te, nu, rt: (te[i], 0, 0)),
                  pl.BlockSpec((1, FF, D), lambda i, te, nu, rt: (te[i], 0, 0))],
        out_specs=pl.BlockSpec((tm, D), lambda i, te, nu, rt: (i, 0)),
        scratch_shapes=[pltpu.VMEM((2, tm, D), jnp.float32), pltpu.SemaphoreType.DMA((2,))],
    )
    return pl.pallas_call(
        functools.partial(_moe_ffn_kernel, tm=tm),
        out_shape=jax.ShapeDtypeStruct((R, D), jnp.float32),
        grid_spec=grid_spec,
        compiler_params=_cparams(("arbitrary",)),
        name="moe_ffn",
    )(tile_expert, n_used, row_token, h2, wg, wu, wd)


def _combine_kernel(dest_ref, x_ref, wt_ref, ys_hbm, o_ref, buf, sem, *, tm):
    i = pl.program_id(0)
    n = pl.num_programs(0)

    def start_gather(tile, slot):
        def body(r, carry):
            for kk in range(TOP_K_EXPERTS):
                row = dest_ref[(tile * tm + r) * TOP_K_EXPERTS + kk]
                _row_gather_copy(ys_hbm, row, buf.at[slot, kk], r, sem.at[slot]).start()
            return carry
        lax.fori_loop(0, tm, body, 0, unroll=GATHER_UNROLL)

    def wait_gather(slot):
        for kk in range(TOP_K_EXPERTS):
            pltpu.make_async_copy(ys_hbm.at[pl.ds(0, tm), :], buf.at[slot, kk], sem.at[slot]).wait()

    @pl.when(i == 0)
    def _():
        start_gather(0, 0)

    @pl.when(i + 1 < n)
    def _():
        start_gather(i + 1, (i + 1) % 2)

    slot = i % 2
    wait_gather(slot)
    wt = wt_ref[...]
    o_ref[...] = x_ref[...] + wt[:, 0:1] * buf[slot, 0] + wt[:, 1:2] * buf[slot, 1]


def moe_combine(x1, wt_lanes, ys, dest_row, tm=128):
    T, D = x1.shape
    tm = _tile(T, tm)
    grid_spec = pltpu.PrefetchScalarGridSpec(
        num_scalar_prefetch=1,
        grid=(T // tm,),
        in_specs=[pl.BlockSpec((tm, D), lambda i, d: (i, 0)), pl.BlockSpec((tm, LANES), lambda i, d: (i, 0)),
                  pl.BlockSpec(memory_space=pl.ANY)],
        out_specs=pl.BlockSpec((tm, D), lambda i, d: (i, 0)),
        scratch_shapes=[pltpu.VMEM((2, TOP_K_EXPERTS, tm, D), jnp.float32), pltpu.SemaphoreType.DMA((2,))],
    )
    return pl.pallas_call(
        functools.partial(_combine_kernel, tm=tm),
        out_shape=jax.ShapeDtypeStruct((T, D), jnp.float32),
        grid_spec=grid_spec,
        compiler_params=_cparams(("arbitrary",)),
        name="moe_combine",
    )(dest_row, x1, wt_lanes, ys)


PLAN_BLOCK = 128


def moe_dispatch_plan(eid, tm):
    T = eid.shape[0]
    A = T * TOP_K_EXPERTS
    e_flat = eid.reshape(A)
    blk = PLAN_BLOCK if A % PLAN_BLOCK == 0 else A
    onehot = (e_flat[:, None] == jnp.arange(N_EXPERTS, dtype=jnp.int32)[None, :]).astype(jnp.float32)
    oh = onehot.reshape(A // blk, blk, N_EXPERTS)
    before = jnp.tril(jnp.ones((blk, blk), jnp.float32), -1)
    within = jnp.einsum('ij,bje->bie', before, oh)
    totals = jnp.sum(oh, axis=1)
    offs = jnp.cumsum(totals, axis=0) - totals
    rank = jnp.sum((within + offs[:, None, :]) * oh, axis=2).reshape(A).astype(jnp.int32)
    counts = (offs[-1] + totals[-1]).astype(jnp.int32)
    padded = ((counts + tm - 1) // tm) * tm
    pend = jnp.cumsum(padded)
    pstart = pend - padded
    dest_row = (pstart[e_flat] + rank).astype(jnp.int32)
    R = ((A + N_EXPERTS * (tm - 1)) + tm - 1) // tm * tm
    row_token = jnp.zeros((R,), jnp.int32).at[dest_row].set(jnp.arange(A, dtype=jnp.int32) // TOP_K_EXPERTS)
    tile_start = jnp.arange(R // tm, dtype=jnp.int32) * tm
    tile_expert = jnp.minimum(jnp.sum(tile_start[:, None] >= pend[None, :], axis=1), N_EXPERTS - 1).astype(jnp.int32)
    n_used = (pend[-1] // tm).astype(jnp.int32).reshape(1)
    return dest_row, row_token, tile_expert, n_used


SKIP_IDX = IDX_DIM + IDX_HEADS
SKIP_GDN = SKIP_IDX + 2 * GDN_HEADS


def _regroup_small_kernel(a_ref, b_ref, o_ref):
    rows = jnp.concatenate([a_ref[:SKIP_IDX], b_ref[SKIP_IDX:SKIP_GDN],
                            jnp.zeros((LANES - SKIP_GDN, a_ref.shape[1]), jnp.float32)], axis=0)
    o_ref[...] = rows.T.astype(o_ref.dtype)


def _in_proj_kernel(h_ref, a_ref, b_ref, *refs, tn, kc, nj, ncast):
    j, i = pl.program_id(0), pl.program_id(1)
    o_ref, w_even, w_odd, stage = refs[ncast], refs[2 * ncast + 1], refs[2 * ncast + 2], refs[2 * ncast + 3]
    jj = jnp.minimum(j, nj - 1)
    skip = jnp.where(jj < OFF_GQ // tn, 0, jnp.where(jj < OFF_GATES // tn, SKIP_IDX, SKIP_GDN))

    def background(w_next):
        stage[:tn] = a_ref[...]
        stage[tn:] = b_ref[...]
        window = stage[pl.ds(pl.multiple_of(skip, 8), tn), :]
        w_next[pl.ds(pl.multiple_of(i * kc, kc), kc), :] = window.T.astype(w_next.dtype)
        for x_ref, y_ref in zip(refs[:ncast], refs[ncast + 1:2 * ncast + 1]):
            y_ref[...] = x_ref[...].astype(y_ref.dtype)

    @pl.when(j == 0)
    def _():
        background(w_even)
        o_ref[...] = jnp.zeros(o_ref.shape, o_ref.dtype)

    @pl.when(j % 2 == 1)
    def _():
        background(w_odd)
        o_ref[...] = jnp.dot(h_ref[...], w_even[...], preferred_element_type=jnp.float32).astype(o_ref.dtype)

    @pl.when((j > 0) & (j % 2 == 0))
    def _():
        background(w_even)
        o_ref[...] = jnp.dot(h_ref[...], w_odd[...], preferred_element_type=jnp.float32).astype(o_ref.dtype)


def in_proj_regrouping(h, w, to_cast, D, tm=1024, tn=1024):
    T, K = h.shape
    n_big = OFF_GATES + 2 * D
    tm, tn = _tile(T, tm), _tile(n_big, tn)
    ni, nj = T // tm, n_big // tn
    kc = K // ni
    assert K % ni == 0 and kc % LANES == 0 and tn % LANES == 0 and w.shape[1] == n_big + SKIP_GDN
    assert OFF_GQ % tn == 0 and OFF_GATES % tn == 0 and SKIP_GDN <= LANES and SKIP_IDX % 8 == 0 and SKIP_GDN % 8 == 0
    wt = w.T
    nsteps = (nj + 1) * ni
    cast_in, cast_out, cast_shapes = [], [], []
    for x in to_cast:
        rows, cols = x.shape
        rb = _cast_block_rows(rows, nsteps)
        imap = functools.partial(lambda j, i, last: (jnp.minimum(j * ni + i, last), 0), last=rows // rb - 1)
        cast_in.append(pl.BlockSpec((rb, cols), imap))
        cast_out.append(pl.BlockSpec((rb, cols), imap))
        cast_shapes.append(jax.ShapeDtypeStruct((rows, cols), jnp.bfloat16))
    src = lambda j: jnp.minimum(j, nj - 1)
    outs = pl.pallas_call(
        functools.partial(_in_proj_kernel, tn=tn, kc=kc, nj=nj, ncast=len(to_cast)),
        out_shape=[jax.ShapeDtypeStruct((T, n_big), jnp.bfloat16)] + cast_shapes,
        grid=(nj + 1, ni),
        in_specs=[pl.BlockSpec((tm, K), lambda j, i: (i, 0)),
                  pl.BlockSpec((tn, kc), lambda j, i: (src(j), i)),
                  pl.BlockSpec((LANES, kc), lambda j, i: ((src(j) + 1) * (tn // LANES), i))] + cast_in,
        out_specs=[pl.BlockSpec((tm, tn), lambda j, i: (jnp.where(j == 0, 0, i), jnp.maximum(j - 1, 0)))] + cast_out,
        scratch_shapes=[pltpu.VMEM((K, tn), jnp.bfloat16), pltpu.VMEM((K, tn), jnp.bfloat16),
                        pltpu.VMEM((tn + LANES, kc), jnp.float32)],
        compiler_params=pltpu.CompilerParams(dimension_semantics=("arbitrary", "arbitrary"),
                                             vmem_limit_bytes=IN_PROJ_VMEM_LIMIT),
        name="in_proj",
    )(h, wt, wt, *to_cast)
    return outs[0], outs[1:]


def small_group_weights(w, D, t=512):
    assert OFF_GQ % LANES == 0 and OFF_GATES % LANES == 0
    t = _tile(D, t)
    wt = w.T
    return pl.pallas_call(
        _regroup_small_kernel,
        out_shape=jax.ShapeDtypeStruct((D, LANES), jnp.bfloat16),
        grid=(D // t,),
        in_specs=[pl.BlockSpec((LANES, t), lambda k: (OFF_GQ // LANES, k)),
                  pl.BlockSpec((LANES, t), lambda k: (OFF_GATES // LANES, k))],
        out_specs=pl.BlockSpec((t, LANES), lambda k: (k, 0)),
        compiler_params=_cparams(("parallel",)),
        name="regroup_w_small",
    )(wt, wt)


def kernel(x, positions, mix_norm_w, w_in, q_norm_w, k_norm_w, idx_k_norm_w, conv_w, a_log, dt_bias, gdn_norm_w, w_proj_attn, w_proj_gdn, w_out, ffn_norm_w, w_router_group, b_router_group, w_router_expert, b_router_expert, w_gate, w_up, w_down):
    B, S, D = x.shape
    T = B * S
    bf16 = jnp.bfloat16
    moe_tm = 256 if T * TOP_K_EXPERTS >= 256 * N_EXPERTS else 64
    xt = x.reshape(T, D)
    cos_t, sin_t = rope_tables(positions)
    for l in range(w_in.shape[0]):
        w_small = small_group_weights(w_in[l], D)
        h = rmsnorm(xt, mix_norm_w[l], bf16)
        NE, _, FF = w_gate[l].shape
        later_weights = [w_proj_attn[l], w_proj_gdn[l], w_out[l], w_gate[l].reshape(NE * D, FF),
                         w_up[l].reshape(NE * D, FF), w_down[l].reshape(NE * FF, D)]
        P, (wpa, wpg, wo, wg, wu, wd) = in_proj_regrouping(h, w_in[l], later_weights, D)
        wg, wu, wd = wg.reshape(NE, D, FF), wu.reshape(NE, D, FF), wd.reshape(NE, FF, D)
        Psm = matmul(h, w_small, jnp.float32, name="in_proj_small")
        q, k, qi, ki, wi = attn_prep(P, Psm, cos_t, sin_t, q_norm_w[l], k_norm_w[l], idx_k_norm_w[l])
        bias = indexer_mask(qi, ki, wi, B, S)
        y_attn = masked_attention(q, k, P, bias, B, S)
        y_gdn = gated_deltanet(P, Psm, conv_w[l], a_log[l], dt_bias[l], gdn_norm_w[l], B, S)
        mixed = gated_merge(y_attn, y_gdn, wpa, wpg, P, D)
        x1 = out_proj_residual(mixed, wo, xt)
        w_router = jnp.zeros((D, LANES), jnp.float32)
        w_router = w_router.at[:, :N_GROUPS].set(w_router_group[l])
        w_router = w_router.at[:, RT_EXP:RT_EXP + N_EXPERTS].set(
            w_router_expert[l].transpose(1, 0, 2).reshape(D, N_EXPERTS))
        b_router = jnp.zeros((1, LANES), jnp.float32)
        b_router = b_router.at[0, :N_GROUPS].set(b_router_group[l])
        b_router = b_router.at[0, RT_EXP:RT_EXP + N_EXPERTS].set(b_router_expert[l].reshape(N_EXPERTS))
        h2, logits = ffn_norm_router(x1, ffn_norm_w[l], w_router)
        eid_l, wt_l = routing(logits, b_router)
        dest_row, row_token, tile_expert, n_used = moe_dispatch_plan(eid_l[:, :TOP_K_EXPERTS], moe_tm)
        ys = moe_ffn(h2, tile_expert, n_used, row_token, wg, wu, wd, moe_tm)
        xt = moe_combine(x1, wt_l, ys, dest_row)
    return xt.reshape(B, S, D)
```

```python
import functools
import math

import jax
import jax.numpy as jnp
from jax import lax
from jax.experimental import pallas as pl
from jax.experimental.pallas import tpu as pltpu

ATT_HEADS = 16
ATT_KV_HEADS = 4
HEAD_DIM = 128
IDX_HEADS = 16
IDX_DIM = 64
TOPK_MAX = 256
ROPE_THETA = 10000.0
GDN_HEADS = 16
GDN_DK = 128
GDN_DV = 128
CONV_WIDTH = 4
CHUNK = 64
N_GROUPS = 4
EXPERTS_PER_GROUP = 8
N_EXPERTS = N_GROUPS * EXPERTS_PER_GROUP
TOP_K_EXPERTS = 2
EPS = 1e-6

ATT_Q_WIDTH = ATT_HEADS * HEAD_DIM
ATT_KV_WIDTH = ATT_KV_HEADS * HEAD_DIM
IDX_Q_WIDTH = IDX_HEADS * IDX_DIM
GDN_WIDTH = GDN_HEADS * GDN_DK

LANES = 128
BF16_SUBLANES = 16
VMEM_LIMIT = 56 * 1024 * 1024
IN_PROJ_VMEM_LIMIT = 60 * 1024 * 1024
NEG_BIG = -1e30
LOG2E = math.log2(math.e)
SCORE_MASKED = 3.0e38
BISECT_MAX_STEPS = 192
BISECT_UNROLL = 4
COUNT_ROWS = 128

OFF_AQ = 0
OFF_AK = OFF_AQ + ATT_Q_WIDTH
OFF_AV = OFF_AK + ATT_KV_WIDTH
OFF_IQ = OFF_AV + ATT_KV_WIDTH
OFF_GQ = OFF_IQ + IDX_Q_WIDTH
OFF_GK = OFF_GQ + GDN_WIDTH
OFF_GV = OFF_GK + GDN_WIDTH
OFF_GZ = OFF_GV + GDN_WIDTH
OFF_GATES = OFF_GZ + GDN_WIDTH
SM_IK = 0
SM_IW = SM_IK + IDX_DIM
SM_GA = SM_IW + IDX_HEADS
SM_GB = SM_GA + GDN_HEADS
RT_EXP = 8


def _cparams(sem):
    return pltpu.CompilerParams(dimension_semantics=sem, vmem_limit_bytes=VMEM_LIMIT)


def _tile(n, pref):
    t = min(n, pref)
    assert n % t == 0, (n, pref)
    return t


def _rmsnorm_kernel(x_ref, w_ref, o_ref):
    x = x_ref[...]
    ms = jnp.mean(x * x, axis=-1, keepdims=True)
    o_ref[...] = (x * lax.rsqrt(ms + EPS) * w_ref[...]).astype(o_ref.dtype)


def rmsnorm(x, w, out_dtype, tm=256):
    T, D = x.shape
    tm = _tile(T, tm)
    return pl.pallas_call(
        _rmsnorm_kernel,
        out_shape=jax.ShapeDtypeStruct((T, D), out_dtype),
        grid=(T // tm,),
        in_specs=[pl.BlockSpec((tm, D), lambda i: (i, 0)), pl.BlockSpec((1, D), lambda i: (0, 0))],
        out_specs=pl.BlockSpec((tm, D), lambda i: (i, 0)),
        compiler_params=_cparams(("parallel",)),
        name="rmsnorm",
    )(x, w.reshape(1, D))


def _matmul_kernel(a_ref, b_ref, o_ref):
    o_ref[...] = jnp.dot(a_ref[...], b_ref[...], preferred_element_type=jnp.float32).astype(o_ref.dtype)


def matmul(a, b, out_dtype, tm=1024, tn=1024, name="matmul"):
    M, K = a.shape
    _, N = b.shape
    tm, tn = _tile(M, tm), _tile(N, tn)
    return pl.pallas_call(
        _matmul_kernel,
        out_shape=jax.ShapeDtypeStruct((M, N), out_dtype),
        grid=(N // tn, M // tm),
        in_specs=[pl.BlockSpec((tm, K), lambda j, i: (i, 0)), pl.BlockSpec((K, tn), lambda j, i: (0, j))],
        out_specs=pl.BlockSpec((tm, tn), lambda j, i: (i, j)),
        compiler_params=_cparams(("parallel", "parallel")),
        name=name,
    )(a, b)


def _cast_block_rows(rows, nsteps):
    for rb in range(BF16_SUBLANES, rows + 1, BF16_SUBLANES):
        if rows % rb == 0 and rows // rb <= nsteps:
            return rb
    raise ValueError((rows, nsteps))


def _rope_table_kernel(pos_ref, inv_ref, sgn_ref, cos_ref, sin_ref):
    ang = pos_ref[...] * inv_ref[...]
    cos_ref[...] = jnp.cos(ang)
    sin_ref[...] = jnp.sin(ang) * sgn_ref[...]


def rope_tables(positions):
    T = positions.size
    pos = positions.reshape(T, 1).astype(jnp.float32)

    def inv(d):
        return ROPE_THETA ** (-jnp.arange(0, d, 2, dtype=jnp.float32) / d)

    i128, i64 = inv(HEAD_DIM), inv(IDX_DIM)
    inv_row = jnp.concatenate([i128, i128, i64, i64, i64, i64]).reshape(1, 2 * LANES)
    s128 = jnp.concatenate([-jnp.ones(HEAD_DIM // 2), jnp.ones(HEAD_DIM // 2)])
    s64 = jnp.concatenate([-jnp.ones(IDX_DIM // 2), jnp.ones(IDX_DIM // 2)])
    sgn_row = jnp.concatenate([s128, s64, s64]).astype(jnp.float32).reshape(1, 2 * LANES)
    tm = _tile(T, 512)
    spec = pl.BlockSpec((tm, 2 * LANES), lambda i: (i, 0))
    row = pl.BlockSpec((1, 2 * LANES), lambda i: (0, 0))
    return pl.pallas_call(
        _rope_table_kernel,
        out_shape=[jax.ShapeDtypeStruct((T, 2 * LANES), jnp.float32)] * 2,
        grid=(T // tm,),
        in_specs=[pl.BlockSpec((tm, 1), lambda i: (i, 0)), row, row],
        out_specs=[spec, spec],
        compiler_params=_cparams(("parallel",)),
        name="rope_tables",
    )(pos, inv_row, sgn_row)


def _rope128(x, cos, sin_signed):
    return x * cos + pltpu.roll(x, HEAD_DIM // 2, 1) * sin_signed


def _rope64x2(x, cos, sin_signed, lane):
    half = IDX_DIM // 2
    first = (lane % IDX_DIM) < half
    partner = jnp.where(first, pltpu.roll(x, LANES - half, 1), pltpu.roll(x, half, 1))
    return x * cos + partner * sin_signed


def _attn_prep_kernel(aq_ref, ak_ref, iq_ref, sm_ref, cos_ref, sin_ref, qw_ref, kw_ref, ikw_ref,
                      q_ref, k_ref, qi_ref, ki_ref, wi_ref):
    cos_a, sin_a = cos_ref[:, :LANES], sin_ref[:, :LANES]
    cos_i, sin_i = cos_ref[:, LANES:], sin_ref[:, LANES:]
    tm = cos_a.shape[0]
    lane = lax.broadcasted_iota(jnp.int32, (tm, LANES), 1)

    def head_norm(xh, w):
        ms = jnp.mean(xh * xh, axis=-1, keepdims=True)
        return xh * lax.rsqrt(ms + EPS) * w

    for h in range(ATT_HEADS):
        sl = slice(h * HEAD_DIM, (h + 1) * HEAD_DIM)
        xh = head_norm(aq_ref[:, sl].astype(jnp.float32), qw_ref[...])
        q_ref[:, sl] = (_rope128(xh, cos_a, sin_a) * (LOG2E * HEAD_DIM ** -0.5)).astype(q_ref.dtype)
    for h in range(ATT_KV_HEADS):
        sl = slice(h * HEAD_DIM, (h + 1) * HEAD_DIM)
        xh = head_norm(ak_ref[:, sl].astype(jnp.float32), kw_ref[...])
        k_ref[:, sl] = _rope128(xh, cos_a, sin_a).astype(k_ref.dtype)
    for p in range(IDX_Q_WIDTH // LANES):
        sl = slice(p * LANES, (p + 1) * LANES)
        xp = iq_ref[:, sl].astype(jnp.float32)
        qi_ref[:, sl] = (_rope64x2(xp, cos_i, sin_i, lane) * (IDX_DIM ** -0.5)).astype(qi_ref.dtype)
    sm = sm_ref[...]
    in_k = lane < IDX_DIM
    xk = jnp.where(in_k, sm, 0.0)
    ms = jnp.sum(xk * xk, axis=-1, keepdims=True) * (1.0 / IDX_DIM)
    kn = xk * lax.rsqrt(ms + EPS) * ikw_ref[...]
    kr = jnp.where(in_k, _rope64x2(kn, cos_i, sin_i, lane), 0.0)
    ki_ref[:, :LANES] = kr.astype(ki_ref.dtype)
    ki_ref[:, LANES:] = pltpu.roll(kr, IDX_DIM, 1).astype(ki_ref.dtype)
    wi_ref[...] = sm * (IDX_HEADS ** -0.5)


def attn_prep(P, Psm, cos_t, sin_t, q_norm_w, k_norm_w, idx_k_norm_w, tm=256):
    T = P.shape[0]
    tm = _tile(T, tm)
    ikw = jnp.concatenate([idx_k_norm_w, jnp.zeros((LANES - IDX_DIM,), jnp.float32)]).reshape(1, LANES)
    row = lambda w: pl.BlockSpec((1, w), lambda i: (0, 0))
    return pl.pallas_call(
        _attn_prep_kernel,
        out_shape=[jax.ShapeDtypeStruct((T, ATT_Q_WIDTH), jnp.bfloat16),
                   jax.ShapeDtypeStruct((T, ATT_KV_WIDTH), jnp.bfloat16),
                   jax.ShapeDtypeStruct((T, IDX_Q_WIDTH), jnp.bfloat16),
                   jax.ShapeDtypeStruct((T, 2 * LANES), jnp.bfloat16),
                   jax.ShapeDtypeStruct((T, LANES), jnp.float32)],
        grid=(T // tm,),
        in_specs=[pl.BlockSpec((tm, ATT_Q_WIDTH), lambda i: (i, OFF_AQ // ATT_Q_WIDTH)),
                  pl.BlockSpec((tm, ATT_KV_WIDTH), lambda i: (i, OFF_AK // ATT_KV_WIDTH)),
                  pl.BlockSpec((tm, IDX_Q_WIDTH), lambda i: (i, OFF_IQ // IDX_Q_WIDTH)),
                  pl.BlockSpec((tm, LANES), lambda i: (i, 0)),
                  pl.BlockSpec((tm, 2 * LANES), lambda i: (i, 0)),
                  pl.BlockSpec((tm, 2 * LANES), lambda i: (i, 0)),
                  row(LANES), row(LANES), row(LANES)],
        out_specs=[pl.BlockSpec((tm, ATT_Q_WIDTH), lambda i: (i, 0)),
                   pl.BlockSpec((tm, ATT_KV_WIDTH), lambda i: (i, 0)),
                   pl.BlockSpec((tm, IDX_Q_WIDTH), lambda i: (i, 0)),
                   pl.BlockSpec((tm, 2 * LANES), lambda i: (i, 0)),
                   pl.BlockSpec((tm, LANES), lambda i: (i, 0))],
        compiler_params=_cparams(("parallel",)),
        name="attn_prep",
    )(P, P, P, Psm, cos_t, sin_t, q_norm_w.reshape(1, LANES), k_norm_w.reshape(1, LANES), ikw)


def _lane_fold(x, op):
    s = x[:, :LANES]
    for j in range(1, x.shape[1] // LANES):
        s = op(s, x[:, j * LANES:(j + 1) * LANES])
    return s


def _indexer_kernel(qi_ref, ki_ref, wi_ref, bias_ref, key_ref, *, n_sel, tq, tk, nchunks):
    q0 = pl.program_id(1) * tq
    nck = (q0 + tq + tk - 1) // tk
    wi = wi_ref[...]
    rowpos = q0 + lax.broadcasted_iota(jnp.int32, (tq, tk), 0)
    colpos0 = lax.broadcasted_iota(jnp.int32, (tq, tk), 1)

    def score_chunk(c, carry):
        mx, mn = carry
        ks = ki_ref[pl.ds(pl.multiple_of(c * tk, tk), tk), :]
        acc = jnp.zeros((tq, tk), jnp.float32)
        for h in range(IDX_HEADS):
            qp = qi_ref[:, (h // 2) * LANES:(h // 2 + 1) * LANES]
            kh = ks[:, (h % 2) * LANES:(h % 2 + 1) * LANES]
            d = lax.dot_general(qp, kh, (((1,), (1,)), ((), ())), preferred_element_type=jnp.float32)
            acc = acc + wi[:, SM_IW + h:SM_IW + h + 1] * jnp.maximum(d, 0.0)
        causal = colpos0 + c * tk <= rowpos
        key_ref[c] = jnp.where(causal, acc, -SCORE_MASKED)
        mx = jnp.maximum(mx, _lane_fold(jnp.where(causal, acc, -SCORE_MASKED), jnp.maximum))
        mn = jnp.minimum(mn, _lane_fold(jnp.where(causal, acc, SCORE_MASKED), jnp.minimum))
        return mx, mn

    mx, mn = lax.fori_loop(0, nck, score_chunk, (jnp.full((tq, LANES), -SCORE_MASKED, jnp.float32),
                                                 jnp.full((tq, LANES), SCORE_MASKED, jnp.float32)))

    ones = jnp.ones((LANES, LANES), jnp.bfloat16)

    def count_ge(probe):
        accs = []
        for r0 in range(0, tq, COUNT_ROWS):
            rows = slice(r0, min(r0 + COUNT_ROWS, tq))

            def body(c, acc, rows=rows):
                for j in range(tk // LANES):
                    acc = acc + jnp.where(key_ref[c, rows, j * LANES:(j + 1) * LANES] >= probe[rows], 1.0, 0.0)
                return acc

            accs.append(lax.fori_loop(0, nck, body, jnp.zeros((rows.stop - rows.start, LANES), jnp.float32)))
        acc = accs[0] if len(accs) == 1 else jnp.concatenate(accs, axis=0)
        return jnp.dot(acc.astype(jnp.bfloat16), ones, preferred_element_type=jnp.float32)

    lo = jnp.broadcast_to(jnp.min(mn, axis=1, keepdims=True), (tq, LANES))
    hi = jnp.broadcast_to(jnp.max(mx, axis=1, keepdims=True), (tq, LANES))
    ncausal = q0 + lax.broadcasted_iota(jnp.int32, (tq, LANES), 0) + 1
    hi = jnp.where(ncausal <= n_sel, lo, hi)

    def unfinished(carry):
        return (carry[0] < BISECT_MAX_STEPS) & (carry[3] > 0)

    def bisect(carry):
        it, lo, hi, _ = carry
        for _ in range(BISECT_UNROLL):
            mid = 0.5 * lo + 0.5 * hi
            cnt = count_ge(mid)
            ok = cnt >= n_sel
            open_row = (mid > lo) & (mid < hi)
            lo = jnp.where(ok, mid, lo)
            hi = jnp.where(cnt == n_sel, mid, jnp.where(ok, hi, mid))
            open_row = open_row & (hi > lo)
        return it + BISECT_UNROLL, lo, hi, jnp.sum(jnp.where(open_row, 1, 0))

    _, thr, _, _ = lax.while_loop(unfinished, bisect, (jnp.int32(0), lo, hi, jnp.int32(1)))

    def write_chunk(c, carry):
        for j in range(tk // LANES):
            cols = slice(j * LANES, (j + 1) * LANES)
            bias_ref[c, :, cols] = jnp.where(key_ref[c, :, cols] >= thr, 0.0, NEG_BIG).astype(bias_ref.dtype)
        return carry

    def write_masked(c, carry):
        bias_ref[c] = jnp.full((tq, tk), NEG_BIG, bias_ref.dtype)
        return carry

    lax.fori_loop(0, nck, write_chunk, 0)
    lax.fori_loop(nck, nchunks, write_masked, 0)


def indexer_mask(qi, ki, wi, B, S, tq=512, tk=512):
    tq, tk = _tile(S, tq), _tile(S, tk)
    n_sel = min(TOPK_MAX, S // 4)
    nq, nchunks = S // tq, S // tk
    return pl.pallas_call(
        functools.partial(_indexer_kernel, n_sel=n_sel, tq=tq, tk=tk, nchunks=nchunks),
        out_shape=jax.ShapeDtypeStruct((B, nq, nchunks, tq, tk), jnp.bfloat16),
        grid=(B, nq),
        in_specs=[pl.BlockSpec((tq, IDX_Q_WIDTH), lambda b, i: (b * nq + i, 0)),
                  pl.BlockSpec((S, 2 * LANES), lambda b, i: (b, 0)),
                  pl.BlockSpec((tq, LANES), lambda b, i: (b * nq + i, 0))],
        out_specs=pl.BlockSpec((None, None, nchunks, tq, tk), lambda b, i: (b, i, 0, 0, 0)),
        scratch_shapes=[pltpu.VMEM((nchunks, tq, tk), jnp.float32)],
        compiler_params=_cparams(("parallel", "parallel")),
        name="indexer_mask",
    )(qi, ki, wi)


def _attn_kernel(q_ref, k_ref, v_ref, bias_ref, o_ref, m_ref, acc_ref, *, tq, tk):
    i, j = pl.program_id(1), pl.program_id(2)
    grp = ATT_HEADS // ATT_KV_HEADS

    @pl.when(j == 0)
    def _():
        m_ref[...] = jnp.full(m_ref.shape, -1e38, jnp.float32)
        acc_ref[...] = jnp.zeros(acc_ref.shape, jnp.float32)

    @pl.when(j * tk <= i * tq + tq - 1)
    def _():
        bias = bias_ref[...].astype(jnp.float32)
        ones = jnp.ones((tk, LANES), v_ref.dtype)
        v1 = [jnp.concatenate([v_ref[:, g * HEAD_DIM:(g + 1) * HEAD_DIM], ones], axis=1) for g in range(ATT_KV_HEADS)]
        s, m_new, alpha, p = {}, {}, {}, {}

        def scores(h):
            g = h // grp
            qh = q_ref[:, h * HEAD_DIM:(h + 1) * HEAD_DIM]
            kh = k_ref[:, g * HEAD_DIM:(g + 1) * HEAD_DIM]
            s[h] = lax.dot_general(qh, kh, (((1,), (1,)), ((), ())), preferred_element_type=jnp.float32) + bias
            m_prev = m_ref[h]
            m_new[h] = jnp.maximum(m_prev, jnp.max(s[h], axis=1, keepdims=True))
            alpha[h] = jnp.exp2(m_prev - m_new[h])
            m_ref[h] = m_new[h]

        def probs(h):
            p[h] = jnp.exp2(s.pop(h) - m_new.pop(h)[:, :1]).astype(v_ref.dtype)

        def values(h):
            a2 = jnp.concatenate([alpha[h], alpha.pop(h)], axis=1)
            acc_ref[h] = a2 * acc_ref[h] + jnp.dot(p.pop(h), v1[h // grp], preferred_element_type=jnp.float32)

        for t in range(ATT_HEADS + 2):
            if t < ATT_HEADS:
                scores(t)
            if 0 <= t - 1 < ATT_HEADS:
                probs(t - 1)
            if 0 <= t - 2 < ATT_HEADS:
                values(t - 2)

    @pl.when(j == pl.num_programs(2) - 1)
    def _():
        for h in range(ATT_HEADS):
            acc = acc_ref[h]
            o_ref[:, h * HEAD_DIM:(h + 1) * HEAD_DIM] = (acc[:, :HEAD_DIM] / acc[:, HEAD_DIM:]).astype(o_ref.dtype)


def masked_attention(q, k, P, bias, B, S):
    _, nq, nkv, tq, tk = bias.shape

    def kv_idx(i, j):
        return jnp.minimum(j, (i * tq + tq - 1) // tk)

    return pl.pallas_call(
        functools.partial(_attn_kernel, tq=tq, tk=tk),
        out_shape=jax.ShapeDtypeStruct((B * S, ATT_Q_WIDTH), jnp.bfloat16),
        grid=(B, nq, nkv),
        in_specs=[pl.BlockSpec((tq, ATT_Q_WIDTH), lambda b, i, j: (b * nq + i, 0)),
                  pl.BlockSpec((tk, ATT_KV_WIDTH), lambda b, i, j: (b * nkv + kv_idx(i, j), 0)),
                  pl.BlockSpec((tk, ATT_KV_WIDTH), lambda b, i, j: (b * nkv + kv_idx(i, j), OFF_AV // ATT_KV_WIDTH)),
                  pl.BlockSpec((None, None, None, tq, tk), lambda b, i, j: (b, i, kv_idx(i, j), 0, 0))],
        out_specs=pl.BlockSpec((tq, ATT_Q_WIDTH), lambda b, i, j: (b * nq + i, 0)),
        scratch_shapes=[pltpu.VMEM((ATT_HEADS, tq, LANES), jnp.float32),
                        pltpu.VMEM((ATT_HEADS, tq, HEAD_DIM + LANES), jnp.float32)],
        compiler_params=_cparams(("parallel", "parallel", "arbitrary")),
        name="masked_attention",
    )(q, k, P, bias)


GDN_HALO = BF16_SUBLANES


def _gdn_kernel(xq_ref, xk_ref, xv_ref, z_ref, sm_ref, cw_ref, alog_ref, dtb_ref, nw_ref, o_ref, state_ref, prev_ref):
    C = CHUNK
    f32, bf16 = jnp.float32, jnp.bfloat16
    heads = range(GDN_HEADS)

    @pl.when(pl.program_id(1) == 0)
    def _():
        state_ref[...] = jnp.zeros(state_ref.shape, f32)
        prev_ref[...] = jnp.zeros(prev_ref.shape, prev_ref.dtype)

    srow = lax.broadcasted_iota(jnp.int32, (CONV_WIDTH * C, GDN_HALO + C), 0)
    scol = lax.broadcasted_iota(jnp.int32, (CONV_WIDTH * C, GDN_HALO + C), 1)
    select = (scol == (srow & (C - 1)) + (srow >> (C.bit_length() - 1)) + (GDN_HALO - (CONV_WIDTH - 1))).astype(bf16)

    def conv_silu(x_ref, part):
        x = x_ref[...]
        xp = jnp.concatenate([prev_ref[part], x], axis=0)
        prev_ref[part] = x[C - GDN_HALO:]
        taps = jnp.dot(select, xp, preferred_element_type=f32)
        w = cw_ref[:, part * GDN_WIDTH:(part + 1) * GDN_WIDTH]
        y = None
        for jj in range(CONV_WIDTH):
            term = w[jj:jj + 1, :] * taps[jj * C:(jj + 1) * C, :]
            y = term if y is None else y + term
        return y * jax.nn.sigmoid(y)

    def l2(yh):
        return yh * lax.rsqrt(jnp.sum(yh * yh, axis=-1, keepdims=True) + EPS)

    yq, yk, yv = conv_silu(xq_ref, 0), conv_silu(xk_ref, 1), conv_silu(xv_ref, 2)
    sm = sm_ref[...]
    a_in = sm + dtb_ref[...]
    softplus = jnp.maximum(a_in, 0.0) + jnp.log(1.0 + jnp.exp(-jnp.abs(a_in)))
    g_all = -jnp.exp(alog_ref[...]) * softplus
    beta_all = jax.nn.sigmoid(sm)

    row = lax.broadcasted_iota(jnp.int32, (C, C), 0)
    col = lax.broadcasted_iota(jnp.int32, (C, C), 1)
    tril, strict = row >= col, row > col
    eye = (row == col).astype(f32)
    hi = lax.Precision.HIGHEST
    gc_col = jnp.dot(tril.astype(f32), g_all, precision=hi, preferred_element_type=f32)
    gc_row = jnp.dot(g_all.T, (row <= col).astype(f32), precision=hi, preferred_element_type=f32)
    nt = (((1,), (1,)), ((), ()))
    dot = functools.partial(jnp.dot, preferred_element_type=f32)
    kq, a, intra, rhs, qd, kd, eglast = [], [], [], [], [], [], []
    for j in heads:
        sl = slice(j * GDN_DK, (j + 1) * GDN_DK)
        gc = gc_col[:, SM_GA + j:SM_GA + j + 1]
        glast = gc[C - 1:C, :]
        eg = jnp.exp(gc)
        beta = beta_all[:, SM_GB + j:SM_GB + j + 1]
        q = (l2(yq[:, sl]) * (GDN_DK ** -0.5)).astype(bf16)
        k = l2(yk[:, sl]).astype(bf16)
        v = yv[:, sl].astype(bf16)
        kf = k.astype(f32)
        kq.append(jnp.concatenate([k, q], axis=0))
        rhs.append(jnp.concatenate([(v.astype(f32) * beta).astype(bf16), (kf * (beta * eg)).astype(bf16)], axis=1))
        qd.append((q.astype(f32) * eg).astype(bf16))
        kd.append((kf * jnp.exp(glast - gc)).astype(bf16))
        eglast.append(jnp.exp(glast))
    skq = [lax.dot_general(kq[j], kq[j][:C], nt, preferred_element_type=f32) for j in heads]
    for j in heads:
        gc = gc_col[:, SM_GA + j:SM_GA + j + 1]
        decay = jnp.exp(jnp.where(tril, gc - gc_row[SM_GA + j:SM_GA + j + 1, :], NEG_BIG))
        a.append(jnp.where(strict, skq[j][:C] * beta_all[:, SM_GB + j:SM_GB + j + 1] * decay, 0.0))
        intra.append((skq[j][C:] * decay).astype(bf16))
    ab = [a[j].astype(bf16) for j in heads]
    xb = [dot(ab[j], ab[j]).astype(bf16) for j in heads]
    tinv = [eye - a[j] for j in heads]
    for it in range(5):
        last = it == 4
        lhs = [tinv[j].astype(bf16) if last else jnp.concatenate([tinv[j].astype(bf16), xb[j]], axis=0) for j in heads]
        prod = [dot(lhs[j], xb[j]) for j in heads]
        tinv = [tinv[j] + prod[j][:C] for j in heads]
        if not last:
            xb = [prod[j][C:].astype(bf16) for j in heads]
    uw = [dot(tinv[j].astype(bf16), rhs[j]) for j in heads]
    sb = [state_ref[j].astype(bf16) for j in heads]
    ws = [dot(jnp.concatenate([uw[j][:, GDN_DV:].astype(bf16), qd[j]], axis=0), sb[j]) for j in heads]
    vb = [(uw[j][:, :GDN_DV] - ws[j][:C]).astype(bf16) for j in heads]
    o = [ws[j][C:] + dot(intra[j], vb[j]) for j in heads]
    upd = [lax.dot_general(kd[j], vb[j], (((0,), (0,)), ((), ())), preferred_element_type=f32) for j in heads]
    for j in heads:
        sl = slice(j * GDN_DK, (j + 1) * GDN_DK)
        state_ref[j] = state_ref[j] * eglast[j] + upd[j]
        ms = jnp.mean(o[j] * o[j], axis=-1, keepdims=True)
        z = z_ref[:, sl].astype(f32)
        o_ref[:, sl] = (o[j] * lax.rsqrt(ms + EPS) * nw_ref[...] * (z * jax.nn.sigmoid(z))).astype(o_ref.dtype)


def gated_deltanet(P, Psm, conv_w, a_log, dt_bias, norm_w, B, S):
    T = B * S
    C, W = CHUNK, GDN_WIDTH
    n = S // C

    def pad_lane(v, off):
        return jnp.zeros((1, LANES), jnp.float32).at[0, off:off + v.shape[0]].set(v)

    blk = lambda off: pl.BlockSpec((C, W), lambda b, c: (b * n + c, off // W))
    row = lambda w: pl.BlockSpec((1, w), lambda b, c: (0, 0))
    return pl.pallas_call(
        _gdn_kernel,
        out_shape=jax.ShapeDtypeStruct((T, W), jnp.bfloat16),
        grid=(B, n),
        in_specs=[blk(OFF_GQ), blk(OFF_GK), blk(OFF_GV), blk(OFF_GZ),
                  pl.BlockSpec((C, LANES), lambda b, c: (b * n + c, 0)),
                  pl.BlockSpec((CONV_WIDTH, 3 * W), lambda b, c: (0, 0)),
                  row(LANES), row(LANES), row(GDN_DV)],
        out_specs=blk(0),
        scratch_shapes=[pltpu.VMEM((GDN_HEADS, GDN_DK, GDN_DV), jnp.float32),
                        pltpu.VMEM((3, GDN_HALO, W), jnp.bfloat16)],
        compiler_params=_cparams(("parallel", "arbitrary")),
        name="gated_deltanet",
    )(P, P, P, P, Psm, conv_w, pad_lane(a_log, SM_GA), pad_lane(dt_bias, SM_GA), norm_w.reshape(1, GDN_DV))


def _merge_kernel(ya_ref, yg_ref, wa_ref, wg_ref, ga_ref, gg_ref, o_ref):
    pa = jnp.dot(ya_ref[...], wa_ref[...], preferred_element_type=jnp.float32)
    pg = jnp.dot(yg_ref[...], wg_ref[...], preferred_element_type=jnp.float32)
    ga = jax.nn.sigmoid(ga_ref[...].astype(jnp.float32))
    gg = jax.nn.sigmoid(gg_ref[...].astype(jnp.float32))
    o_ref[...] = (ga * pa + gg * pg).astype(o_ref.dtype)


def gated_merge(ya, yg, wa, wg, P, D, tm=1024, tn=1024):
    T = ya.shape[0]
    tm, tn = _tile(T, tm), _tile(D, tn)
    goff = OFF_GATES // tn
    return pl.pallas_call(
        _merge_kernel,
        out_shape=jax.ShapeDtypeStruct((T, D), jnp.bfloat16),
        grid=(D // tn, T // tm),
        in_specs=[pl.BlockSpec((tm, ATT_Q_WIDTH), lambda j, i: (i, 0)),
                  pl.BlockSpec((tm, GDN_WIDTH), lambda j, i: (i, 0)),
                  pl.BlockSpec((ATT_Q_WIDTH, tn), lambda j, i: (0, j)),
                  pl.BlockSpec((GDN_WIDTH, tn), lambda j, i: (0, j)),
                  pl.BlockSpec((tm, tn), lambda j, i: (i, goff + j)),
                  pl.BlockSpec((tm, tn), lambda j, i: (i, goff + D // tn + j))],
        out_specs=pl.BlockSpec((tm, tn), lambda j, i: (i, j)),
        compiler_params=_cparams(("parallel", "parallel")),
        name="gated_merge",
    )(ya, yg, wa, wg, P, P)


def _outproj_kernel(a_ref, w_ref, x_ref, o_ref):
    o_ref[...] = x_ref[...] + jnp.dot(a_ref[...], w_ref[...], preferred_element_type=jnp.float32)


def out_proj_residual(a, w, x, tm=1024, tn=1024):
    T, D = x.shape
    K = a.shape[1]
    tm, tn = _tile(T, tm), _tile(D, tn)
    return pl.pallas_call(
        _outproj_kernel,
        out_shape=jax.ShapeDtypeStruct((T, D), jnp.float32),
        grid=(D // tn, T // tm),
        in_specs=[pl.BlockSpec((tm, K), lambda j, i: (i, 0)),
                  pl.BlockSpec((K, tn), lambda j, i: (0, j)),
                  pl.BlockSpec((tm, tn), lambda j, i: (i, j))],
        out_specs=pl.BlockSpec((tm, tn), lambda j, i: (i, j)),
        compiler_params=_cparams(("parallel", "parallel")),
        name="out_proj_residual",
    )(a, w, x)


def _ffn_norm_router_kernel(x_ref, w_ref, rhi_ref, rlo_ref, h_ref, lg_ref):
    x = x_ref[...]
    ms = jnp.mean(x * x, axis=-1, keepdims=True)
    h = x * lax.rsqrt(ms + EPS) * w_ref[...]
    h_ref[...] = h
    hh = h.astype(jnp.bfloat16)
    hl = (h - hh.astype(jnp.float32)).astype(jnp.bfloat16)
    f32 = jnp.float32
    lg_ref[...] = (jnp.dot(hh, rhi_ref[...], preferred_element_type=f32)
                   + jnp.dot(hh, rlo_ref[...], preferred_element_type=f32)
                   + jnp.dot(hl, rhi_ref[...], preferred_element_type=f32))


def ffn_norm_router(x1, norm_w, w_router, tm=256):
    T, D = x1.shape
    tm = _tile(T, tm)
    rhi = w_router.astype(jnp.bfloat16)
    rlo = (w_router - rhi.astype(jnp.float32)).astype(jnp.bfloat16)
    return pl.pallas_call(
        _ffn_norm_router_kernel,
        out_shape=[jax.ShapeDtypeStruct((T, D), jnp.float32), jax.ShapeDtypeStruct((T, LANES), jnp.float32)],
        grid=(T // tm,),
        in_specs=[pl.BlockSpec((tm, D), lambda i: (i, 0)), pl.BlockSpec((1, D), lambda i: (0, 0)),
                  pl.BlockSpec((D, LANES), lambda i: (0, 0)), pl.BlockSpec((D, LANES), lambda i: (0, 0))],
        out_specs=[pl.BlockSpec((tm, D), lambda i: (i, 0)), pl.BlockSpec((tm, LANES), lambda i: (i, 0))],
        compiler_params=_cparams(("parallel",)),
        name="ffn_norm_router",
    )(x1, norm_w.reshape(1, D), rhi, rlo)


def _routing_kernel(lg_ref, b_ref, eid_ref, wt_ref):
    lg = lg_ref[...] + b_ref[...]
    lane = lax.broadcasted_iota(jnp.int32, lg.shape, 1)
    ninf = -jnp.inf

    def first_argmax(vals, vmax):
        return jnp.min(jnp.where(vals == vmax, lane, LANES), axis=-1, keepdims=True)

    glog = jnp.where(lane < N_GROUPS, lg, ninf)
    gmax = jnp.max(glog, axis=-1, keepdims=True)
    p_grp = 1.0 / jnp.sum(jnp.exp(glog - gmax), axis=-1, keepdims=True)
    grp = first_argmax(glog, gmax)
    base = RT_EXP + grp * EXPERTS_PER_GROUP
    elog = jnp.where((lane >= base) & (lane < base + EXPERTS_PER_GROUP), lg, ninf)
    emax = jnp.max(elog, axis=-1, keepdims=True)
    idx1 = first_argmax(elog, emax)
    elog2 = jnp.where(lane == idx1, ninf, elog)
    emax2 = jnp.max(elog2, axis=-1, keepdims=True)
    idx2 = first_argmax(elog2, emax2)
    e2 = jnp.exp(emax2 - emax)
    w1 = p_grp / (1.0 + e2)
    w2 = p_grp * e2 / (1.0 + e2)
    eid_ref[...] = jnp.where(lane == 0, idx1 - RT_EXP, jnp.where(lane == 1, idx2 - RT_EXP, 0))
    wt_ref[...] = jnp.where(lane == 0, w1, jnp.where(lane == 1, w2, 0.0))


def routing(logits, bias_row, tm=512):
    T = logits.shape[0]
    tm = _tile(T, tm)
    spec = pl.BlockSpec((tm, LANES), lambda i: (i, 0))
    return pl.pallas_call(
        _routing_kernel,
        out_shape=[jax.ShapeDtypeStruct((T, LANES), jnp.int32), jax.ShapeDtypeStruct((T, LANES), jnp.float32)],
        grid=(T // tm,),
        in_specs=[spec, pl.BlockSpec((1, LANES), lambda i: (0, 0))],
        out_specs=[spec, spec],
        compiler_params=_cparams(("parallel",)),
        name="routing",
    )(logits, bias_row)


GATHER_UNROLL = 8


def _row_gather_copy(src_hbm, row, dst, r, sem):
    return pltpu.make_async_copy(src_hbm.at[pl.ds(row, 1), :], dst.at[pl.ds(r, 1), :], sem)


def _moe_ffn_kernel(texp_ref, nused_ref, rowtok_ref, h_hbm, wg_ref, wu_ref, wd_ref, o_ref,
                    xbuf, sem, *, tm):
    i = pl.program_id(0)
    nused = nused_ref[0]

    def start_gather(tile, slot):
        def body(r, carry):
            _row_gather_copy(h_hbm, rowtok_ref[tile * tm + r], xbuf.at[slot], r, sem.at[slot]).start()
            return carry
        lax.fori_loop(0, tm, body, 0, unroll=GATHER_UNROLL)

    def wait_gather(slot):
        pltpu.make_async_copy(h_hbm.at[pl.ds(0, tm), :], xbuf.at[slot], sem.at[slot]).wait()

    @pl.when((i == 0) & (nused > 0))
    def _():
        start_gather(0, 0)

    @pl.when(i + 1 < nused)
    def _():
        start_gather(i + 1, (i + 1) % 2)

    @pl.when(i < nused)
    def _():
        slot = i % 2
        wait_gather(slot)
        x = xbuf[slot].astype(jnp.bfloat16)
        g = jnp.dot(x, wg_ref[0], preferred_element_type=jnp.float32)
        u = jnp.dot(x, wu_ref[0], preferred_element_type=jnp.float32)
        hmid = (g * jax.nn.sigmoid(g) * u).astype(jnp.bfloat16)
        y = jnp.dot(hmid, wd_ref[0], preferred_element_type=jnp.float32)
        o_ref[...] = y

    @pl.when(i >= nused)
    def _():
        o_ref[...] = jnp.zeros(o_ref.shape, o_ref.dtype)


def moe_ffn(h2, tile_expert, n_used, row_token, wg, wu, wd, tm):
    T, D = h2.shape
    R = row_token.shape[0]
    FF = wg.shape[2]
    ntiles = R // tm
    grid_spec = pltpu.PrefetchScalarGridSpec(
        num_scalar_prefetch=3,
        grid=(ntiles,),
        in_specs=[pl.BlockSpec(memory_space=pl.ANY),
                  pl.BlockSpec((1, D, FF), lambda i, te, nu, rt: (te[i], 0, 0)),
                  pl.BlockSpec((1, D, FF), lambda i, te, nu, rt: (te[i], 0, 0)),
                  pl.BlockSpec((1, FF, D), lambda i, te, nu, rt: (te[i], 0, 0))],
        out_specs=pl.BlockSpec((tm, D), lambda i, te, nu, rt: (i, 0)),
        scratch_shapes=[pltpu.VMEM((2, tm, D), jnp.float32), pltpu.SemaphoreType.DMA((2,))],
    )
    return pl.pallas_call(
        functools.partial(_moe_ffn_kernel, tm=tm),
        out_shape=jax.ShapeDtypeStruct((R, D), jnp.float32),
        grid_spec=grid_spec,
        compiler_params=_cparams(("arbitrary",)),
        name="moe_ffn",
    )(tile_expert, n_used, row_token, h2, wg, wu, wd)


def _combine_kernel(dest_ref, x_ref, wt_ref, ys_hbm, o_ref, buf, sem, *, tm):
    i = pl.program_id(0)
    n = pl.num_programs(0)

    def start_gather(tile, slot):
        def body(r, carry):
            for kk in range(TOP_K_EXPERTS):
                row = dest_ref[(tile * tm + r) * TOP_K_EXPERTS + kk]
                _row_gather_copy(ys_hbm, row, buf.at[slot, kk], r, sem.at[slot]).start()
            return carry
        lax.fori_loop(0, tm, body, 0, unroll=GATHER_UNROLL)

    def wait_gather(slot):
        for kk in range(TOP_K_EXPERTS):
            pltpu.make_async_copy(ys_hbm.at[pl.ds(0, tm), :], buf.at[slot, kk], sem.at[slot]).wait()

    @pl.when(i == 0)
    def _():
        start_gather(0, 0)

    @pl.when(i + 1 < n)
    def _():
        start_gather(i + 1, (i + 1) % 2)

    slot = i % 2
    wait_gather(slot)
    wt = wt_ref[...]
    o_ref[...] = x_ref[...] + wt[:, 0:1] * buf[slot, 0] + wt[:, 1:2] * buf[slot, 1]


def moe_combine(x1, wt_lanes, ys, dest_row, tm=128):
    T, D = x1.shape
    tm = _tile(T, tm)
    grid_spec = pltpu.PrefetchScalarGridSpec(
        num_scalar_prefetch=1,
        grid=(T // tm,),
        in_specs=[pl.BlockSpec((tm, D), lambda i, d: (i, 0)), pl.BlockSpec((tm, LANES), lambda i, d: (i, 0)),
                  pl.BlockSpec(memory_space=pl.ANY)],
        out_specs=pl.BlockSpec((tm, D), lambda i, d: (i, 0)),
        scratch_shapes=[pltpu.VMEM((2, TOP_K_EXPERTS, tm, D), jnp.float32), pltpu.SemaphoreType.DMA((2,))],
    )
    return pl.pallas_call(
        functools.partial(_combine_kernel, tm=tm),
        out_shape=jax.ShapeDtypeStruct((T, D), jnp.float32),
        grid_spec=grid_spec,
        compiler_params=_cparams(("arbitrary",)),
        name="moe_combine",
    )(dest_row, x1, wt_lanes, ys)


PLAN_BLOCK = 128


def moe_dispatch_plan(eid, tm):
    T = eid.shape[0]
    A = T * TOP_K_EXPERTS
    e_flat = eid.reshape(A)
    blk = PLAN_BLOCK if A % PLAN_BLOCK == 0 else A
    onehot = (e_flat[:, None] == jnp.arange(N_EXPERTS, dtype=jnp.int32)[None, :]).astype(jnp.float32)
    oh = onehot.reshape(A // blk, blk, N_EXPERTS)
    before = jnp.tril(jnp.ones((blk, blk), jnp.float32), -1)
    within = jnp.einsum('ij,bje->bie', before, oh)
    totals = jnp.sum(oh, axis=1)
    offs = jnp.cumsum(totals, axis=0) - totals
    rank = jnp.sum((within + offs[:, None, :]) * oh, axis=2).reshape(A).astype(jnp.int32)
    counts = (offs[-1] + totals[-1]).astype(jnp.int32)
    padded = ((counts + tm - 1) // tm) * tm
    pend = jnp.cumsum(padded)
    pstart = pend - padded
    dest_row = (pstart[e_flat] + rank).astype(jnp.int32)
    R = ((A + N_EXPERTS * (tm - 1)) + tm - 1) // tm * tm
    row_token = jnp.zeros((R,), jnp.int32).at[dest_row].set(jnp.arange(A, dtype=jnp.int32) // TOP_K_EXPERTS)
    tile_start = jnp.arange(R // tm, dtype=jnp.int32) * tm
    tile_expert = jnp.minimum(jnp.sum(tile_start[:, None] >= pend[None, :], axis=1), N_EXPERTS - 1).astype(jnp.int32)
    n_used = (pend[-1] // tm).astype(jnp.int32).reshape(1)
    return dest_row, row_token, tile_expert, n_used


SKIP_IDX = IDX_DIM + IDX_HEADS
SKIP_GDN = SKIP_IDX + 2 * GDN_HEADS


def _regroup_small_kernel(a_ref, b_ref, o_ref):
    rows = jnp.concatenate([a_ref[:SKIP_IDX], b_ref[SKIP_IDX:SKIP_GDN],
                            jnp.zeros((LANES - SKIP_GDN, a_ref.shape[1]), jnp.float32)], axis=0)
    o_ref[...] = rows.T.astype(o_ref.dtype)


def _in_proj_kernel(h_ref, a_ref, b_ref, *refs, tn, kc, nj, ncast):
    j, i = pl.program_id(0), pl.program_id(1)
    o_ref, w_even, w_odd, stage = refs[ncast], refs[2 * ncast + 1], refs[2 * ncast + 2], refs[2 * ncast + 3]
    jj = jnp.minimum(j, nj - 1)
    skip = jnp.where(jj < OFF_GQ // tn, 0, jnp.where(jj < OFF_GATES // tn, SKIP_IDX, SKIP_GDN))

    def background(w_next):
        stage[:tn] = a_ref[...]
        stage[tn:] = b_ref[...]
        window = stage[pl.ds(pl.multiple_of(skip, 8), tn), :]
        w_next[pl.ds(pl.multiple_of(i * kc, kc), kc), :] = window.T.astype(w_next.dtype)
        for x_ref, y_ref in zip(refs[:ncast], refs[ncast + 1:2 * ncast + 1]):
            y_ref[...] = x_ref[...].astype(y_ref.dtype)

    @pl.when(j == 0)
    def _():
        background(w_even)
        o_ref[...] = jnp.zeros(o_ref.shape, o_ref.dtype)

    @pl.when(j % 2 == 1)
    def _():
        background(w_odd)
        o_ref[...] = jnp.dot(h_ref[...], w_even[...], preferred_element_type=jnp.float32).astype(o_ref.dtype)

    @pl.when((j > 0) & (j % 2 == 0))
    def _():
        background(w_even)
        o_ref[...] = jnp.dot(h_ref[...], w_odd[...], preferred_element_type=jnp.float32).astype(o_ref.dtype)


def in_proj_regrouping(h, w, to_cast, D, tm=1024, tn=1024):
    T, K = h.shape
    n_big = OFF_GATES + 2 * D
    tm, tn = _tile(T, tm), _tile(n_big, tn)
    ni, nj = T // tm, n_big // tn
    kc = K // ni
    assert K % ni == 0 and kc % LANES == 0 and tn % LANES == 0 and w.shape[1] == n_big + SKIP_GDN
    assert OFF_GQ % tn == 0 and OFF_GATES % tn == 0 and SKIP_GDN <= LANES and SKIP_IDX % 8 == 0 and SKIP_GDN % 8 == 0
    wt = w.T
    nsteps = (nj + 1) * ni
    cast_in, cast_out, cast_shapes = [], [], []
    for x in to_cast:
        rows, cols = x.shape
        rb = _cast_block_rows(rows, nsteps)
        imap = functools.partial(lambda j, i, last: (jnp.minimum(j * ni + i, last), 0), last=rows // rb - 1)
        cast_in.append(pl.BlockSpec((rb, cols), imap))
        cast_out.append(pl.BlockSpec((rb, cols), imap))
        cast_shapes.append(jax.ShapeDtypeStruct((rows, cols), jnp.bfloat16))
    src = lambda j: jnp.minimum(j, nj - 1)
    outs = pl.pallas_call(
        functools.partial(_in_proj_kernel, tn=tn, kc=kc, nj=nj, ncast=len(to_cast)),
        out_shape=[jax.ShapeDtypeStruct((T, n_big), jnp.bfloat16)] + cast_shapes,
        grid=(nj + 1, ni),
        in_specs=[pl.BlockSpec((tm, K), lambda j, i: (i, 0)),
                  pl.BlockSpec((tn, kc), lambda j, i: (src(j), i)),
                  pl.BlockSpec((LANES, kc), lambda j, i: ((src(j) + 1) * (tn // LANES), i))] + cast_in,
        out_specs=[pl.BlockSpec((tm, tn), lambda j, i: (jnp.where(j == 0, 0, i), jnp.maximum(j - 1, 0)))] + cast_out,
        scratch_shapes=[pltpu.VMEM((K, tn), jnp.bfloat16), pltpu.VMEM((K, tn), jnp.bfloat16),
                        pltpu.VMEM((tn + LANES, kc), jnp.float32)],
        compiler_params=pltpu.CompilerParams(dimension_semantics=("arbitrary", "arbitrary"),
                                             vmem_limit_bytes=IN_PROJ_VMEM_LIMIT),
        name="in_proj",
    )(h, wt, wt, *to_cast)
    return outs[0], outs[1:]


def small_group_weights(w, D, t=512):
    assert OFF_GQ % LANES == 0 and OFF_GATES % LANES == 0
    t = _tile(D, t)
    wt = w.T
    return pl.pallas_call(
        _regroup_small_kernel,
        out_shape=jax.ShapeDtypeStruct((D, LANES), jnp.bfloat16),
        grid=(D // t,),
        in_specs=[pl.BlockSpec((LANES, t), lambda k: (OFF_GQ // LANES, k)),
                  pl.BlockSpec((LANES, t), lambda k: (OFF_GATES // LANES, k))],
        out_specs=pl.BlockSpec((t, LANES), lambda k: (k, 0)),
        compiler_params=_cparams(("parallel",)),
        name="regroup_w_small",
    )(wt, wt)


def kernel(x, positions, mix_norm_w, w_in, q_norm_w, k_norm_w, idx_k_norm_w, conv_w, a_log, dt_bias, gdn_norm_w, w_proj_attn, w_proj_gdn, w_out, ffn_norm_w, w_router_group, b_router_group, w_router_expert, b_router_expert, w_gate, w_up, w_down):
    B, S, D = x.shape
    T = B * S
    bf16 = jnp.bfloat16
    moe_tm = 256 if T * TOP_K_EXPERTS >= 256 * N_EXPERTS else 64
    xt = x.reshape(T, D)
    cos_t, sin_t = rope_tables(positions)
    for l in range(w_in.shape[0]):
        w_small = small_group_weights(w_in[l], D)
        h = rmsnorm(xt, mix_norm_w[l], bf16)
        NE, _, FF = w_gate[l].shape
        later_weights = [w_proj_attn[l], w_proj_gdn[l], w_out[l], w_gate[l].reshape(NE * D, FF),
                         w_up[l].reshape(NE * D, FF), w_down[l].reshape(NE * FF, D)]
        P, (wpa, wpg, wo, wg, wu, wd) = in_proj_regrouping(h, w_in[l], later_weights, D)
        wg, wu, wd = wg.reshape(NE, D, FF), wu.reshape(NE, D, FF), wd.reshape(NE, FF, D)
        Psm = matmul(h, w_small, jnp.float32, name="in_proj_small")
        q, k, qi, ki, wi = attn_prep(P, Psm, cos_t, sin_t, q_norm_w[l], k_norm_w[l], idx_k_norm_w[l])
        bias = indexer_mask(qi, ki, wi, B, S)
        y_attn = masked_attention(q, k, P, bias, B, S)
        y_gdn = gated_deltanet(P, Psm, conv_w[l], a_log[l], dt_bias[l], gdn_norm_w[l], B, S)
        mixed = gated_merge(y_attn, y_gdn, wpa, wpg, P, D)
        x1 = out_proj_residual(mixed, wo, xt)
        w_router = jnp.zeros((D, LANES), jnp.float32)
        w_router = w_router.at[:, :N_GROUPS].set(w_router_group[l])
        w_router = w_router.at[:, RT_EXP:RT_EXP + N_EXPERTS].set(
            w_router_expert[l].transpose(1, 0, 2).reshape(D, N_EXPERTS))
        b_router = jnp.zeros((1, LANES), jnp.float32)
        b_router = b_router.at[0, :N_GROUPS].set(b_router_group[l])
        b_router = b_router.at[0, RT_EXP:RT_EXP + N_EXPERTS].set(b_router_expert[l].reshape(N_EXPERTS))
        h2, logits = ffn_norm_router(x1, ffn_norm_w[l], w_router)
        eid_l, wt_l = routing(logits, b_router)
        dest_row, row_token, tile_expert, n_used = moe_dispatch_plan(eid_l[:, :TOP_K_EXPERTS], moe_tm)
        ys = moe_ffn(h2, tile_expert, n_used, row_token, wg, wu, wd, moe_tm)
        xt = moe_combine(x1, wt_l, ys, dest_row)
    return xt.reshape(B, S, D)
```
